```python
import jax, jax.numpy as jnp
from jax import lax
import numpy as np

D_MODEL = 1024
BATCH = 8
SEQ = 2048
DEPTH = 4
DEC_BATCH = 32
DEC_SEQ = 1
PAST_LEN = 8192
PAGE_SIZE = 128

N_A = DEPTH // 2
N_B = DEPTH - N_A
D_FF = 4 * D_MODEL
RMS_EPS = 1e-6
HG_EXPAND = 128
HG_HEADS = D_MODEL // HG_EXPAND
HG_K = HG_EXPAND
HG_V = D_MODEL // HG_HEADS
HG_CHUNK = 64
NSA_HEADS = 16
NSA_HEAD_DIM = D_MODEL // NSA_HEADS
NSA_KV_GROUPS = 4
NSA_HPG = NSA_HEADS // NSA_KV_GROUPS
NSA_SCALE = NSA_HEAD_DIM ** -0.5
CMP_LEN = 32
CMP_STRIDE = 16
CMP_HID = NSA_HEAD_DIM
SLC_BLOCK = 64
SLC_TOPK = 16
WINDOW = 512
Q_BLOCK = 64
FORCED_SCORE = 1e4
NEG_INF = -1e30
TINY = 1e-30

kernel_name = "hgrn2_nsa_yoco_decoder_step"


def rmsnorm(x, g):
    xf = x.astype(jnp.float32)
    y = xf * lax.rsqrt(jnp.mean(xf * xf, axis=-1, keepdims=True) + RMS_EPS)
    return (y * g.astype(jnp.float32)).astype(x.dtype)


def sq_relu_mlp(h, w_up, w_down):
    return jnp.square(jax.nn.relu(h @ w_up)) @ w_down


def masked_softmax(s, mask):
    s = jnp.where(mask, s.astype(jnp.float32), NEG_INF)
    e = jnp.exp(s - jnp.max(s, axis=-1, keepdims=True)) * mask
    return e / jnp.maximum(jnp.sum(e, axis=-1, keepdims=True), TINY)


def hgrn_lower_bounds(lb_logits):
    cum = jnp.cumsum(jax.nn.softmax(lb_logits.astype(jnp.float32), axis=0), axis=0)
    return cum - cum[:1]


def hgrn_chunk(S0, q, k, v, logf):
    C = q.shape[1]
    b = jnp.cumsum(logf, axis=1)
    causal = jnp.tril(jnp.ones((C, C), dtype=bool))[None, :, :, None, None]
    dec = jnp.exp(jnp.where(causal, b[:, :, None] - b[:, None, :], -jnp.inf))
    a = jnp.einsum('btshk,bshk->bhts', q[:, :, None] * dec, k)
    o = jnp.einsum('bhts,bshv->bthv', a, v) + jnp.einsum('bthk,bhkv->bthv', q * jnp.exp(b), S0)
    b_end = b[:, -1]
    S = jnp.exp(b_end)[..., None] * S0 + jnp.einsum('bshk,bshv->bhkv', k * jnp.exp(b_end[:, None] - b), v)
    return S, o


def hgrn_mix(h, S0, w_in, lb, onorm, w_out):
    B, T, _ = h.shape
    hk, hv = HG_HEADS * HG_K, HG_HEADS * HG_V
    proj = (h @ w_in).astype(jnp.float32)
    zq, zf, zi, zg = jnp.split(proj, [hk, 2 * hk, 2 * hk + hv], axis=-1)
    lb = lb.reshape(HG_HEADS, HG_K)
    logf = jnp.logaddexp(jnp.log(lb), jnp.log1p(-lb) + jax.nn.log_sigmoid(zf.reshape(B, T, HG_HEADS, HG_K)))
    q = zq.reshape(B, T, HG_HEADS, HG_K)
    k = 1.0 - jnp.exp(logf)
    v = zi.reshape(B, T, HG_HEADS, HG_V)
    C = HG_CHUNK if T % HG_CHUNK == 0 else T
    n = T // C

    def to_chunks(a):
        return jnp.moveaxis(a.reshape(B, n, C, *a.shape[2:]), 1, 0)

    def step(S, xs):
        return hgrn_chunk(S, *xs)

    S, o = lax.scan(step, S0.astype(jnp.float32), (to_chunks(q), to_chunks(k), to_chunks(v), to_chunks(logf)))
    o = jnp.moveaxis(o, 0, 1).reshape(B, T, HG_HEADS, HG_V)
    o = o * lax.rsqrt(jnp.mean(o * o, axis=-1, keepdims=True) + RMS_EPS) * onorm.astype(jnp.float32).reshape(HG_HEADS, HG_V)
    o = o.reshape(B, T, hv) * jax.nn.silu(zg)
    return o.astype(h.dtype) @ w_out, S.astype(S0.dtype)


def compress(kv, pe, w1, w2):
    B, T, G, dk = kv.shape
    n_cmp = (T - CMP_LEN) // CMP_STRIDE + 1
    r = CMP_LEN // CMP_STRIDE
    sub = kv[:, :(n_cmp + r - 1) * CMP_STRIDE].reshape(B, n_cmp + r - 1, CMP_STRIDE, G, dk)
    blocks = jnp.concatenate([sub[:, i:i + n_cmp] for i in range(r)], axis=2)
    blocks = blocks + pe[None, None, :, None, :]
    flat = jnp.moveaxis(blocks, 3, 2).reshape(B, n_cmp, G, CMP_LEN * dk)
    return jax.nn.silu(flat @ w1) @ w2


def nsa_attend(q, gates, t_pos, kc, vc, ks_b, vs_b, M, kw, vw, w_pos):
    B, Q = q.shape[:2]
    qg = q.reshape(B, Q, NSA_KV_GROUPS, NSA_HPG, NSA_HEAD_DIM) * NSA_SCALE
    n_cmp = kc.shape[1]
    n_slc = ks_b.shape[2]
    e_pos = jnp.arange(n_cmp) * CMP_STRIDE + CMP_LEN - 1
    mask_c = (e_pos[None, :] <= t_pos[:, None])[None, :, None, None, :]
    p_c = masked_softmax(jnp.einsum('bqghd,bigd->bqghi', qg, kc), mask_c)
    o_c = jnp.einsum('bqghi,bigd->bqghd', p_c.astype(vc.dtype), vc)
    imp = jnp.einsum('bqghi,ij->bqgj', p_c, M)
    blk = jnp.arange(n_slc)[None, :]
    cur = (t_pos // SLC_BLOCK)[:, None]
    forced = (blk == 0) | (blk == cur) | (blk == cur - 1)
    score = jnp.where(forced[None, :, None], FORCED_SCORE, jnp.where((blk <= cur)[None, :, None], imp, -1.0))
    _, idx = lax.top_k(score, min(SLC_TOPK, n_slc))
    n_sel = idx.shape[-1]
    bi = jnp.arange(B)[:, None, None, None]
    gi = jnp.arange(NSA_KV_GROUPS)[None, None, :, None]
    k_sel = ks_b[bi, gi, idx]
    v_sel = vs_b[bi, gi, idx]
    pos_sel = idx[..., None] * SLC_BLOCK + jnp.arange(SLC_BLOCK)
    s_s = jnp.einsum('bqghd,bqgnrd->bqghnr', qg, k_sel).reshape(B, Q, NSA_KV_GROUPS, NSA_HPG, n_sel * SLC_BLOCK)
    mask_s = (pos_sel <= t_pos[None, :, None, None, None]).reshape(B, Q, NSA_KV_GROUPS, 1, n_sel * SLC_BLOCK)
    p_s = masked_softmax(s_s, mask_s).reshape(B, Q, NSA_KV_GROUPS, NSA_HPG, n_sel, SLC_BLOCK)
    o_s = jnp.einsum('bqghnr,bqgnrd->bqghd', p_s.astype(v_sel.dtype), v_sel)
    mask_w = (w_pos[None, :] <= t_pos[:, None]) & (w_pos[None, :] > t_pos[:, None] - WINDOW) & (w_pos[None, :] >= 0)
    p_w = masked_softmax(jnp.einsum('bqghd,bkgd->bqghk', qg, kw), mask_w[None, :, None, None, :])
    o_w = jnp.einsum('bqghk,bkgd->bqghd', p_w.astype(vw.dtype), vw)
    g = gates.reshape(B, Q, 3, NSA_KV_GROUPS, NSA_HPG)[..., None]
    o = g[:, :, 0] * o_c + g[:, :, 1] * o_s + g[:, :, 2] * o_w
    return o.reshape(B, Q, NSA_HEADS * NSA_HEAD_DIM)


def nsa_mix(h, ctx, attend, w_q, w_o):
    B, T, _ = h.shape
    nq = NSA_HEADS * NSA_HEAD_DIM
    proj = h @ w_q
    q = proj[..., :nq].reshape(B, T, NSA_HEADS, NSA_HEAD_DIM)
    gates = jax.nn.sigmoid(proj[..., nq:].astype(jnp.float32)).reshape(B, T, 3, NSA_HEADS)
    return attend(q, gates, ctx).astype(h.dtype) @ w_o


def setup_inputs(seed: int = 0) -> dict:
    key = jax.random.key(seed)
    ks = jax.random.split(key, 32)
    n_pages = PAST_LEN // PAGE_SIZE
    n_pool = (DEC_BATCH * n_pages * 5) // 4
    w_buf = min(WINDOW, PAST_LEN)
    G, dk = NSA_KV_GROUPS, NSA_HEAD_DIM
    hk, hv = HG_HEADS * HG_K, HG_HEADS * HG_V

    def nrm(k, shape, scale):
        return jax.random.normal(k, shape, jnp.float32) * scale

    return {
        "x_prompt": nrm(ks[0], (BATCH, SEQ, D_MODEL), 1.0),
        "x_sample": nrm(ks[1], (DEC_BATCH, DEC_SEQ, D_MODEL), 1.0),
        "cache_nsa_kv": nrm(ks[2], (n_pool, PAGE_SIZE, 4, G, dk), 1.0),
        "cache_win_kv": nrm(ks[3], (DEC_BATCH, w_buf, 2, G, dk), 1.0),
        "state_hgrn": nrm(ks[4], (N_A, DEC_BATCH, HG_HEADS, HG_K, HG_V), 0.5),
        "page_table": jax.random.permutation(ks[5], n_pool)[:DEC_BATCH * n_pages].reshape(DEC_BATCH, n_pages).astype(jnp.int32),
        "norm_mix": 1.0 + nrm(ks[6], (DEPTH, D_MODEL), 0.02),
        "norm_mlp": 1.0 + nrm(ks[7], (DEPTH, D_MODEL), 0.02),
        "w_mlp_up": nrm(ks[8], (DEPTH, D_MODEL, D_FF), D_MODEL ** -0.5),
        "w_mlp_down": nrm(ks[9], (DEPTH, D_FF, D_MODEL), D_FF ** -0.5),
        "w_hgrn_in": nrm(ks[10], (N_A, D_MODEL, 2 * hk + 2 * hv), D_MODEL ** -0.5),
        "hgrn_lb_logits": nrm(ks[11], (N_A, hk), 1.0),
        "hgrn_onorm": 1.0 + nrm(ks[12], (N_A, hv), 0.02),
        "w_hgrn_out": nrm(ks[13], (N_A, hv, D_MODEL), hv ** -0.5),
        "norm_kv": 1.0 + nrm(ks[14], (D_MODEL,), 0.02),
        "w_kv": nrm(ks[15], (D_MODEL, 6 * G * dk), D_MODEL ** -0.5),
        "cmp_pe_k": nrm(ks[16], (CMP_LEN, dk), 0.1),
        "cmp_w1_k": nrm(ks[17], (CMP_LEN * dk, CMP_HID), (CMP_LEN * dk) ** -0.5),
        "cmp_w2_k": nrm(ks[18], (CMP_HID, dk), CMP_HID ** -0.5),
        "cmp_pe_v": nrm(ks[19], (CMP_LEN, dk), 0.1),
        "cmp_w1_v": nrm(ks[20], (CMP_LEN * dk, CMP_HID), (CMP_LEN * dk) ** -0.5),
        "cmp_w2_v": nrm(ks[21], (CMP_HID, dk), CMP_HID ** -0.5),
        "w_nsa_q": nrm(ks[22], (N_B, D_MODEL, NSA_HEADS * dk + 3 * NSA_HEADS), D_MODEL ** -0.5),
        "w_nsa_out": nrm(ks[23], (N_B, NSA_HEADS * dk, D_MODEL), (NSA_HEADS * dk) ** -0.5),
        "norm_final": 1.0 + nrm(ks[24], (D_MODEL,), 0.02),
    }


def reference(x_prompt, x_sample, cache_nsa_kv, cache_win_kv, state_hgrn, page_table,
              norm_mix, norm_mlp, w_mlp_up, w_mlp_down,
              w_hgrn_in, hgrn_lb_logits, hgrn_onorm, w_hgrn_out,
              norm_kv, w_kv, cmp_pe_k, cmp_w1_k, cmp_w2_k, cmp_pe_v, cmp_w1_v, cmp_w2_v,
              w_nsa_q, w_nsa_out, norm_final):
    lbs = hgrn_lower_bounds(hgrn_lb_logits)
    G, dk = NSA_KV_GROUPS, NSA_HEAD_DIM

    def shared_kv(x):
        B, T, _ = x.shape
        return (rmsnorm(x, norm_kv) @ w_kv).reshape(B, T, 6, G, dk)

    def nsa_context(kv4):
        B, T = kv4.shape[:2]
        kc = compress(kv4[:, :, 0], cmp_pe_k, cmp_w1_k, cmp_w2_k)
        vc = compress(kv4[:, :, 1], cmp_pe_v, cmp_w1_v, cmp_w2_v)
        n_slc = -(-T // SLC_BLOCK)
        slc = jnp.pad(kv4[:, :, 2:4], ((0, 0), (0, n_slc * SLC_BLOCK - T), (0, 0), (0, 0), (0, 0)))
        slc = slc.reshape(B, n_slc, SLC_BLOCK, 2, G, dk).transpose(3, 0, 4, 1, 2, 5)
        ci = jnp.arange(kc.shape[1])[:, None] * CMP_STRIDE
        sj = jnp.arange(n_slc)[None, :] * SLC_BLOCK
        M = ((ci < sj + SLC_BLOCK) & (ci + CMP_LEN > sj)).astype(jnp.float32)
        return kc, vc, slc[0], slc[1], M

    def trunk(x, S_init, make_ctx, attend):
        new_S = []
        ctx = None
        new_kv = None
        for l in range(DEPTH):
            if l == N_A:
                ctx, new_kv = make_ctx(x)
            h = rmsnorm(x, norm_mix[l])
            if l < N_A:
                o, S = hgrn_mix(h, S_init[l], w_hgrn_in[l], lbs[l], hgrn_onorm[l], w_hgrn_out[l])
                new_S.append(S)
            else:
                o = nsa_mix(h, ctx, attend, w_nsa_q[l - N_A], w_nsa_out[l - N_A])
            x = x + o
            x = x + sq_relu_mlp(rmsnorm(x, norm_mlp[l]), w_mlp_up[l], w_mlp_down[l])
        return rmsnorm(x, norm_final), jnp.stack(new_S), new_kv

    def prompt_ctx(x):
        kv = shared_kv(x)
        pad = ((0, 0), (WINDOW, 0), (0, 0), (0, 0))
        kc, vc, ks_b, vs_b, M = nsa_context(kv[:, :, :4])
        ctx = (kc, vc, ks_b, vs_b, M, jnp.pad(kv[:, :, 4], pad), jnp.pad(kv[:, :, 5], pad))
        return ctx, (kv[:, :, :4], kv[:, -min(WINDOW, x.shape[1]):, 4:])

    def prompt_attend(q, gates, ctx):
        kc, vc, ks_b, vs_b, M, kw_pad, vw_pad = ctx
        B, T = q.shape[:2]
        nb = T // Q_BLOCK

        def to_blocks(a):
            return jnp.moveaxis(a.reshape(B, nb, Q_BLOCK, *a.shape[2:]), 1, 0)

        def one(args):
            qi, gi, s0 = args
            kw = lax.dynamic_slice_in_dim(kw_pad, s0, WINDOW + Q_BLOCK, axis=1)
            vw = lax.dynamic_slice_in_dim(vw_pad, s0, WINDOW + Q_BLOCK, axis=1)
            w_pos = s0 - WINDOW + jnp.arange(WINDOW + Q_BLOCK)
            t_pos = s0 + jnp.arange(Q_BLOCK)
            return nsa_attend(qi, gi, t_pos, kc, vc, ks_b, vs_b, M, kw, vw, w_pos)

        o = lax.map(one, (to_blocks(q), to_blocks(gates), jnp.arange(nb) * Q_BLOCK))
        return jnp.moveaxis(o, 0, 1).reshape(B, T, -1)

    past_len = page_table.shape[1] * cache_nsa_kv.shape[1]
    w_buf = cache_win_kv.shape[1]

    def sample_ctx(x):
        kv = shared_kv(x)
        B, T = x.shape[:2]
        past = cache_nsa_kv[page_table].reshape(B, past_len, 4, G, dk)
        full = jnp.concatenate([past, kv[:, :, :4].astype(past.dtype)], axis=1)
        win = jnp.concatenate([cache_win_kv, kv[:, :, 4:].astype(cache_win_kv.dtype)], axis=1)
        w_pos = jnp.concatenate([past_len - w_buf + jnp.arange(w_buf), past_len + jnp.arange(T)])
        kc, vc, ks_b, vs_b, M = nsa_context(full)
        ctx = (kc, vc, ks_b, vs_b, M, win[:, :, 0], win[:, :, 1], w_pos)
        return ctx, (kv[:, :, :4], win[:, -w_buf:])

    def sample_attend(q, gates, ctx):
        kc, vc, ks_b, vs_b, M, kw, vw, w_pos = ctx
        t_pos = past_len + jnp.arange(q.shape[1])
        return nsa_attend(q, gates, t_pos, kc, vc, ks_b, vs_b, M, kw, vw, w_pos)

    S0_prompt = jnp.zeros((N_A, x_prompt.shape[0], HG_HEADS, HG_K, HG_V), x_prompt.dtype)
    y_prompt, hgrn_state_prompt, (nsa_kv_prompt, win_kv_prompt) = trunk(x_prompt, S0_prompt, prompt_ctx, prompt_attend)
    y_sample, hgrn_state_sample, (nsa_kv_sample, win_kv_sample) = trunk(x_sample, state_hgrn, sample_ctx, sample_attend)
    return (y_prompt, y_sample, nsa_kv_prompt, nsa_kv_sample, win_kv_prompt, win_kv_sample, hgrn_state_prompt, hgrn_state_sample)
```

```python
import functools

import jax
import jax.numpy as jnp
from jax import lax
from jax.experimental import pallas as pl
from jax.experimental.pallas import tpu as pltpu

F32 = jnp.float32
BF16 = jnp.bfloat16

D_MODEL = 1024
DEPTH = 4
N_A = DEPTH // 2
D_FF = 4 * D_MODEL
RMS_EPS = 1e-6
HG_HEADS = 8
HG_K = 128
HG_V = 128
NSA_HEADS = 16
NSA_HEAD_DIM = 64
NSA_KV_GROUPS = 4
NSA_HPG = NSA_HEADS // NSA_KV_GROUPS
NSA_SCALE = NSA_HEAD_DIM ** -0.5
CMP_LEN = 32
CMP_STRIDE = 16
SLC_BLOCK = 64
SLC_TOPK = 16
WINDOW = 512
FORCED_SCORE = 1e4
NEG_INF = -1e30
TINY = 1e-30

HG_CHUNK = 64
HG_SUB = 16
VMEM_LIMIT = 56 * 1024 * 1024


def _cparams(*sem):
    return pltpu.CompilerParams(dimension_semantics=sem, vmem_limit_bytes=VMEM_LIMIT)


def _rms(x, g):
    return x * lax.rsqrt(jnp.mean(x * x, axis=-1, keepdims=True) + RMS_EPS) * g


def _sigmoid(x):
    return 1.0 / (1.0 + jnp.exp(-x))


def _dot(a, b):
    return jnp.dot(a, b, preferred_element_type=F32)


def _dot_nt(a, b):
    return lax.dot_general(a, b, (((1,), (1,)), ((), ())), preferred_element_type=F32)


def _dot_tn(a, b):
    return lax.dot_general(a, b, (((0,), (0,)), ((), ())), preferred_element_type=F32)


def _split3(x):
    hi = x.astype(BF16)
    r1 = x - hi.astype(F32)
    mid = r1.astype(BF16)
    lo = (r1 - mid.astype(F32)).astype(BF16)
    return hi, mid, lo


def _masked_softmax(s, mask):
    s = jnp.where(mask, s, NEG_INF)
    e = jnp.where(mask, jnp.exp(s - jnp.max(s, axis=-1, keepdims=True)), 0.0)
    return e / jnp.maximum(jnp.sum(e, axis=-1, keepdims=True), TINY)


def _rms_proj_kernel(x_ref, g_ref, w_ref, o_ref, y_ref):
    @pl.when(pl.program_id(1) == 0)
    def _():
        y_ref[...] = _rms(x_ref[...], g_ref[...]).astype(BF16)

    o_ref[...] = _dot(y_ref[...], w_ref[...].astype(BF16)).astype(o_ref.dtype)


def rms_proj(x, g, w, tm, tn, out_dtype=F32):
    M, D = x.shape
    N = w.shape[1]
    return pl.pallas_call(
        _rms_proj_kernel,
        grid=(M // tm, N // tn),
        in_specs=[pl.BlockSpec((tm, D), lambda i, j: (i, 0)),
                  pl.BlockSpec((1, D), lambda i, j: (0, 0)),
                  pl.BlockSpec((D, tn), lambda i, j: (0, j))],
        out_specs=pl.BlockSpec((tm, tn), lambda i, j: (i, j)),
        out_shape=jax.ShapeDtypeStruct((M, N), out_dtype),
        scratch_shapes=[pltpu.VMEM((tm, D), BF16)],
        compiler_params=_cparams("parallel", "arbitrary"),
        name="rms_proj",
    )(x, g.reshape(1, D), w)


def _proj_res_kernel(a_ref, w_ref, r_ref, o_ref, wb_ref):
    @pl.when(pl.program_id(0) == 0)
    def _():
        wb_ref[...] = w_ref[...].astype(BF16)

    o_ref[...] = r_ref[...] + _dot(a_ref[...].astype(BF16), wb_ref[...])


def proj_res(a, w, res, tm):
    M, K = a.shape
    N = w.shape[1]
    return pl.pallas_call(
        _proj_res_kernel,
        grid=(M // tm,),
        in_specs=[pl.BlockSpec((tm, K), lambda i: (i, 0)),
                  pl.BlockSpec((K, N), lambda i: (0, 0)),
                  pl.BlockSpec((tm, N), lambda i: (i, 0))],
        out_specs=pl.BlockSpec((tm, N), lambda i: (i, 0)),
        out_shape=jax.ShapeDtypeStruct((M, N), F32),
        scratch_shapes=[pltpu.VMEM((K, N), BF16)],
        compiler_params=_cparams("arbitrary"),
        name="proj_res",
    )(a, w, res)


def _mlp_kernel(x_ref, g_ref, wu_ref, wd_ref, gf_ref, o_ref, y_ref, acc_ref, *, final_norm):
    f = pl.program_id(1)

    @pl.when(f == 0)
    def _():
        y_ref[...] = _rms(x_ref[...], g_ref[...]).astype(BF16)
        acc_ref[...] = jnp.zeros_like(acc_ref)

    h = jnp.maximum(_dot(y_ref[...], wu_ref[...].astype(BF16)), 0.0)
    acc_ref[...] += _dot((h * h).astype(BF16), wd_ref[...].astype(BF16))

    @pl.when(f == pl.num_programs(1) - 1)
    def _():
        out = x_ref[...] + acc_ref[...]
        if final_norm:
            out = _rms(out, gf_ref[...])
        o_ref[...] = out


def mlp_res(x, g, w_up, w_down, g_final, tm, tf, final_norm):
    M, D = x.shape
    Fdim = w_up.shape[1]
    return pl.pallas_call(
        functools.partial(_mlp_kernel, final_norm=final_norm),
        grid=(M // tm, Fdim // tf),
        in_specs=[pl.BlockSpec((tm, D), lambda i, f: (i, 0)),
                  pl.BlockSpec((1, D), lambda i, f: (0, 0)),
                  pl.BlockSpec((D, tf), lambda i, f: (0, f)),
                  pl.BlockSpec((tf, D), lambda i, f: (f, 0)),
                  pl.BlockSpec((1, D), lambda i, f: (0, 0))],
        out_specs=pl.BlockSpec((tm, D), lambda i, f: (i, 0)),
        out_shape=jax.ShapeDtypeStruct((M, D), F32),
        scratch_shapes=[pltpu.VMEM((tm, D), BF16), pltpu.VMEM((tm, D), F32)],
        compiler_params=_cparams("parallel", "arbitrary"),
        name="mlp_res",
    )(x, g.reshape(1, D), w_up, w_down, g_final.reshape(1, D))


def _hgrn_lower_bound(lg, layer):
    m = jnp.max(lg, axis=0, keepdims=True)
    e = jnp.exp(lg - m)
    p = e / jnp.sum(e, axis=0, keepdims=True)
    lb = jnp.sum(p[1:layer + 1], axis=0, keepdims=True)
    return jnp.log(lb), jnp.log(1.0 - lb)


def _hgrn_logf(z, lg, layer):
    ls = jnp.minimum(z, 0.0) - jnp.log(1.0 + jnp.exp(-jnp.abs(z)))
    if layer == 0:
        return ls
    log_lb, log1m = _hgrn_lower_bound(lg, layer)
    b2 = log1m + ls
    return jnp.maximum(log_lb, b2) + jnp.log(1.0 + jnp.exp(-jnp.abs(log_lb - b2)))


def _hgrn_kernel(zq_ref, zf_ref, zi_ref, zg_ref, lg_ref, on_ref, o_ref, s_ref,
                 st_ref, b_ref, oa_ref, *, layer, tc):
    t = pl.program_id(2)

    @pl.when(t == 0)
    def _():
        st_ref[...] = jnp.zeros_like(st_ref)

    q = zq_ref[0]
    v = zi_ref[0]
    logf = _hgrn_logf(zf_ref[0], lg_ref[...], layer)
    k = 1.0 - jnp.exp(logf)

    C, c = HG_CHUNK, HG_SUB
    r_i = lax.broadcasted_iota(jnp.int32, (C, C), 0)
    c_i = lax.broadcasted_iota(jnp.int32, (C, C), 1)
    tril = jnp.where(r_i >= c_i, 1.0, 0.0).astype(BF16)

    for ci in range(tc // C):
        sl = slice(ci * C, (ci + 1) * C)
        hi, mid, lo = _split3(logf[sl])
        b = _dot(tril, hi) + _dot(tril, mid) + _dot(tril, lo)
        b_ref[sl, :] = b
        qc, kc, vc = q[sl], k[sl], v[sl]
        vcb = vc.astype(BF16)
        st = st_ref[...]
        o = _dot_nt((qc * jnp.exp(b)).astype(BF16), st.astype(BF16))
        parts = [jnp.zeros((c, HG_V), F32)]
        for i in range(1, C // c):
            r = b[i * c:i * c + 1]
            qt = (qc[i * c:(i + 1) * c] * jnp.exp(b[i * c:(i + 1) * c] - r)).astype(BF16)
            kt = (kc[:i * c] * jnp.exp(r - b[:i * c])).astype(BF16)
            a = _dot_nt(qt, kt)
            parts.append(_dot(a.astype(BF16), vcb[:i * c]))
        oa_ref[sl, :] = o + jnp.concatenate(parts, axis=0)
        b_end = b[C - 1:C]
        kd = (kc * jnp.exp(b_end - b)).astype(BF16)
        st_ref[...] = st * jnp.exp(b_end) + _dot_tn(vcb, kd)

    b_all = b_ref[...]
    acc = oa_ref[...]
    sub = lax.broadcasted_iota(jnp.int32, (tc, HG_K), 0) % c
    ones = jnp.ones((HG_K, HG_V), BF16)
    for d in range(c):
        if d == 0:
            p = q * k
            vs = v
        else:
            e = jnp.exp(b_all - pltpu.roll(b_all, d, axis=0))
            p = jnp.where(sub >= d, q * pltpu.roll(k, d, axis=0) * e, 0.0)
            vs = pltpu.roll(v, d, axis=0)
        acc = acc + _dot(p.astype(BF16), ones) * vs

    o = acc * lax.rsqrt(jnp.mean(acc * acc, axis=-1, keepdims=True) + RMS_EPS) * on_ref[...]
    zg = zg_ref[0]
    o_ref[0] = (o * (zg * _sigmoid(zg))).astype(o_ref.dtype)

    @pl.when(t == pl.num_programs(2) - 1)
    def _():
        s_ref[0, 0] = st_ref[...].T


def hgrn_prompt(proj, lb_logits, onorm, layer, tc=256):
    B, T, _ = proj.shape
    H = HG_HEADS
    return pl.pallas_call(
        functools.partial(_hgrn_kernel, layer=layer, tc=tc),
        grid=(B, H, T // tc),
        in_specs=[pl.BlockSpec((1, tc, HG_K), lambda b, h, t: (b, t, h)),
                  pl.BlockSpec((1, tc, HG_K), lambda b, h, t: (b, t, H + h)),
                  pl.BlockSpec((1, tc, HG_V), lambda b, h, t: (b, t, 2 * H + h)),
                  pl.BlockSpec((1, tc, HG_V), lambda b, h, t: (b, t, 3 * H + h)),
                  pl.BlockSpec((N_A, HG_K), lambda b, h, t: (0, h)),
                  pl.BlockSpec((1, HG_V), lambda b, h, t: (0, h))],
        out_specs=[pl.BlockSpec((1, tc, HG_V), lambda b, h, t: (b, t, h)),
                   pl.BlockSpec((1, 1, HG_K, HG_V), lambda b, h, t: (b, h, 0, 0))],
        out_shape=[jax.ShapeDtypeStruct((B, T, H * HG_V), BF16),
                   jax.ShapeDtypeStruct((B, H, HG_K, HG_V), F32)],
        scratch_shapes=[pltpu.VMEM((HG_V, HG_K), F32),
                        pltpu.VMEM((tc, HG_K), F32),
                        pltpu.VMEM((tc, HG_V), F32)],
        compiler_params=_cparams("parallel", "parallel", "arbitrary"),
        name="hgrn_prompt",
    )(proj, proj, proj, proj, lb_logits, onorm.reshape(1, H * HG_V))


def _kv_proj_kernel(x_ref, g_ref, w_ref, nsa_ref, win_ref, hm_ref, wb_ref):
    @pl.when(pl.program_id(0) == 0)
    def _():
        wb_ref[...] = w_ref[...].astype(BF16)

    y = _rms(x_ref[...], g_ref[...]).astype(BF16)
    kv = _dot(y, wb_ref[...])
    n_nsa = nsa_ref.shape[1]
    nsa_ref[...] = kv[:, :n_nsa]
    win_ref[...] = kv[:, n_nsa:]
    for n in range(hm_ref.shape[1]):
        hm_ref[0, n] = kv[:, n * NSA_HEAD_DIM:(n + 1) * NSA_HEAD_DIM].astype(BF16)


def kv_proj_prompt(x, g, w_kv, B, T, tm=512):
    M, D = x.shape
    N = w_kv.shape[1]
    nh = N // NSA_HEAD_DIM
    n_nsa = 4 * NSA_KV_GROUPS * NSA_HEAD_DIM
    tpb = T // tm
    return pl.pallas_call(
        _kv_proj_kernel,
        grid=(M // tm,),
        in_specs=[pl.BlockSpec((tm, D), lambda i: (i, 0)),
                  pl.BlockSpec((1, D), lambda i: (0, 0)),
                  pl.BlockSpec((D, N), lambda i: (0, 0))],
        out_specs=[pl.BlockSpec((tm, n_nsa), lambda i: (i, 0)),
                   pl.BlockSpec((tm, N - n_nsa), lambda i: (i, 0)),
                   pl.BlockSpec((1, nh, tm, NSA_HEAD_DIM), lambda i: (i // tpb, 0, i % tpb, 0))],
        out_shape=[jax.ShapeDtypeStruct((M, n_nsa), F32),
                   jax.ShapeDtypeStruct((M, N - n_nsa), F32),
                   jax.ShapeDtypeStruct((B, nh, T, NSA_HEAD_DIM), BF16)],
        scratch_shapes=[pltpu.VMEM((D, N), BF16)],
        compiler_params=_cparams("arbitrary"),
        name="kv_proj",
    )(x, g.reshape(1, D), w_kv)


def _q_proj_kernel(x_ref, g_ref, w_ref, q_ref, gt_ref, wb_ref):
    @pl.when(pl.program_id(0) == 0)
    def _():
        wb_ref[...] = w_ref[...].astype(BF16)

    y = _rms(x_ref[...], g_ref[...]).astype(BF16)
    pr = _dot(y, wb_ref[...])
    nq = NSA_HEADS * NSA_HEAD_DIM
    for h in range(NSA_HEADS):
        q_ref[0, h] = (pr[:, h * NSA_HEAD_DIM:(h + 1) * NSA_HEAD_DIM] * NSA_SCALE).astype(BF16)
    gates = _sigmoid(pr[:, nq:])
    ng = 3 * NSA_HPG
    for gi in range(NSA_KV_GROUPS):
        gt_ref[0, gi] = gates[:, gi * ng:(gi + 1) * ng]


def _permute_gate_cols(w_q):
    nq = NSA_HEADS * NSA_HEAD_DIM
    wg = w_q[:, nq:].reshape(-1, 3, NSA_KV_GROUPS, NSA_HPG).transpose(0, 2, 1, 3).reshape(-1, 3 * NSA_HEADS)
    return jnp.concatenate([w_q[:, :nq], wg], axis=1)


def q_proj_prompt(x, g, w_qp, B, T, tm=512):
    M, D = x.shape
    N = w_qp.shape[1]
    tpb = T // tm
    ng = 3 * NSA_HPG
    return pl.pallas_call(
        _q_proj_kernel,
        grid=(M // tm,),
        in_specs=[pl.BlockSpec((tm, D), lambda i: (i, 0)),
                  pl.BlockSpec((1, D), lambda i: (0, 0)),
                  pl.BlockSpec((D, N), lambda i: (0, 0))],
        out_specs=[pl.BlockSpec((1, NSA_HEADS, tm, NSA_HEAD_DIM), lambda i: (i // tpb, 0, i % tpb, 0)),
                   pl.BlockSpec((1, NSA_KV_GROUPS, tm, ng), lambda i: (i // tpb, 0, i % tpb, 0))],
        out_shape=[jax.ShapeDtypeStruct((B, NSA_HEADS, T, NSA_HEAD_DIM), BF16),
                   jax.ShapeDtypeStruct((B, NSA_KV_GROUPS, T, ng), F32)],
        scratch_shapes=[pltpu.VMEM((D, N), BF16)],
        compiler_params=_cparams("arbitrary"),
        name="q_proj",
    )(x, g.reshape(1, D), w_qp)


def _cmp_weights(pe_k, w1_k, pe_v, w1_v, w2_k, w2_v):
    half = CMP_STRIDE * NSA_HEAD_DIM

    def ab(w1):
        return jnp.concatenate([w1[:half], w1[half:]], axis=1)

    wab = jnp.stack([ab(w1_k), ab(w1_v)])
    pe = jnp.stack([pe_k.reshape(2, half), pe_v.reshape(2, half)])
    w2 = jnp.stack([w2_k, w2_v])
    return wab, pe, w2


def _cmp_hidden(ab, pe, wab):
    hd = NSA_HEAD_DIM
    n = ab.shape[0]
    nxt = pltpu.roll(ab, n - 1, axis=0)
    pt = _dot(pe.astype(BF16), wab)
    hid = ab[:, :hd] + nxt[:, hd:] + pt[0:1, :hd] + pt[1:2, hd:]
    return hid * _sigmoid(hid)


def _cmp_prompt_kernel(x_ref, wab_ref, pe_ref, w2_ref, o_ref):
    wab = wab_ref[0].astype(BF16)
    ab = _dot(x_ref[0, 0], wab)
    act = _cmp_hidden(ab, pe_ref[0], wab)
    o_ref[0, 0] = _dot(act.astype(BF16), w2_ref[0].astype(BF16)).astype(o_ref.dtype)


def compress_prompt(kvh, wab, pe, w2):
    B, _, T, hd = kvh.shape
    ns = T // CMP_STRIDE
    G = NSA_KV_GROUPS
    x = kvh.reshape(B, kvh.shape[1], ns, CMP_STRIDE * hd)
    return pl.pallas_call(
        _cmp_prompt_kernel,
        grid=(B, 2 * G),
        in_specs=[pl.BlockSpec((1, 1, ns, CMP_STRIDE * hd), lambda b, n: (b, n, 0, 0)),
                  pl.BlockSpec((1, CMP_STRIDE * hd, 2 * hd), lambda b, n: (n // G, 0, 0)),
                  pl.BlockSpec((1, 2, CMP_STRIDE * hd), lambda b, n: (n // G, 0, 0)),
                  pl.BlockSpec((1, hd, hd), lambda b, n: (n // G, 0, 0))],
        out_specs=pl.BlockSpec((1, 1, ns, hd), lambda b, n: (b, n, 0, 0)),
        out_shape=jax.ShapeDtypeStruct((B, 2 * G, ns, hd), BF16),
        compiler_params=_cparams("parallel", "parallel"),
        name="compress_prompt",
    )(x, wab, pe, w2)


def _nsa_prompt_kernel(q_ref, gt_ref, kc_ref, vc_ref, ks_ref, vs_ref, kw_ref, vw_ref, o_ref, *, tq, ck):
    i = pl.program_id(2)
    s0 = i * tq
    hpg, hd = NSA_HPG, NSA_HEAD_DIM
    R = hpg * tq
    T = ks_ref.shape[2]
    n_cmp = kc_ref.shape[2]
    n_slc = T // SLC_BLOCK
    Q = q_ref[0].reshape(R, hd)

    def tpos(shape):
        return s0 + lax.broadcasted_iota(jnp.int32, shape, 0) % tq

    sc = _dot_nt(Q, kc_ref[0, 0])
    e_pos = lax.broadcasted_iota(jnp.int32, (R, n_cmp), 1) * CMP_STRIDE + (CMP_LEN - 1)
    p_c = _masked_softmax(sc, e_pos <= tpos((R, n_cmp)))
    o_c = _dot(p_c.astype(BF16), vc_ref[0, 0])

    psum = p_c[0:tq]
    for h in range(1, hpg):
        psum = psum + p_c[h * tq:(h + 1) * tq]
    ci = lax.broadcasted_iota(jnp.int32, (n_cmp, n_slc), 0) * CMP_STRIDE
    sj = lax.broadcasted_iota(jnp.int32, (n_cmp, n_slc), 1) * SLC_BLOCK
    ov = jnp.where((ci < sj + SLC_BLOCK) & (ci + CMP_LEN > sj), 1.0, 0.0).astype(BF16)
    hi, mid, lo = _split3(psum)
    imp = _dot(hi, ov) + _dot(mid, ov) + _dot(lo, ov)
    blk = lax.broadcasted_iota(jnp.int32, (tq, n_slc), 1)
    cur = (s0 + lax.broadcasted_iota(jnp.int32, (tq, n_slc), 0)) // SLC_BLOCK
    forced = (blk == 0) | (blk == cur) | (blk == cur - 1)
    score = jnp.where(forced, FORCED_SCORE, jnp.where(blk <= cur, imp, -1.0))
    rank = jnp.zeros((tq, n_slc), F32)
    for j in range(n_slc):
        cj = score[:, j:j + 1]
        rank = rank + jnp.where((cj > score) | ((cj == score) & (blk > j)), 1.0, 0.0)
    sel = jnp.where(rank < SLC_TOPK, 1.0, 0.0).astype(BF16)

    bpc = ck // SLC_BLOCK

    def body(c, carry):
        m, l, acc = carry
        k0 = pl.multiple_of(c * ck, ck)
        kk = ks_ref[0, 0, pl.ds(k0, ck), :]
        vv = vs_ref[0, 0, pl.ds(k0, ck), :]
        s = _dot_nt(Q, kk).reshape(hpg, tq, ck)
        ex = jnp.where(lax.broadcasted_iota(jnp.int32, (n_slc, ck), 0)
                       == c * bpc + lax.broadcasted_iota(jnp.int32, (n_slc, ck), 1) // SLC_BLOCK, 1.0, 0.0)
        selk = _dot(sel, ex.astype(BF16))
        kpos = k0 + lax.broadcasted_iota(jnp.int32, (tq, ck), 1)
        ok = (selk > 0.5) & (kpos <= s0 + lax.broadcasted_iota(jnp.int32, (tq, ck), 0))
        ok = jnp.broadcast_to(ok[None], (hpg, tq, ck))
        s = jnp.where(ok, s, NEG_INF)
        m_new = jnp.maximum(m, jnp.max(s, axis=-1, keepdims=True))
        p = jnp.where(ok, jnp.exp(s - m_new), 0.0)
        alpha = jnp.exp(m - m_new)
        l = alpha * l + jnp.sum(p, axis=-1, keepdims=True)
        pv = _dot(p.reshape(R, ck).astype(BF16), vv).reshape(hpg, tq, hd)
        return m_new, l, alpha * acc + pv

    n_chunks = (s0 + tq + ck - 1) // ck
    m0 = jnp.full((hpg, tq, 1), NEG_INF, F32)
    m, l, acc = lax.fori_loop(0, n_chunks, body, (m0, jnp.zeros((hpg, tq, 1), F32), jnp.zeros((hpg, tq, hd), F32)))
    o_s = (acc / jnp.maximum(l, TINY)).reshape(R, hd)

    wk = WINDOW + tq
    ws = pl.multiple_of(jnp.maximum(s0 - WINDOW, 0), tq)
    sw = _dot_nt(Q, kw_ref[0, 0, pl.ds(ws, wk), :])
    wpos = ws + lax.broadcasted_iota(jnp.int32, (R, wk), 1)
    tp = tpos((R, wk))
    p_w = _masked_softmax(sw, (wpos <= tp) & (wpos > tp - WINDOW))
    o_w = _dot(p_w.astype(BF16), vw_ref[0, 0, pl.ds(ws, wk), :])

    gt = gt_ref[0, 0]
    for h in range(hpg):
        rows = slice(h * tq, (h + 1) * tq)
        o_h = (gt[:, h:h + 1] * o_c[rows] + gt[:, hpg + h:hpg + h + 1] * o_s[rows]
               + gt[:, 2 * hpg + h:2 * hpg + h + 1] * o_w[rows])
        o_ref[0, :, h * hd:(h + 1) * hd] = o_h.astype(o_ref.dtype)


def nsa_prompt(q_hm, gates, cmp, kvh, tq=128, ck=256):
    B, _, T, hd = q_hm.shape
    G, hpg = NSA_KV_GROUPS, NSA_HPG
    n_cmp = cmp.shape[2]
    ng = gates.shape[3]
    full = lambda off: pl.BlockSpec((1, 1, T, hd), lambda b, g, i: (b, off + g, 0, 0))
    return pl.pallas_call(
        functools.partial(_nsa_prompt_kernel, tq=tq, ck=ck),
        grid=(B, G, T // tq),
        in_specs=[pl.BlockSpec((1, hpg, tq, hd), lambda b, g, i: (b, g, i, 0)),
                  pl.BlockSpec((1, 1, tq, ng), lambda b, g, i: (b, g, i, 0)),
                  pl.BlockSpec((1, 1, n_cmp, hd), lambda b, g, i: (b, g, 0, 0)),
                  pl.BlockSpec((1, 1, n_cmp, hd), lambda b, g, i: (b, G + g, 0, 0)),
                  full(2 * G), full(3 * G), full(4 * G), full(5 * G)],
        out_specs=pl.BlockSpec((1, tq, hpg * hd), lambda b, g, i: (b, i, g)),
        out_shape=jax.ShapeDtypeStruct((B, T, NSA_HEADS * hd), BF16),
        compiler_params=_cparams("parallel", "parallel", "arbitrary"),
        name="nsa_prompt",
    )(q_hm, gates, cmp, cmp, kvh, kvh, kvh, kvh)


def _row_to_col(row):
    n = row.shape[1]
    eye = lax.broadcasted_iota(jnp.int32, (n, n), 0) == lax.broadcasted_iota(jnp.int32, (n, n), 1)
    return jnp.sum(jnp.where(eye, jnp.broadcast_to(row, (n, n)), 0.0), axis=-1, keepdims=True)


def _hgrn_step_kernel(z_ref, s0_ref, lg_ref, on_ref, o_ref, s_ref, *, layer):
    hk = HG_HEADS * HG_K
    hv = HG_HEADS * HG_V
    for h in range(HG_HEADS):
        kl = slice(h * HG_K, (h + 1) * HG_K)
        vl = slice(h * HG_V, (h + 1) * HG_V)
        q = z_ref[0, :, kl]
        logf = _hgrn_logf(z_ref[0, :, hk + h * HG_K:hk + (h + 1) * HG_K], lg_ref[:, kl], layer)
        f = jnp.exp(logf)
        v = z_ref[0, :, 2 * hk + h * HG_V:2 * hk + (h + 1) * HG_V]
        zg = z_ref[0, :, 2 * hk + hv + h * HG_V:2 * hk + hv + (h + 1) * HG_V]
        s = _row_to_col(f) * s0_ref[0, h] + _row_to_col(1.0 - f) * v
        s_ref[0, h] = s
        o = jnp.sum(_row_to_col(q) * s, axis=0, keepdims=True)
        o = o * lax.rsqrt(jnp.mean(o * o, axis=-1, keepdims=True) + RMS_EPS) * on_ref[:, vl]
        o_ref[0, :, vl] = o * (zg * _sigmoid(zg))


def hgrn_step(proj, s0, lb_logits, onorm, layer):
    B = proj.shape[0]
    H = HG_HEADS
    return pl.pallas_call(
        functools.partial(_hgrn_step_kernel, layer=layer),
        grid=(B,),
        in_specs=[pl.BlockSpec((1, 1, proj.shape[2]), lambda b: (b, 0, 0)),
                  pl.BlockSpec((1, H, HG_K, HG_V), lambda b: (b, 0, 0, 0)),
                  pl.BlockSpec((N_A, H * HG_K), lambda b: (0, 0)),
                  pl.BlockSpec((1, H * HG_V), lambda b: (0, 0))],
        out_specs=[pl.BlockSpec((1, 1, H * HG_V), lambda b: (b, 0, 0)),
                   pl.BlockSpec((1, H, HG_K, HG_V), lambda b: (b, 0, 0, 0))],
        out_shape=[jax.ShapeDtypeStruct((B, 1, H * HG_V), F32),
                   jax.ShapeDtypeStruct((B, H, HG_K, HG_V), F32)],
        compiler_params=_cparams("parallel"),
        name="hgrn_step",
    )(proj, s0, lb_logits, onorm.reshape(1, H * HG_V))


def _cmp_big_weights(w1_k, w1_v):
    def one(w1):
        w = w1.reshape(2, CMP_STRIDE, NSA_HEAD_DIM, -1)
        big = jnp.einsum("alds,gh->lgdhas", w, jnp.eye(2, dtype=w1.dtype))
        return big.reshape(CMP_STRIDE * 2 * NSA_HEAD_DIM, 2 * 2 * w.shape[-1])

    return jnp.stack([one(w1_k), one(w1_v)])


def _cmp_sample_kernel(pt_ref, cache_ref, wbig_ref, wab_ref, pe_ref, w2_ref, o_ref, buf_ref, sem, *, n_pages):
    b = pl.program_id(0)
    G, hd = NSA_KV_GROUPS, NSA_HEAD_DIM
    n_cb = buf_ref.shape[0]
    lanes = buf_ref.shape[3]

    def page_copy(n):
        p, cb = n // n_cb, n % n_cb
        return pltpu.make_async_copy(cache_ref.at[pt_ref[b, p], :, pl.ds(cb * lanes, lanes)], buf_ref.at[cb, p], sem)

    def start(n, c):
        page_copy(n).start()
        return c

    def wait(n, c):
        page_copy(n).wait()
        return c

    lax.fori_loop(0, n_pages * n_cb, start, 0)
    lax.fori_loop(0, n_pages * n_cb, wait, 0)

    spp = buf_ref.shape[2] // CMP_STRIDE
    ns = n_pages * spp
    for c in range(2):
        wbig = wbig_ref[c].astype(BF16)
        wab = wab_ref[c].astype(BF16)
        w2 = w2_ref[c].astype(BF16)
        for gp in range(G // 2):
            off = c * G * hd + gp * 2 * hd
            cb = off // lanes
            x = jnp.concatenate(
                [buf_ref[cb, :, pl.ds(l, spp, stride=CMP_STRIDE), :].reshape(ns, lanes).astype(BF16)
                 for l in range(CMP_STRIDE)], axis=1)
            ab2 = _dot(x, wbig)
            for gl in range(2):
                act = _cmp_hidden(ab2[:, gl * 2 * hd:(gl + 1) * 2 * hd], pe_ref[c], wab)
                col = off + gl * hd
                o_ref[0, :, col:col + hd] = _dot(act.astype(BF16), w2).astype(o_ref.dtype)


def compress_sample(cache, page_table, wbig, wab, pe, w2):
    B, n_pages = page_table.shape
    page = cache.shape[1]
    G, hd = NSA_KV_GROUPS, NSA_HEAD_DIM
    ns = n_pages * page // CMP_STRIDE
    grid_spec = pltpu.PrefetchScalarGridSpec(
        num_scalar_prefetch=1,
        grid=(B,),
        in_specs=[pl.BlockSpec(memory_space=pl.ANY),
                  pl.BlockSpec(wbig.shape, lambda b, pt: (0, 0, 0)),
                  pl.BlockSpec(wab.shape, lambda b, pt: (0, 0, 0)),
                  pl.BlockSpec(pe.shape, lambda b, pt: (0, 0, 0)),
                  pl.BlockSpec(w2.shape, lambda b, pt: (0, 0, 0))],
        out_specs=pl.BlockSpec((1, ns, 2 * G * hd), lambda b, pt: (b, 0, 0)),
        scratch_shapes=[pltpu.VMEM((G, n_pages, page, 2 * hd), F32), pltpu.SemaphoreType.DMA(())],
    )
    return pl.pallas_call(
        functools.partial(_cmp_sample_kernel, n_pages=n_pages),
        grid_spec=grid_spec,
        out_shape=jax.ShapeDtypeStruct((B, ns, 2 * G * hd), BF16),
        compiler_params=_cparams("arbitrary"),
        name="compress_sample",
    )(page_table, cache, wbig, wab, pe, w2)


def _group_queries(pr_ref, g):
    hd = NSA_HEAD_DIM
    rows = [pr_ref[0, :, (g * NSA_HPG + h) * hd:(g * NSA_HPG + h + 1) * hd] for h in range(NSA_HPG)]
    return jnp.concatenate(rows, axis=0) * NSA_SCALE


def _nsa_sample_select_kernel(pr_ref, cmp_ref, oc_ref, idx_ref, *, t_pos, n_slc, n_pad):
    G, hpg, hd = NSA_KV_GROUPS, NSA_HPG, NSA_HEAD_DIM
    n_cmp = cmp_ref.shape[1]
    cmp = cmp_ref[0]
    ci = lax.broadcasted_iota(jnp.int32, (n_cmp, n_pad), 0) * CMP_STRIDE
    sj = lax.broadcasted_iota(jnp.int32, (n_cmp, n_pad), 1) * SLC_BLOCK
    ov = jnp.where((ci < sj + SLC_BLOCK) & (ci + CMP_LEN > sj), 1.0, 0.0).astype(BF16)
    blk = lax.broadcasted_iota(jnp.int32, (1, n_pad), 1)
    cur = t_pos // SLC_BLOCK
    forced = (blk == 0) | (blk == cur) | (blk == cur - 1)
    jr = lax.broadcasted_iota(jnp.int32, (n_pad, n_pad), 0)
    jc = lax.broadcasted_iota(jnp.int32, (n_pad, n_pad), 1)
    for g in range(G):
        qg = _group_queries(pr_ref, g).astype(BF16)
        sc = _dot_nt(qg, cmp[:, g * hd:(g + 1) * hd])
        e_pos = lax.broadcasted_iota(jnp.int32, (hpg, n_cmp), 1) * CMP_STRIDE + (CMP_LEN - 1)
        p_c = _masked_softmax(sc, e_pos <= t_pos)
        o_c = _dot(p_c.astype(BF16), cmp[:, (G + g) * hd:(G + g + 1) * hd])
        for h in range(hpg):
            col = (g * hpg + h) * hd
            oc_ref[0, :, col:col + hd] = o_c[h:h + 1]
        hi, mid, lo = _split3(jnp.sum(p_c, axis=0, keepdims=True))
        imp = _dot(hi, ov) + _dot(mid, ov) + _dot(lo, ov)
        score = jnp.where(forced, FORCED_SCORE, jnp.where(blk <= cur, imp, -1.0))
        score = jnp.where(blk < n_slc, score, -2.0)
        col_s = _row_to_col(score)
        beats = (col_s > score) | ((col_s == score) & (jr < jc))
        rank = jnp.sum(jnp.where(beats, 1.0, 0.0), axis=0, keepdims=True)
        rr = lax.broadcasted_iota(jnp.int32, (SLC_TOPK, n_pad), 0).astype(F32)
        bsel = jnp.where(jnp.broadcast_to(rank, (SLC_TOPK, n_pad)) == rr,
                         lax.broadcasted_iota(jnp.int32, (SLC_TOPK, n_pad), 1).astype(F32), 0.0)
        idx_ref[0, g * SLC_TOPK:(g + 1) * SLC_TOPK, :] = jnp.sum(bsel, axis=-1, keepdims=True).astype(jnp.int32)


def nsa_sample_select(proj, cmp_s, t_pos, n_slc):
    B = proj.shape[0]
    n_pad = -(-n_slc // 128) * 128
    G = NSA_KV_GROUPS
    nq = NSA_HEADS * NSA_HEAD_DIM
    return pl.pallas_call(
        functools.partial(_nsa_sample_select_kernel, t_pos=t_pos, n_slc=n_slc, n_pad=n_pad),
        grid=(B,),
        in_specs=[pl.BlockSpec((1, 1, proj.shape[2]), lambda b: (b, 0, 0)),
                  pl.BlockSpec((1,) + cmp_s.shape[1:], lambda b: (b, 0, 0))],
        out_specs=[pl.BlockSpec((1, 1, nq), lambda b: (b, 0, 0)),
                   pl.BlockSpec((1, G * SLC_TOPK, 1), lambda b: (b, 0, 0))],
        out_shape=[jax.ShapeDtypeStruct((B, 1, nq), F32),
                   jax.ShapeDtypeStruct((B, G * SLC_TOPK, 1), jnp.int32)],
        compiler_params=_cparams("parallel"),
        name="nsa_sample_select",
    )(proj, cmp_s)


def _nsa_sample_attend_kernel(pt_ref, idx_ref, pr_ref, oc_ref, kvn_ref, win_ref, cache_ref, o_ref, buf_ref, sem,
                              *, t_pos, past_len):
    b = pl.program_id(0)
    G, hpg, hd = NSA_KV_GROUPS, NSA_HPG, NSA_HEAD_DIM
    n_sel = G * SLC_TOPK
    slc_off = 2 * G * hd
    bpp = cache_ref.shape[1] // SLC_BLOCK
    new_blk = past_len // SLC_BLOCK

    def blk_copy(n):
        j = jnp.minimum(idx_ref[b, n], new_blk - 1)
        src = cache_ref.at[pt_ref[b, j // bpp], pl.ds((j % bpp) * SLC_BLOCK, SLC_BLOCK), pl.ds(slc_off, slc_off)]
        return pltpu.make_async_copy(src, buf_ref.at[n], sem)

    def start(n, c):
        blk_copy(n).start()
        return c

    def wait(n, c):
        blk_copy(n).wait()
        return c

    lax.fori_loop(0, n_sel, start, 0)
    lax.fori_loop(0, n_sel, wait, 0)

    nk = SLC_TOPK * SLC_BLOCK
    w_buf = win_ref.shape[1]
    gates = _sigmoid(pr_ref[0, :, NSA_HEADS * hd:])
    kvn = kvn_ref[0]

    def new_row(kind, g):
        return kvn[:, (kind * G + g) * hd:(kind * G + g + 1) * hd].astype(BF16).astype(F32)

    def attend_with_new(qg, s, ok, v, k_new, v_new):
        s_new = jnp.sum(qg.astype(F32) * k_new, axis=-1, keepdims=True)
        s = jnp.where(ok, s, NEG_INF)
        m = jnp.maximum(jnp.max(s, axis=-1, keepdims=True), s_new)
        e = jnp.where(ok, jnp.exp(s - m), 0.0)
        e_new = jnp.exp(s_new - m)
        den = jnp.maximum(jnp.sum(e, axis=-1, keepdims=True) + e_new, TINY)
        return (_dot(e.astype(BF16), v) + e_new * v_new) / den

    for g in range(G):
        qg = _group_queries(pr_ref, g).astype(BF16)
        ksel = buf_ref[g * SLC_TOPK:(g + 1) * SLC_TOPK, :, g * hd:(g + 1) * hd].reshape(nk, hd).astype(BF16)
        vsel = buf_ref[g * SLC_TOPK:(g + 1) * SLC_TOPK, :, (G + g) * hd:(G + g + 1) * hd].reshape(nk, hd).astype(BF16)
        slot = lax.broadcasted_iota(jnp.int32, (1, nk), 1) // SLC_BLOCK
        cached = jnp.zeros((1, nk), jnp.int32)
        for r in range(SLC_TOPK):
            cached = jnp.where(slot == r, jnp.where(idx_ref[b, g * SLC_TOPK + r] < new_blk, 1, 0), cached)
        ok = jnp.broadcast_to(cached > 0, (hpg, nk))
        o_s = attend_with_new(qg, _dot_nt(qg, ksel), ok, vsel, new_row(2, g), new_row(3, g))
        win = win_ref[0]
        kw = win[:, g * hd:(g + 1) * hd].astype(BF16)
        vw = win[:, (G + g) * hd:(G + g + 1) * hd].astype(BF16)
        wpos = past_len - w_buf + lax.broadcasted_iota(jnp.int32, (hpg, w_buf), 1)
        okw = (wpos <= t_pos) & (wpos > t_pos - WINDOW) & (wpos >= 0)
        o_w = attend_with_new(qg, _dot_nt(qg, kw), okw, vw, new_row(4, g), new_row(5, g))
        for h in range(hpg):
            col = (g * hpg + h) * hd
            gc = g * 3 * hpg + h
            o_h = (gates[:, gc:gc + 1] * oc_ref[0, :, col:col + hd]
                   + gates[:, gc + hpg:gc + hpg + 1] * o_s[h:h + 1]
                   + gates[:, gc + 2 * hpg:gc + 2 * hpg + 1] * o_w[h:h + 1])
            o_ref[0, :, col:col + hd] = o_h


def nsa_sample_attend(proj, o_c, kv_new, win_cache, cache, page_table, idx, t_pos, past_len):
    B = proj.shape[0]
    G, hd = NSA_KV_GROUPS, NSA_HEAD_DIM
    nq = NSA_HEADS * hd
    row = lambda a: pl.BlockSpec((1, 1, a.shape[2]), lambda b, pt, ix: (b, 0, 0))
    grid_spec = pltpu.PrefetchScalarGridSpec(
        num_scalar_prefetch=2,
        grid=(B,),
        in_specs=[row(proj), row(o_c), row(kv_new),
                  pl.BlockSpec((1,) + win_cache.shape[1:], lambda b, pt, ix: (b, 0, 0)),
                  pl.BlockSpec(memory_space=pl.ANY)],
        out_specs=pl.BlockSpec((1, 1, nq), lambda b, pt, ix: (b, 0, 0)),
        scratch_shapes=[pltpu.VMEM((G * SLC_TOPK, SLC_BLOCK, 2 * G * hd), F32), pltpu.SemaphoreType.DMA(())],
    )
    return pl.pallas_call(
        functools.partial(_nsa_sample_attend_kernel, t_pos=t_pos, past_len=past_len),
        grid_spec=grid_spec,
        out_shape=jax.ShapeDtypeStruct((B, 1, nq), F32),
        compiler_params=_cparams("arbitrary"),
        name="nsa_sample_attend",
    )(page_table, idx, proj, o_c, kv_new, win_cache, cache)


def kernel(x_prompt, x_sample, cache_nsa_kv, cache_win_kv, state_hgrn, page_table, norm_mix, norm_mlp, w_mlp_up, w_mlp_down, w_hgrn_in, hgrn_lb_logits, hgrn_onorm, w_hgrn_out, norm_kv, w_kv, cmp_pe_k, cmp_w1_k, cmp_w2_k, cmp_pe_v, cmp_w1_v, cmp_w2_v, w_nsa_q, w_nsa_out, norm_final):
    B, T, D = x_prompt.shape
    Bs, Ts, _ = x_sample.shape
    G, hd = NSA_KV_GROUPS, NSA_HEAD_DIM
    n_pool, page = cache_nsa_kv.shape[:2]
    past_len = page_table.shape[1] * page
    w_buf = cache_win_kv.shape[1]
    assert Ts == 1 and T % 1024 == 0 and T >= WINDOW + 128 and past_len % SLC_BLOCK == 0 and w_buf <= past_len

    wab, pe, w2 = _cmp_weights(cmp_pe_k, cmp_w1_k, cmp_pe_v, cmp_w1_v, cmp_w2_k, cmp_w2_v)
    wbig = _cmp_big_weights(cmp_w1_k, cmp_w1_v)
    wq = [_permute_gate_cols(w_nsa_q[l]) for l in range(DEPTH - N_A)]

    tm = 1024
    x = x_prompt.reshape(B * T, D)
    states_p = []
    for l in range(DEPTH):
        if l == N_A:
            nsa_p, win_p, kvh = kv_proj_prompt(x, norm_kv, w_kv, B, T)
            cmp_p = compress_prompt(kvh, wab, pe, w2)
        if l < N_A:
            proj = rms_proj(x, norm_mix[l], w_hgrn_in[l], tm, 512).reshape(B, T, -1)
            o, s_new = hgrn_prompt(proj, hgrn_lb_logits, hgrn_onorm[l], l)
            states_p.append(s_new)
            x = proj_res(o.reshape(B * T, -1), w_hgrn_out[l], x, tm)
        else:
            q_hm, gates = q_proj_prompt(x, norm_mix[l], wq[l - N_A], B, T)
            o = nsa_prompt(q_hm, gates, cmp_p, kvh)
            x = proj_res(o.reshape(B * T, -1), w_nsa_out[l - N_A], x, tm)
        x = mlp_res(x, norm_mlp[l], w_mlp_up[l], w_mlp_down[l], norm_final, tm, 512, l == DEPTH - 1)
    y_prompt = x.reshape(B, T, D)
    nsa_kv_prompt = nsa_p.reshape(B, T, 4, G, hd)
    win_kv_prompt = win_p.reshape(B, T, 2, G, hd)[:, -min(WINDOW, T):]

    t_pos = past_len
    n_slc = -(-(past_len + 1) // SLC_BLOCK)
    xs = x_sample.reshape(Bs, D)
    cache = cache_nsa_kv.reshape(n_pool, page, 4 * G * hd)
    win_cache = cache_win_kv.reshape(Bs, w_buf, 2 * G * hd)
    states_s = []
    for l in range(DEPTH):
        if l == N_A:
            kv_s = rms_proj(xs, norm_kv, w_kv, Bs, 512)
            cmp_s = compress_sample(cache, page_table, wbig, wab, pe, w2)
        if l < N_A:
            proj = rms_proj(xs, norm_mix[l], w_hgrn_in[l], Bs, 512).reshape(Bs, 1, -1)
            o, s_new = hgrn_step(proj, state_hgrn[l], hgrn_lb_logits, hgrn_onorm[l], l)
            states_s.append(s_new)
        else:
            proj = rms_proj(xs, norm_mix[l], wq[l - N_A], Bs, wq[l - N_A].shape[1]).reshape(Bs, 1, -1)
            o_c, idx = nsa_sample_select(proj, cmp_s, t_pos, n_slc)
            o = nsa_sample_attend(proj, o_c, kv_s.reshape(Bs, 1, -1), win_cache, cache, page_table,
                                  idx.reshape(Bs, G * SLC_TOPK), t_pos, past_len)
        w_o = w_hgrn_out[l] if l < N_A else w_nsa_out[l - N_A]
        xs = proj_res(o.reshape(Bs, -1), w_o, xs, Bs)
        xs = mlp_res(xs, norm_mlp[l], w_mlp_up[l], w_mlp_down[l], norm_final, Bs, 512, l == DEPTH - 1)
    y_sample = xs.reshape(Bs, 1, D)
    n_nsa = 4 * G * hd
    nsa_kv_sample = kv_s[:, :n_nsa].reshape(Bs, 1, 4, G, hd)
    win_new = kv_s[:, n_nsa:].reshape(Bs, 1, 2, G, hd).astype(cache_win_kv.dtype)
    win_kv_sample = jnp.concatenate([cache_win_kv, win_new], axis=1)[:, -w_buf:]

    return (y_prompt, y_sample, nsa_kv_prompt, nsa_kv_sample, win_kv_prompt, win_kv_sample,
            jnp.stack(states_p), jnp.stack(states_s))
```

```python
import functools

import jax
import jax.numpy as jnp
from jax import lax
from jax.experimental import pallas as pl
from jax.experimental.pallas import tpu as pltpu

F32 = jnp.float32
BF16 = jnp.bfloat16

D_MODEL = 1024
DEPTH = 4
N_A = DEPTH // 2
D_FF = 4 * D_MODEL
RMS_EPS = 1e-6
HG_HEADS = 8
HG_K = 128
HG_V = 128
NSA_HEADS = 16
NSA_HEAD_DIM = 64
NSA_KV_GROUPS = 4
NSA_HPG = NSA_HEADS // NSA_KV_GROUPS
NSA_SCALE = NSA_HEAD_DIM ** -0.5
CMP_LEN = 32
CMP_STRIDE = 16
SLC_BLOCK = 64
SLC_TOPK = 16
WINDOW = 512
FORCED_SCORE = 1e4
NEG_INF = -1e30
TINY = 1e-30

HG_CHUNK = 128
VMEM_LIMIT = 56 * 1024 * 1024


def _cparams(*sem):
    return pltpu.CompilerParams(dimension_semantics=sem, vmem_limit_bytes=VMEM_LIMIT)


def _rms(x, g):
    return x * lax.rsqrt(jnp.mean(x * x, axis=-1, keepdims=True) + RMS_EPS) * g


def _sigmoid(x):
    return 1.0 / (1.0 + jnp.exp(-x))


def _dot(a, b):
    return jnp.dot(a, b, preferred_element_type=F32)


def _dot_nt(a, b):
    return lax.dot_general(a, b, (((1,), (1,)), ((), ())), preferred_element_type=F32)


def _dot_tn(a, b):
    return lax.dot_general(a, b, (((0,), (0,)), ((), ())), preferred_element_type=F32)


def _split3(x):
    hi = x.astype(BF16)
    r1 = x - hi.astype(F32)
    mid = r1.astype(BF16)
    lo = (r1 - mid.astype(F32)).astype(BF16)
    return hi, mid, lo


def _masked_softmax(s, mask):
    s = jnp.where(mask, s, NEG_INF)
    e = jnp.where(mask, jnp.exp(s - jnp.max(s, axis=-1, keepdims=True)), 0.0)
    return e / jnp.maximum(jnp.sum(e, axis=-1, keepdims=True), TINY)


def _rms_proj_kernel(x_ref, g_ref, w_ref, o_ref, y_ref):
    @pl.when(pl.program_id(1) == 0)
    def _():
        y_ref[...] = _rms(x_ref[...], g_ref[...]).astype(BF16)

    o_ref[...] = _dot(y_ref[...], w_ref[...].astype(BF16)).astype(o_ref.dtype)


def rms_proj(x, g, w, tm, tn, out_dtype=F32):
    M, D = x.shape
    N = w.shape[1]
    return pl.pallas_call(
        _rms_proj_kernel,
        grid=(M // tm, N // tn),
        in_specs=[pl.BlockSpec((tm, D), lambda i, j: (i, 0)),
                  pl.BlockSpec((1, D), lambda i, j: (0, 0)),
                  pl.BlockSpec((D, tn), lambda i, j: (0, j))],
        out_specs=pl.BlockSpec((tm, tn), lambda i, j: (i, j)),
        out_shape=jax.ShapeDtypeStruct((M, N), out_dtype),
        scratch_shapes=[pltpu.VMEM((tm, D), BF16)],
        compiler_params=_cparams("parallel", "arbitrary"),
        name="rms_proj",
    )(x, g.reshape(1, D), w)


def _proj_res_kernel(a_ref, w_ref, r_ref, o_ref, wb_ref):
    @pl.when(pl.program_id(0) == 0)
    def _():
        wb_ref[...] = w_ref[...].astype(BF16)

    o_ref[...] = r_ref[...] + _dot(a_ref[...].astype(BF16), wb_ref[...])


def proj_res(a, w, res, tm):
    M, K = a.shape
    N = w.shape[1]
    return pl.pallas_call(
        _proj_res_kernel,
        grid=(M // tm,),
        in_specs=[pl.BlockSpec((tm, K), lambda i: (i, 0)),
                  pl.BlockSpec((K, N), lambda i: (0, 0)),
                  pl.BlockSpec((tm, N), lambda i: (i, 0))],
        out_specs=pl.BlockSpec((tm, N), lambda i: (i, 0)),
        out_shape=jax.ShapeDtypeStruct((M, N), F32),
        scratch_shapes=[pltpu.VMEM((K, N), BF16)],
        compiler_params=_cparams("arbitrary"),
        name="proj_res",
    )(a, w, res)


def _mlp_kernel(x_ref, g_ref, wu_ref, wd_ref, gf_ref, o_ref, y_ref, acc_ref, *, final_norm):
    f = pl.program_id(1)

    @pl.when(f == 0)
    def _():
        y_ref[...] = _rms(x_ref[...], g_ref[...]).astype(BF16)
        acc_ref[...] = jnp.zeros_like(acc_ref)

    h = jnp.maximum(_dot(y_ref[...], wu_ref[...].astype(BF16)), 0.0)
    acc_ref[...] += _dot((h * h).astype(BF16), wd_ref[...].astype(BF16))

    @pl.when(f == pl.num_programs(1) - 1)
    def _():
        out = x_ref[...] + acc_ref[...]
        if final_norm:
            out = _rms(out, gf_ref[...])
        o_ref[...] = out


def mlp_res(x, g, w_up, w_down, g_final, tm, tf, final_norm):
    M, D = x.shape
    Fdim = w_up.shape[1]
    return pl.pallas_call(
        functools.partial(_mlp_kernel, final_norm=final_norm),
        grid=(M // tm, Fdim // tf),
        in_specs=[pl.BlockSpec((tm, D), lambda i, f: (i, 0)),
                  pl.BlockSpec((1, D), lambda i, f: (0, 0)),
                  pl.BlockSpec((D, tf), lambda i, f: (0, f)),
                  pl.BlockSpec((tf, D), lambda i, f: (f, 0)),
                  pl.BlockSpec((1, D), lambda i, f: (0, 0))],
        out_specs=pl.BlockSpec((tm, D), lambda i, f: (i, 0)),
        out_shape=jax.ShapeDtypeStruct((M, D), F32),
        scratch_shapes=[pltpu.VMEM((tm, D), BF16), pltpu.VMEM((tm, D), F32)],
        compiler_params=_cparams("parallel", "arbitrary"),
        name="mlp_res",
    )(x, g.reshape(1, D), w_up, w_down, g_final.reshape(1, D))


def _hgrn_lower_bound(lg, layer):
    m = jnp.max(lg, axis=0, keepdims=True)
    e = jnp.exp(lg - m)
    p = e / jnp.sum(e, axis=0, keepdims=True)
    lb = jnp.sum(p[1:layer + 1], axis=0, keepdims=True)
    return jnp.log(lb), jnp.log(1.0 - lb)


def _hgrn_logf(z, lg, layer):
    ls = jnp.minimum(z, 0.0) - jnp.log(1.0 + jnp.exp(-jnp.abs(z)))
    if layer == 0:
        return ls
    log_lb, log1m = _hgrn_lower_bound(lg, layer)
    b2 = log1m + ls
    return jnp.maximum(log_lb, b2) + jnp.log(1.0 + jnp.exp(-jnp.abs(log_lb - b2)))


def _hgrn_sum_matrices(C, n_lev):
    r = lax.broadcasted_iota(jnp.int32, (C, C), 0)
    u = lax.broadcasted_iota(jnp.int32, (C, C), 1)
    mats = [r >= u]
    for lev in range(n_lev):
        h = 1 << lev
        off = r & (2 * h - 1)
        mid = r - off + h
        mats.append(((off >= h) & (u >= mid) & (u <= r)) | ((off < h) & (u > r) & (u < mid)))
    return jnp.concatenate([jnp.where(m, 1.0, 0.0).astype(BF16) for m in mats], axis=0)


def _hgrn_kernel(zq_ref, zf_ref, zi_ref, zg_ref, lg_ref, on_ref, o_ref, s_ref, st_ref, w_ref, *, layer, tc):
    t = pl.program_id(2)
    C = HG_CHUNK
    n_lev = C.bit_length() - 1

    @pl.when(t == 0)
    def _():
        st_ref[...] = jnp.zeros_like(st_ref)
        w_ref[...] = _hgrn_sum_matrices(C, n_lev)

    r_i = lax.broadcasted_iota(jnp.int32, (C, C), 0)
    c_i = lax.broadcasted_iota(jnp.int32, (C, C), 1)
    lg = lg_ref[...]
    onorm = on_ref[...]

    def chunk(ci, st):
        rows = pl.ds(pl.multiple_of(ci * C, C), C)
        q = zq_ref[0, rows, :]
        v = zi_ref[0, rows, :]
        vb = v.astype(BF16)
        logf = _hgrn_logf(zf_ref[0, rows, :], lg, layer)
        k = 1.0 - jnp.exp(logf)
        hi, mid, lo = _split3(logf)
        b = _dot(w_ref[0:C, :], hi) + _dot(w_ref[0:C, :], mid) + _dot(w_ref[0:C, :], lo)
        e_lev = jnp.exp(_dot(w_ref[C:, :], hi) + _dot(w_ref[C:, :], mid))
        a = jnp.where(r_i == c_i, _dot_nt(q.astype(BF16), k.astype(BF16)), 0.0)
        for lev in range(n_lev):
            h = 1 << lev
            e = e_lev[lev * C:(lev + 1) * C]
            upper = (lax.broadcasted_iota(jnp.int32, (C, HG_K), 0) & (2 * h - 1)) >= h
            qt = jnp.where(upper, q * e, 0.0).astype(BF16)
            kt = jnp.where(upper, 0.0, k * e).astype(BF16)
            al = _dot_nt(qt, kt)
            if 2 * h < C:
                al = jnp.where((r_i >> (lev + 1)) == (c_i >> (lev + 1)), al, 0.0)
            a = a + al
        o = _dot(a.astype(BF16), vb) + _dot_nt((q * jnp.exp(b)).astype(BF16), st.astype(BF16))
        b_end = b[C - 1:C]
        kd = (k * jnp.exp(b_end - b)).astype(BF16)
        st = st * jnp.exp(b_end) + _dot_tn(vb, kd)
        o = o * lax.rsqrt(jnp.mean(o * o, axis=-1, keepdims=True) + RMS_EPS) * onorm
        zg = zg_ref[0, rows, :]
        o_ref[0, rows, :] = (o * (zg * _sigmoid(zg))).astype(o_ref.dtype)
        return st

    st_ref[...] = lax.fori_loop(0, tc // C, chunk, st_ref[...], unroll=2)

    @pl.when(t == pl.num_programs(2) - 1)
    def _():
        s_ref[0, 0] = st_ref[...].T


def hgrn_prompt(proj, lb_logits, onorm, layer, tc=512):
    B, T, _ = proj.shape
    H = HG_HEADS
    n_mats = HG_CHUNK.bit_length()
    return pl.pallas_call(
        functools.partial(_hgrn_kernel, layer=layer, tc=tc),
        grid=(B, H, T // tc),
        in_specs=[pl.BlockSpec((1, tc, HG_K), lambda b, h, t: (b, t, h)),
                  pl.BlockSpec((1, tc, HG_K), lambda b, h, t: (b, t, H + h)),
                  pl.BlockSpec((1, tc, HG_V), lambda b, h, t: (b, t, 2 * H + h)),
                  pl.BlockSpec((1, tc, HG_V), lambda b, h, t: (b, t, 3 * H + h)),
                  pl.BlockSpec((N_A, HG_K), lambda b, h, t: (0, h)),
                  pl.BlockSpec((1, HG_V), lambda b, h, t: (0, h))],
        out_specs=[pl.BlockSpec((1, tc, HG_V), lambda b, h, t: (b, t, h)),
                   pl.BlockSpec((1, 1, HG_K, HG_V), lambda b, h, t: (b, h, 0, 0))],
        out_shape=[jax.ShapeDtypeStruct((B, T, H * HG_V), BF16),
                   jax.ShapeDtypeStruct((B, H, HG_K, HG_V), F32)],
        scratch_shapes=[pltpu.VMEM((HG_V, HG_K), F32),
                        pltpu.VMEM((n_mats * HG_CHUNK, HG_CHUNK), BF16)],
        compiler_params=_cparams("parallel", "parallel", "arbitrary"),
        name="hgrn_prompt",
    )(proj, proj, proj, proj, lb_logits, onorm.reshape(1, H * HG_V))


def _kv_proj_kernel(x_ref, g_ref, w_ref, nsa_ref, win_ref, k_ref, vt_ref, wb_ref):
    @pl.when(pl.program_id(0) == 0)
    def _():
        wb_ref[...] = w_ref[...].astype(BF16)

    G, hd = NSA_KV_GROUPS, NSA_HEAD_DIM
    y = _rms(x_ref[...], g_ref[...]).astype(BF16)
    kv = _dot(y, wb_ref[...])
    n_nsa = nsa_ref.shape[1]
    nsa_ref[...] = kv[:, :n_nsa]
    win_ref[...] = kv[:, n_nsa:]
    for n, kind in enumerate((2, 4)):
        for gi in range(G):
            col = (kind * G + gi) * hd
            k_ref[0, n * G + gi] = kv[:, col:col + hd].astype(BF16)
    for n, kind in enumerate((3, 5)):
        for gp in range(G // 2):
            col = (kind * G + 2 * gp) * hd
            t = kv[:, col:col + 2 * hd].T
            vt_ref[0, n * G + 2 * gp] = t[:hd].astype(BF16)
            vt_ref[0, n * G + 2 * gp + 1] = t[hd:].astype(BF16)


def kv_proj_prompt(x, g, w_kv, B, T, tm=512):
    M, D = x.shape
    N = w_kv.shape[1]
    G, hd = NSA_KV_GROUPS, NSA_HEAD_DIM
    n_nsa = 4 * G * hd
    tpb = T // tm
    return pl.pallas_call(
        _kv_proj_kernel,
        grid=(M // tm,),
        in_specs=[pl.BlockSpec((tm, D), lambda i: (i, 0)),
                  pl.BlockSpec((1, D), lambda i: (0, 0)),
                  pl.BlockSpec((D, N), lambda i: (0, 0))],
        out_specs=[pl.BlockSpec((tm, n_nsa), lambda i: (i, 0)),
                   pl.BlockSpec((tm, N - n_nsa), lambda i: (i, 0)),
                   pl.BlockSpec((1, 2 * G, tm, hd), lambda i: (i // tpb, 0, i % tpb, 0)),
                   pl.BlockSpec((1, 2 * G, hd, tm), lambda i: (i // tpb, 0, 0, i % tpb))],
        out_shape=[jax.ShapeDtypeStruct((M, n_nsa), F32),
                   jax.ShapeDtypeStruct((M, N - n_nsa), F32),
                   jax.ShapeDtypeStruct((B, 2 * G, T, hd), BF16),
                   jax.ShapeDtypeStruct((B, 2 * G, hd, T), BF16)],
        scratch_shapes=[pltpu.VMEM((D, N), BF16)],
        compiler_params=_cparams("arbitrary"),
        name="kv_proj",
    )(x, g.reshape(1, D), w_kv)


GATE_ROWS = 16
GATE_LANES = 128


def _q_proj_kernel(x_ref, g_ref, w_ref, q_ref, gt_ref, wb_ref):
    @pl.when(pl.program_id(0) == 0)
    def _():
        wb_ref[...] = w_ref[...].astype(BF16)

    y = _rms(x_ref[...], g_ref[...]).astype(BF16)
    pr = _dot(y, wb_ref[...])
    nq = NSA_HEADS * NSA_HEAD_DIM
    for h in range(NSA_HEADS):
        q_ref[0, h] = (pr[:, h * NSA_HEAD_DIM:(h + 1) * NSA_HEAD_DIM] * NSA_SCALE).astype(BF16)
    gates_t = _sigmoid(pr[:, nq:]).T
    for gi in range(NSA_KV_GROUPS):
        gt_ref[0, gi] = gates_t[gi * GATE_ROWS:(gi + 1) * GATE_ROWS]


def _permute_gate_cols(w_q):
    nq = NSA_HEADS * NSA_HEAD_DIM
    d = w_q.shape[0]
    wg = w_q[:, nq:].reshape(d, 3, NSA_KV_GROUPS, NSA_HPG).transpose(0, 2, 1, 3).reshape(d, NSA_KV_GROUPS, 3 * NSA_HPG)
    wg = jnp.pad(wg, ((0, 0), (0, 0), (0, GATE_ROWS - 3 * NSA_HPG))).reshape(d, NSA_KV_GROUPS * GATE_ROWS)
    wg = jnp.pad(wg, ((0, 0), (0, GATE_LANES - NSA_KV_GROUPS * GATE_ROWS)))
    return jnp.concatenate([w_q[:, :nq], wg], axis=1)


def q_proj_prompt(x, g, w_qp, B, T, tm=512):
    M, D = x.shape
    N = w_qp.shape[1]
    tpb = T // tm
    return pl.pallas_call(
        _q_proj_kernel,
        grid=(M // tm,),
        in_specs=[pl.BlockSpec((tm, D), lambda i: (i, 0)),
                  pl.BlockSpec((1, D), lambda i: (0, 0)),
                  pl.BlockSpec((D, N), lambda i: (0, 0))],
        out_specs=[pl.BlockSpec((1, NSA_HEADS, tm, NSA_HEAD_DIM), lambda i: (i // tpb, 0, i % tpb, 0)),
                   pl.BlockSpec((1, NSA_KV_GROUPS, GATE_ROWS, tm), lambda i: (i // tpb, 0, 0, i % tpb))],
        out_shape=[jax.ShapeDtypeStruct((B, NSA_HEADS, T, NSA_HEAD_DIM), BF16),
                   jax.ShapeDtypeStruct((B, NSA_KV_GROUPS, GATE_ROWS, T), F32)],
        scratch_shapes=[pltpu.VMEM((D, N), BF16)],
        compiler_params=_cparams("arbitrary"),
        name="q_proj",
    )(x, g.reshape(1, D), w_qp)


def _cmp_weights(pe_k, w1_k, pe_v, w1_v, w2_k, w2_v):
    half = CMP_STRIDE * NSA_HEAD_DIM

    def ab(w1):
        return jnp.concatenate([w1[:half], w1[half:]], axis=1)

    def big(w1):
        w = w1.reshape(2, CMP_STRIDE, NSA_HEAD_DIM, -1)
        b = jnp.einsum("alds,gh->lgdhas", w, jnp.eye(2, dtype=w1.dtype))
        return b.reshape(CMP_STRIDE * 2 * NSA_HEAD_DIM, 2 * 2 * w.shape[-1])

    wab = jnp.stack([ab(w1_k), ab(w1_v)])
    wbig = jnp.stack([big(w1_k), big(w1_v)])
    pe = jnp.stack([pe_k.reshape(2, half), pe_v.reshape(2, half)])
    w2 = jnp.stack([w2_k, w2_v])
    w2t = jnp.stack([w2_k.T, w2_v.T])
    return wab, wbig, pe, w2, w2t


def _cmp_taps(x_ref, ns):
    return jnp.concatenate([x_ref[pl.ds(l, ns, stride=CMP_STRIDE), :].astype(BF16) for l in range(CMP_STRIDE)], axis=1)


def _cmp_hidden(ab, pe, wab):
    hd = NSA_HEAD_DIM
    n = ab.shape[0]
    nxt = pltpu.roll(ab, n - 1, axis=0)
    pt = _dot(pe.astype(BF16), wab)
    hid = ab[:, :hd] + nxt[:, hd:] + pt[0:1, :hd] + pt[1:2, hd:]
    return hid * _sigmoid(hid)


def _cmp_prompt_kernel(xk_ref, xv_ref, wbig_ref, wab_ref, pe_ref, w2_ref, w2t_ref, kc_ref, vct_ref):
    hd = NSA_HEAD_DIM
    ns = xk_ref.shape[0] // CMP_STRIDE
    for c, x_ref in enumerate((xk_ref, xv_ref)):
        wab = wab_ref[c].astype(BF16)
        ab2 = _dot(_cmp_taps(x_ref, ns), wbig_ref[c].astype(BF16))
        for gl in range(2):
            act = _cmp_hidden(ab2[:, gl * 2 * hd:(gl + 1) * 2 * hd], pe_ref[c], wab).astype(BF16)
            if c == 0:
                kc_ref[0, gl] = _dot(act, w2_ref[c].astype(BF16)).astype(kc_ref.dtype)
            else:
                vct_ref[0, gl] = _dot_nt(w2t_ref[c].astype(BF16), act).astype(vct_ref.dtype)


def compress_prompt(nsa, B, T, wbig, wab, pe, w2, w2t):
    G, hd = NSA_KV_GROUPS, NSA_HEAD_DIM
    ns = T // CMP_STRIDE
    const = lambda a: pl.BlockSpec(a.shape, lambda b, gp: (0,) * a.ndim)
    return pl.pallas_call(
        _cmp_prompt_kernel,
        grid=(B, G // 2),
        in_specs=[pl.BlockSpec((T, 2 * hd), lambda b, gp: (b, gp)),
                  pl.BlockSpec((T, 2 * hd), lambda b, gp: (b, G // 2 + gp)),
                  const(wbig), const(wab), const(pe), const(w2), const(w2t)],
        out_specs=[pl.BlockSpec((1, 2, ns, hd), lambda b, gp: (b, gp, 0, 0)),
                   pl.BlockSpec((1, 2, hd, ns), lambda b, gp: (b, gp, 0, 0))],
        out_shape=[jax.ShapeDtypeStruct((B, G, ns, hd), BF16),
                   jax.ShapeDtypeStruct((B, G, hd, ns), BF16)],
        compiler_params=_cparams("parallel", "parallel"),
        name="compress_prompt",
    )(nsa, nsa, wbig, wab, pe, w2, w2t)


def _nsa_prompt_kernel(q_ref, gt_ref, kc_ref, vct_ref, ks_ref, vst_ref, kw_ref, vwt_ref, o_ref, lim_ref, *, tq, ck):
    i = pl.program_id(2)
    s0 = i * tq
    hpg, hd = NSA_HPG, NSA_HEAD_DIM
    R = hpg * tq
    T = ks_ref.shape[2]
    n_cmp = kc_ref.shape[2]
    n_slc = T // SLC_BLOCK
    Q = q_ref[0].reshape(R, hd)
    tpos = s0 + lax.broadcasted_iota(jnp.int32, (1, R), 1) % tq

    sc = _dot_nt(kc_ref[0, 0], Q)
    ok_c = lax.broadcasted_iota(jnp.int32, (n_cmp, R), 0) * CMP_STRIDE + (CMP_LEN - 1) <= tpos
    sc = jnp.where(ok_c, sc, NEG_INF)
    e_c = jnp.where(ok_c, jnp.exp(sc - jnp.max(sc, axis=0, keepdims=True)), 0.0)
    p_c = e_c / jnp.maximum(jnp.sum(e_c, axis=0, keepdims=True), TINY)
    o_c = _dot(vct_ref[0, 0], p_c.astype(BF16))

    psum = p_c[:, 0:tq]
    for h in range(1, hpg):
        psum = psum + p_c[:, h * tq:(h + 1) * tq]
    sj = lax.broadcasted_iota(jnp.int32, (n_slc, n_cmp), 0) * SLC_BLOCK
    ci = lax.broadcasted_iota(jnp.int32, (n_slc, n_cmp), 1) * CMP_STRIDE
    ov = jnp.where((ci < sj + SLC_BLOCK) & (ci + CMP_LEN > sj), 1.0, 0.0).astype(BF16)
    hi, mid, lo = _split3(psum)
    imp = _dot(ov, hi) + _dot(ov, mid) + _dot(ov, lo)
    blk = lax.broadcasted_iota(jnp.int32, (n_slc, tq), 0)
    qpos = s0 + lax.broadcasted_iota(jnp.int32, (n_slc, tq), 1)
    cur = qpos // SLC_BLOCK
    forced = (blk == 0) | (blk == cur) | (blk == cur - 1)
    score = jnp.where(forced, FORCED_SCORE, jnp.where(blk <= cur, imp, -1.0))
    rank = jnp.zeros((n_slc, tq), F32)
    for j in range(n_slc):
        cj = score[j:j + 1, :]
        rank = rank + jnp.where((cj > score) | ((cj == score) & (blk > j)), 1.0, 0.0)
    lim = jnp.where(rank < SLC_TOPK, qpos, -1)
    lim_ref[...] = jnp.concatenate([lim] * hpg, axis=1)

    bpc = ck // SLC_BLOCK

    def body(c, carry):
        m, l, acc = carry
        k0 = pl.multiple_of(c * ck, ck)
        s = _dot_nt(ks_ref[0, 0, pl.ds(k0, ck), :], Q)
        limk = jnp.concatenate(
            [jnp.broadcast_to(lim_ref[pl.ds(c * bpc + jj, 1), :], (SLC_BLOCK, R)) for jj in range(bpc)], axis=0)
        kpos = k0 + lax.broadcasted_iota(jnp.int32, (ck, R), 0)
        s = jnp.where(kpos <= limk, s, NEG_INF)
        m_new = jnp.maximum(m, jnp.max(s, axis=0, keepdims=True))
        p = jnp.exp(s - m_new)
        alpha = jnp.exp(m - m_new)
        l = alpha * l + jnp.sum(p, axis=0, keepdims=True)
        acc = alpha * acc + _dot(vst_ref[0, 0, :, pl.ds(k0, ck)], p.astype(BF16))
        return m_new, l, acc

    n_chunks = (s0 + tq + ck - 1) // ck
    m0 = jnp.full((1, R), NEG_INF, F32)
    _, l, acc = lax.fori_loop(0, n_chunks, body, (m0, jnp.zeros((1, R), F32), jnp.zeros((hd, R), F32)))
    o_s = acc / jnp.maximum(l, TINY)

    wk = WINDOW + tq
    ws = pl.multiple_of(jnp.maximum(s0 - WINDOW, 0), tq)
    sw = _dot_nt(kw_ref[0, 0, pl.ds(ws, wk), :], Q)
    wpos = ws + lax.broadcasted_iota(jnp.int32, (wk, R), 0)
    sw = jnp.where((wpos <= tpos) & (wpos > tpos - WINDOW), sw, NEG_INF)
    p_w = jnp.exp(sw - jnp.max(sw, axis=0, keepdims=True))
    l_w = jnp.maximum(jnp.sum(p_w, axis=0, keepdims=True), TINY)
    o_w = _dot(vwt_ref[0, 0, :, pl.ds(ws, wk)], p_w.astype(BF16)) / l_w

    gt = gt_ref[0, 0]
    outs = []
    for h in range(hpg):
        cols = slice(h * tq, (h + 1) * tq)
        outs.append(gt[h:h + 1] * o_c[:, cols] + gt[hpg + h:hpg + h + 1] * o_s[:, cols]
                    + gt[2 * hpg + h:2 * hpg + h + 1] * o_w[:, cols])
    for pair in range(hpg // 2):
        both = jnp.concatenate(outs[2 * pair:2 * pair + 2], axis=0)
        o_ref[0, :, pair * 2 * hd:(pair + 1) * 2 * hd] = both.T.astype(o_ref.dtype)


def nsa_prompt(q_hm, gates_t, kc, vct, k_hm, vt_hm, tq=256, ck=256):
    B, _, T, hd = q_hm.shape
    G, hpg = NSA_KV_GROUPS, NSA_HPG
    n_cmp = kc.shape[2]
    keys = lambda off: pl.BlockSpec((1, 1, T, hd), lambda b, g, i: (b, off + g, 0, 0))
    vals = lambda off: pl.BlockSpec((1, 1, hd, T), lambda b, g, i: (b, off + g, 0, 0))
    return pl.pallas_call(
        functools.partial(_nsa_prompt_kernel, tq=tq, ck=ck),
        grid=(B, G, T // tq),
        in_specs=[pl.BlockSpec((1, hpg, tq, hd), lambda b, g, i: (b, g, i, 0)),
                  pl.BlockSpec((1, 1, GATE_ROWS, tq), lambda b, g, i: (b, g, 0, i)),
                  pl.BlockSpec((1, 1, n_cmp, hd), lambda b, g, i: (b, g, 0, 0)),
                  pl.BlockSpec((1, 1, hd, n_cmp), lambda b, g, i: (b, g, 0, 0)),
                  keys(0), vals(0), keys(G), vals(G)],
        out_specs=pl.BlockSpec((1, tq, hpg * hd), lambda b, g, i: (b, i, g)),
        out_shape=jax.ShapeDtypeStruct((B, T, NSA_HEADS * hd), BF16),
        scratch_shapes=[pltpu.VMEM((T // SLC_BLOCK, hpg * tq), jnp.int32)],
        compiler_params=_cparams("parallel", "parallel", "arbitrary"),
        name="nsa_prompt",
    )(q_hm, gates_t, kc, vct, k_hm, vt_hm, k_hm, vt_hm)


def _row_to_col(row):
    n = row.shape[1]
    eye = lax.broadcasted_iota(jnp.int32, (n, n), 0) == lax.broadcasted_iota(jnp.int32, (n, n), 1)
    return jnp.sum(jnp.where(eye, jnp.broadcast_to(row, (n, n)), 0.0), axis=-1, keepdims=True)


def _hgrn_step_kernel(z_ref, s0_ref, lg_ref, on_ref, o_ref, s_ref, *, layer):
    hk = HG_HEADS * HG_K
    hv = HG_HEADS * HG_V
    for h in range(HG_HEADS):
        kl = slice(h * HG_K, (h + 1) * HG_K)
        vl = slice(h * HG_V, (h + 1) * HG_V)
        q = z_ref[0, :, kl]
        logf = _hgrn_logf(z_ref[0, :, hk + h * HG_K:hk + (h + 1) * HG_K], lg_ref[:, kl], layer)
        f = jnp.exp(logf)
        v = z_ref[0, :, 2 * hk + h * HG_V:2 * hk + (h + 1) * HG_V]
        zg = z_ref[0, :, 2 * hk + hv + h * HG_V:2 * hk + hv + (h + 1) * HG_V]
        s = _row_to_col(f) * s0_ref[0, h] + _row_to_col(1.0 - f) * v
        s_ref[0, h] = s
        o = jnp.sum(_row_to_col(q) * s, axis=0, keepdims=True)
        o = o * lax.rsqrt(jnp.mean(o * o, axis=-1, keepdims=True) + RMS_EPS) * on_ref[:, vl]
        o_ref[0, :, vl] = o * (zg * _sigmoid(zg))


def hgrn_step(proj, s0, lb_logits, onorm, layer):
    B = proj.shape[0]
    H = HG_HEADS
    return pl.pallas_call(
        functools.partial(_hgrn_step_kernel, layer=layer),
        grid=(B,),
        in_specs=[pl.BlockSpec((1, 1, proj.shape[2]), lambda b: (b, 0, 0)),
                  pl.BlockSpec((1, H, HG_K, HG_V), lambda b: (b, 0, 0, 0)),
                  pl.BlockSpec((N_A, H * HG_K), lambda b: (0, 0)),
                  pl.BlockSpec((1, H * HG_V), lambda b: (0, 0))],
        out_specs=[pl.BlockSpec((1, 1, H * HG_V), lambda b: (b, 0, 0)),
                   pl.BlockSpec((1, H, HG_K, HG_V), lambda b: (b, 0, 0, 0))],
        out_shape=[jax.ShapeDtypeStruct((B, 1, H * HG_V), F32),
                   jax.ShapeDtypeStruct((B, H, HG_K, HG_V), F32)],
        compiler_params=_cparams("parallel"),
        name="hgrn_step",
    )(proj, s0, lb_logits, onorm.reshape(1, H * HG_V))


def _cmp_sample_kernel(pt_ref, cache_ref, wbig_ref, wab_ref, pe_ref, w2_ref, o_ref, raw_ref, buf_ref, ab_ref, sem,
                       *, n_pages):
    b = pl.program_id(0)
    nb = pl.num_programs(0)
    G, hd = NSA_KV_GROUPS, NSA_HEAD_DIM
    hp = n_pages // 2
    spp = cache_ref.shape[3] // CMP_STRIDE
    nsh = hp * spp

    def page_copy(bb, half, p):
        src = cache_ref.at[pt_ref[bb, half * hp + p], pl.ds(0, 2)]
        return pltpu.make_async_copy(src, raw_ref.at[half, p], sem.at[half])

    def start_half(bb, half):
        def body(p, c):
            page_copy(bb, half, p).start()
            return c
        lax.fori_loop(0, hp, body, 0)

    def wait_half(bb, half):
        def body(p, c):
            page_copy(bb, half, p).wait()
            return c
        lax.fori_loop(0, hp, body, 0)

    def compute_half(half):
        def to_token_rows(p, c):
            for kind in range(2):
                for gp in range(G // 2):
                    buf_ref[kind * (G // 2) + gp, p] = raw_ref[half, p, kind, gp * 2 * hd:(gp + 1) * 2 * hd, :].T
            return c
        lax.fori_loop(0, hp, to_token_rows, 0)
        for cb in range(G):
            x = jnp.concatenate(
                [buf_ref[cb, :, pl.ds(l, spp, stride=CMP_STRIDE), :].reshape(nsh, 2 * hd).astype(BF16)
                 for l in range(CMP_STRIDE)], axis=1)
            ab_ref[cb, half * nsh:(half + 1) * nsh, :] = _dot(x, wbig_ref[cb // (G // 2)].astype(BF16))

    @pl.when(b == 0)
    def _():
        start_half(0, 0)

    start_half(b, 1)
    wait_half(b, 0)
    compute_half(0)

    @pl.when(b + 1 < nb)
    def _():
        start_half(b + 1, 0)

    wait_half(b, 1)
    compute_half(1)

    for c in range(2):
        wab = wab_ref[c].astype(BF16)
        w2 = w2_ref[c].astype(BF16)
        for gp in range(G // 2):
            ab2 = ab_ref[c * (G // 2) + gp]
            for gl in range(2):
                act = _cmp_hidden(ab2[:, gl * 2 * hd:(gl + 1) * 2 * hd], pe_ref[c], wab)
                col = (c * G + 2 * gp + gl) * hd
                o_ref[0, :, col:col + hd] = _dot(act.astype(BF16), w2).astype(o_ref.dtype)


def compress_sample(cache_t, page_table, wbig, wab, pe, w2):
    B, n_pages = page_table.shape
    page = cache_t.shape[3]
    G, hd = NSA_KV_GROUPS, NSA_HEAD_DIM
    ns = n_pages * page // CMP_STRIDE
    assert n_pages % 2 == 0 and page == 2 * hd
    const = lambda a: pl.BlockSpec(a.shape, lambda b, pt: (0,) * a.ndim)
    grid_spec = pltpu.PrefetchScalarGridSpec(
        num_scalar_prefetch=1,
        grid=(B,),
        in_specs=[pl.BlockSpec(memory_space=pl.ANY), const(wbig), const(wab), const(pe), const(w2)],
        out_specs=pl.BlockSpec((1, ns, 2 * G * hd), lambda b, pt: (b, 0, 0)),
        scratch_shapes=[pltpu.VMEM((2, n_pages // 2, 2, G * hd, page), F32),
                        pltpu.VMEM((G, n_pages // 2, page, 2 * hd), F32),
                        pltpu.VMEM((G, ns, 4 * hd), F32),
                        pltpu.SemaphoreType.DMA((2,))],
    )
    return pl.pallas_call(
        functools.partial(_cmp_sample_kernel, n_pages=n_pages),
        grid_spec=grid_spec,
        out_shape=jax.ShapeDtypeStruct((B, ns, 2 * G * hd), BF16),
        compiler_params=_cparams("arbitrary"),
        name="compress_sample",
    )(page_table, cache_t, wbig, wab, pe, w2)


def _group_queries(pr_ref, g):
    hd = NSA_HEAD_DIM
    rows = [pr_ref[0, :, (g * NSA_HPG + h) * hd:(g * NSA_HPG + h + 1) * hd] for h in range(NSA_HPG)]
    return jnp.concatenate(rows, axis=0) * NSA_SCALE


def _nsa_sample_select_kernel(pr_ref, cmp_ref, oc_ref, idx_ref, *, t_pos, n_slc, n_pad):
    G, hpg, hd = NSA_KV_GROUPS, NSA_HPG, NSA_HEAD_DIM
    n_cmp = cmp_ref.shape[1]
    cmp = cmp_ref[0]
    ci = lax.broadcasted_iota(jnp.int32, (n_cmp, n_pad), 0) * CMP_STRIDE
    sj = lax.broadcasted_iota(jnp.int32, (n_cmp, n_pad), 1) * SLC_BLOCK
    ov = jnp.where((ci < sj + SLC_BLOCK) & (ci + CMP_LEN > sj), 1.0, 0.0).astype(BF16)
    blk = lax.broadcasted_iota(jnp.int32, (1, n_pad), 1)
    cur = t_pos // SLC_BLOCK
    forced = (blk == 0) | (blk == cur) | (blk == cur - 1)
    jr = lax.broadcasted_iota(jnp.int32, (n_pad, n_pad), 0)
    jc = lax.broadcasted_iota(jnp.int32, (n_pad, n_pad), 1)
    for g in range(G):
        qg = _group_queries(pr_ref, g).astype(BF16)
        sc = _dot_nt(qg, cmp[:, g * hd:(g + 1) * hd])
        e_pos = lax.broadcasted_iota(jnp.int32, (hpg, n_cmp), 1) * CMP_STRIDE + (CMP_LEN - 1)
        p_c = _masked_softmax(sc, e_pos <= t_pos)
        o_c = _dot(p_c.astype(BF16), cmp[:, (G + g) * hd:(G + g + 1) * hd])
        for h in range(hpg):
            col = (g * hpg + h) * hd
            oc_ref[0, :, col:col + hd] = o_c[h:h + 1]
        hi, mid, lo = _split3(jnp.sum(p_c, axis=0, keepdims=True))
        imp = _dot(hi, ov) + _dot(mid, ov) + _dot(lo, ov)
        score = jnp.where(forced, FORCED_SCORE, jnp.where(blk <= cur, imp, -1.0))
        score = jnp.where(blk < n_slc, score, -2.0)
        col_s = _row_to_col(score)
        beats = (col_s > score) | ((col_s == score) & (jr < jc))
        rank = jnp.sum(jnp.where(beats, 1.0, 0.0), axis=0, keepdims=True)
        rr = lax.broadcasted_iota(jnp.int32, (SLC_TOPK, n_pad), 0).astype(F32)
        bsel = jnp.where(jnp.broadcast_to(rank, (SLC_TOPK, n_pad)) == rr,
                         lax.broadcasted_iota(jnp.int32, (SLC_TOPK, n_pad), 1).astype(F32), 0.0)
        idx_ref[0, g * SLC_TOPK:(g + 1) * SLC_TOPK, :] = jnp.sum(bsel, axis=-1, keepdims=True).astype(jnp.int32)


def nsa_sample_select(proj, cmp_s, t_pos, n_slc):
    B = proj.shape[0]
    n_pad = -(-n_slc // 128) * 128
    G = NSA_KV_GROUPS
    nq = NSA_HEADS * NSA_HEAD_DIM
    return pl.pallas_call(
        functools.partial(_nsa_sample_select_kernel, t_pos=t_pos, n_slc=n_slc, n_pad=n_pad),
        grid=(B,),
        in_specs=[pl.BlockSpec((1, 1, proj.shape[2]), lambda b: (b, 0, 0)),
                  pl.BlockSpec((1,) + cmp_s.shape[1:], lambda b: (b, 0, 0))],
        out_specs=[pl.BlockSpec((1, 1, nq), lambda b: (b, 0, 0)),
                   pl.BlockSpec((1, G * SLC_TOPK, 1), lambda b: (b, 0, 0))],
        out_shape=[jax.ShapeDtypeStruct((B, 1, nq), F32),
                   jax.ShapeDtypeStruct((B, G * SLC_TOPK, 1), jnp.int32)],
        compiler_params=_cparams("parallel"),
        name="nsa_sample_select",
    )(proj, cmp_s)


def _nsa_sample_attend_kernel(pt_ref, idx_ref, pr_ref, oc_ref, kvn_ref, win_ref, cache_ref, o_ref,
                              kbuf_ref, vbuf_ref, sem, *, t_pos, past_len):
    b = pl.program_id(0)
    nb = pl.num_programs(0)
    G, hpg, hd = NSA_KV_GROUPS, NSA_HPG, NSA_HEAD_DIM
    n_sel = G * SLC_TOPK
    page = cache_ref.shape[3]
    bpp = page // SLC_BLOCK
    new_blk = past_len // SLC_BLOCK

    def blk_copies(bb, n):
        slot = bb % 2
        j = jnp.minimum(idx_ref[bb, n], new_blk - 1)
        pg = pt_ref[bb, j // bpp]
        rows = pl.ds(pl.multiple_of((n // SLC_TOPK) * hd, hd), hd)
        return (pltpu.make_async_copy(cache_ref.at[pg, 2, rows], kbuf_ref.at[slot, n], sem.at[slot]),
                pltpu.make_async_copy(cache_ref.at[pg, 3, rows], vbuf_ref.at[slot, n], sem.at[slot]))

    def start_all(bb):
        def body(n, c):
            for cp in blk_copies(bb, n):
                cp.start()
            return c
        lax.fori_loop(0, n_sel, body, 0)

    def wait_all(bb):
        def body(n, c):
            for cp in blk_copies(bb, n):
                cp.wait()
            return c
        lax.fori_loop(0, n_sel, body, 0)

    @pl.when(b == 0)
    def _():
        start_all(0)

    @pl.when(b + 1 < nb)
    def _():
        start_all(b + 1)

    wait_all(b)
    slot = b % 2

    nk = SLC_TOPK * page
    w_buf = win_ref.shape[3]
    nq = NSA_HEADS * hd
    gates = _sigmoid(pr_ref[0, :, nq:nq + GATE_LANES])
    kvn = kvn_ref[0]

    def new_row(kind, g):
        return kvn[:, (kind * G + g) * hd:(kind * G + g + 1) * hd].astype(BF16).astype(F32)

    def attend_with_new(qg, s, ok, vt, k_new, v_new):
        s_new = jnp.sum(qg.astype(F32) * k_new, axis=-1, keepdims=True)
        s = jnp.where(ok, s, NEG_INF)
        m = jnp.maximum(jnp.max(s, axis=-1, keepdims=True), s_new)
        e = jnp.where(ok, jnp.exp(s - m), 0.0)
        e_new = jnp.exp(s_new - m)
        den = jnp.maximum(jnp.sum(e, axis=-1, keepdims=True) + e_new, TINY)
        return (_dot_nt(e.astype(BF16), vt) + e_new * v_new) / den

    lane = lax.broadcasted_iota(jnp.int32, (1, nk), 1)
    for g in range(G):
        qg = _group_queries(pr_ref, g).astype(BF16)
        kt = jnp.concatenate([kbuf_ref[slot, g * SLC_TOPK + r] for r in range(SLC_TOPK)], axis=1).astype(BF16)
        vt = jnp.concatenate([vbuf_ref[slot, g * SLC_TOPK + r] for r in range(SLC_TOPK)], axis=1).astype(BF16)
        vis = jnp.zeros((1, nk), jnp.int32)
        for r in range(SLC_TOPK):
            j = idx_ref[b, g * SLC_TOPK + r]
            half = jnp.where(j < new_blk, j % bpp, -1)
            vis = jnp.where(lane // page == r, jnp.where((lane % page) // SLC_BLOCK == half, 1, 0), vis)
        ok = jnp.broadcast_to(vis > 0, (hpg, nk))
        o_s = attend_with_new(qg, _dot(qg, kt), ok, vt, new_row(2, g), new_row(3, g))
        kwt = win_ref[0, 0, g * hd:(g + 1) * hd, :].astype(BF16)
        vwt = win_ref[0, 1, g * hd:(g + 1) * hd, :].astype(BF16)
        wpos = past_len - w_buf + lax.broadcasted_iota(jnp.int32, (hpg, w_buf), 1)
        okw = (wpos <= t_pos) & (wpos > t_pos - WINDOW) & (wpos >= 0)
        o_w = attend_with_new(qg, _dot(qg, kwt), okw, vwt, new_row(4, g), new_row(5, g))
        for h in range(hpg):
            col = (g * hpg + h) * hd
            gc = g * GATE_ROWS + h
            o_h = (gates[:, gc:gc + 1] * oc_ref[0, :, col:col + hd]
                   + gates[:, gc + hpg:gc + hpg + 1] * o_s[h:h + 1]
                   + gates[:, gc + 2 * hpg:gc + 2 * hpg + 1] * o_w[h:h + 1])
            o_ref[0, :, col:col + hd] = o_h


def nsa_sample_attend(proj, o_c, kv_new, win_t, cache_t, page_table, idx, t_pos, past_len):
    B = proj.shape[0]
    G, hd = NSA_KV_GROUPS, NSA_HEAD_DIM
    nq = NSA_HEADS * hd
    page = cache_t.shape[3]
    row = lambda a: pl.BlockSpec((1, 1, a.shape[2]), lambda b, pt, ix: (b, 0, 0))
    grid_spec = pltpu.PrefetchScalarGridSpec(
        num_scalar_prefetch=2,
        grid=(B,),
        in_specs=[row(proj), row(o_c), row(kv_new),
                  pl.BlockSpec((1,) + win_t.shape[1:], lambda b, pt, ix: (b, 0, 0, 0)),
                  pl.BlockSpec(memory_space=pl.ANY)],
        out_specs=pl.BlockSpec((1, 1, nq), lambda b, pt, ix: (b, 0, 0)),
        scratch_shapes=[pltpu.VMEM((2, G * SLC_TOPK, hd, page), F32),
                        pltpu.VMEM((2, G * SLC_TOPK, hd, page), F32),
                        pltpu.SemaphoreType.DMA((2,))],
    )
    return pl.pallas_call(
        functools.partial(_nsa_sample_attend_kernel, t_pos=t_pos, past_len=past_len),
        grid_spec=grid_spec,
        out_shape=jax.ShapeDtypeStruct((B, 1, nq), F32),
        compiler_params=_cparams("arbitrary"),
        name="nsa_sample_attend",
    )(page_table, idx, proj, o_c, kv_new, win_t, cache_t)


def kernel(x_prompt, x_sample, cache_nsa_kv, cache_win_kv, state_hgrn, page_table, norm_mix, norm_mlp, w_mlp_up, w_mlp_down, w_hgrn_in, hgrn_lb_logits, hgrn_onorm, w_hgrn_out, norm_kv, w_kv, cmp_pe_k, cmp_w1_k, cmp_w2_k, cmp_pe_v, cmp_w1_v, cmp_w2_v, w_nsa_q, w_nsa_out, norm_final):
    B, T, D = x_prompt.shape
    Bs, Ts, _ = x_sample.shape
    G, hd = NSA_KV_GROUPS, NSA_HEAD_DIM
    n_pool, page = cache_nsa_kv.shape[:2]
    past_len = page_table.shape[1] * page
    w_buf = cache_win_kv.shape[1]
    assert Ts == 1 and T % 1024 == 0 and T >= WINDOW + 256 and past_len % SLC_BLOCK == 0 and w_buf <= past_len

    wab, wbig, pe, w2, w2t = _cmp_weights(cmp_pe_k, cmp_w1_k, cmp_pe_v, cmp_w1_v, cmp_w2_k, cmp_w2_v)
    wq = [_permute_gate_cols(w_nsa_q[l]) for l in range(DEPTH - N_A)]

    tm = 1024
    x = x_prompt.reshape(B * T, D)
    states_p = []
    for l in range(DEPTH):
        if l == N_A:
            nsa_p, win_p, k_hm, vt_hm = kv_proj_prompt(x, norm_kv, w_kv, B, T)
            kc_p, vct_p = compress_prompt(nsa_p, B, T, wbig, wab, pe, w2, w2t)
        if l < N_A:
            proj = rms_proj(x, norm_mix[l], w_hgrn_in[l], tm, 1024).reshape(B, T, -1)
            o, s_new = hgrn_prompt(proj, hgrn_lb_logits, hgrn_onorm[l], l)
            states_p.append(s_new)
            x = proj_res(o.reshape(B * T, -1), w_hgrn_out[l], x, tm)
        else:
            q_hm, gates_t = q_proj_prompt(x, norm_mix[l], wq[l - N_A], B, T)
            o = nsa_prompt(q_hm, gates_t, kc_p, vct_p, k_hm, vt_hm)
            x = proj_res(o.reshape(B * T, -1), w_nsa_out[l - N_A], x, tm)
        x = mlp_res(x, norm_mlp[l], w_mlp_up[l], w_mlp_down[l], norm_final, tm, 512, l == DEPTH - 1)
    y_prompt = x.reshape(B, T, D)
    nsa_kv_prompt = nsa_p.reshape(B, T, 4, G, hd)
    win_kv_prompt = win_p.reshape(B, T, 2, G, hd)[:, -min(WINDOW, T):]

    t_pos = past_len
    n_slc = -(-(past_len + 1) // SLC_BLOCK)
    xs = x_sample.reshape(Bs, D)
    cache_t = cache_nsa_kv.transpose(0, 2, 3, 4, 1).reshape(n_pool, 4, G * hd, page)
    win_t = cache_win_kv.transpose(0, 2, 3, 4, 1).reshape(Bs, 2, G * hd, w_buf)
    states_s = []
    for l in range(DEPTH):
        if l == N_A:
            kv_s = rms_proj(xs, norm_kv, w_kv, Bs, 512)
            cmp_s = compress_sample(cache_t, page_table, wbig, wab, pe, w2)
        if l < N_A:
            proj = rms_proj(xs, norm_mix[l], w_hgrn_in[l], Bs, 512).reshape(Bs, 1, -1)
            o, s_new = hgrn_step(proj, state_hgrn[l], hgrn_lb_logits, hgrn_onorm[l], l)
            states_s.append(s_new)
        else:
            proj = rms_proj(xs, norm_mix[l], wq[l - N_A], Bs, 384).reshape(Bs, 1, -1)
            o_c, idx = nsa_sample_select(proj, cmp_s, t_pos, n_slc)
            o = nsa_sample_attend(proj, o_c, kv_s.reshape(Bs, 1, -1), win_t, cache_t, page_table,
                                  idx.reshape(Bs, G * SLC_TOPK), t_pos, past_len)
        w_o = w_hgrn_out[l] if l < N_A else w_nsa_out[l - N_A]
        xs = proj_res(o.reshape(Bs, -1), w_o, xs, Bs)
        xs = mlp_res(xs, norm_mlp[l], w_mlp_up[l], w_mlp_down[l], norm_final, Bs, 512, l == DEPTH - 1)
    y_sample = xs.reshape(Bs, 1, D)
    n_nsa = 4 * G * hd
    nsa_kv_sample = kv_s[:, :n_nsa].reshape(Bs, 1, 4, G, hd)
    win_new = kv_s[:, n_nsa:].reshape(Bs, 1, 2, G, hd).astype(cache_win_kv.dtype)
    win_kv_sample = jnp.concatenate([cache_win_kv, win_new], axis=1)[:, -w_buf:]

    return (y_prompt, y_sample, nsa_kv_prompt, nsa_kv_sample, win_kv_prompt, win_kv_sample,
            jnp.stack(states_p), jnp.stack(states_s))
```

```python
import functools

import jax
import jax.numpy as jnp
from jax import lax
from jax.experimental import pallas as pl
from jax.experimental.pallas import tpu as pltpu

F32 = jnp.float32
BF16 = jnp.bfloat16

D_MODEL = 1024
DEPTH = 4
N_A = DEPTH // 2
D_FF = 4 * D_MODEL
RMS_EPS = 1e-6
HG_HEADS = 8
HG_K = 128
HG_V = 128
NSA_HEADS = 16
NSA_HEAD_DIM = 64
NSA_KV_GROUPS = 4
NSA_HPG = NSA_HEADS // NSA_KV_GROUPS
NSA_SCALE = NSA_HEAD_DIM ** -0.5
CMP_LEN = 32
CMP_STRIDE = 16
SLC_BLOCK = 64
SLC_TOPK = 16
WINDOW = 512
FORCED_SCORE = 1e4
NEG_INF = -1e30
TINY = 1e-30

HG_CHUNK = 128
HG_MATRIX_LEVELS = 2
VMEM_LIMIT = 56 * 1024 * 1024


def _cparams(*sem):
    return pltpu.CompilerParams(dimension_semantics=sem, vmem_limit_bytes=VMEM_LIMIT)


def _rms(x, g):
    return x * lax.rsqrt(jnp.mean(x * x, axis=-1, keepdims=True) + RMS_EPS) * g


def _sigmoid(x):
    return 1.0 / (1.0 + jnp.exp(-x))


def _dot(a, b):
    return jnp.dot(a, b, preferred_element_type=F32)


def _dot_nt(a, b):
    return lax.dot_general(a, b, (((1,), (1,)), ((), ())), preferred_element_type=F32)


def _dot_tn(a, b):
    return lax.dot_general(a, b, (((0,), (0,)), ((), ())), preferred_element_type=F32)


def _split3(x):
    hi = x.astype(BF16)
    r1 = x - hi.astype(F32)
    mid = r1.astype(BF16)
    lo = (r1 - mid.astype(F32)).astype(BF16)
    return hi, mid, lo


def _masked_softmax(s, mask):
    s = jnp.where(mask, s, NEG_INF)
    e = jnp.where(mask, jnp.exp(s - jnp.max(s, axis=-1, keepdims=True)), 0.0)
    return e / jnp.maximum(jnp.sum(e, axis=-1, keepdims=True), TINY)


def _rms_proj_kernel(x_ref, g_ref, w_ref, o_ref, y_ref):
    @pl.when(pl.program_id(1) == 0)
    def _():
        y_ref[...] = _rms(x_ref[...], g_ref[...]).astype(BF16)

    o_ref[...] = _dot(y_ref[...], w_ref[...].astype(BF16)).astype(o_ref.dtype)


def rms_proj(x, g, w, tm, tn, out_dtype=F32):
    M, D = x.shape
    N = w.shape[1]
    return pl.pallas_call(
        _rms_proj_kernel,
        grid=(M // tm, N // tn),
        in_specs=[pl.BlockSpec((tm, D), lambda i, j: (i, 0)),
                  pl.BlockSpec((1, D), lambda i, j: (0, 0)),
                  pl.BlockSpec((D, tn), lambda i, j: (0, j))],
        out_specs=pl.BlockSpec((tm, tn), lambda i, j: (i, j)),
        out_shape=jax.ShapeDtypeStruct((M, N), out_dtype),
        scratch_shapes=[pltpu.VMEM((tm, D), BF16)],
        compiler_params=_cparams("parallel", "arbitrary"),
        name="rms_proj",
    )(x, g.reshape(1, D), w)


def _proj_res_kernel(a_ref, w_ref, r_ref, o_ref, wb_ref):
    @pl.when(pl.program_id(0) == 0)
    def _():
        wb_ref[...] = w_ref[...].astype(BF16)

    o_ref[...] = r_ref[...] + _dot(a_ref[...].astype(BF16), wb_ref[...])


def proj_res(a, w, res, tm):
    M, K = a.shape
    N = w.shape[1]
    return pl.pallas_call(
        _proj_res_kernel,
        grid=(M // tm,),
        in_specs=[pl.BlockSpec((tm, K), lambda i: (i, 0)),
                  pl.BlockSpec((K, N), lambda i: (0, 0)),
                  pl.BlockSpec((tm, N), lambda i: (i, 0))],
        out_specs=pl.BlockSpec((tm, N), lambda i: (i, 0)),
        out_shape=jax.ShapeDtypeStruct((M, N), F32),
        scratch_shapes=[pltpu.VMEM((K, N), BF16)],
        compiler_params=_cparams("arbitrary"),
        name="proj_res",
    )(a, w, res)


def _mlp_kernel(x_ref, g_ref, wu_ref, wd_ref, gf_ref, o_ref, y_ref, acc_ref, *, final_norm):
    f = pl.program_id(1)

    @pl.when(f == 0)
    def _():
        y_ref[...] = _rms(x_ref[...], g_ref[...]).astype(BF16)
        acc_ref[...] = jnp.zeros_like(acc_ref)

    h = jnp.maximum(_dot(y_ref[...], wu_ref[...].astype(BF16)), 0.0)
    acc_ref[...] += _dot((h * h).astype(BF16), wd_ref[...].astype(BF16))

    @pl.when(f == pl.num_programs(1) - 1)
    def _():
        out = x_ref[...] + acc_ref[...]
        if final_norm:
            out = _rms(out, gf_ref[...])
        o_ref[...] = out


def mlp_res(x, g, w_up, w_down, g_final, tm, tf, final_norm):
    M, D = x.shape
    Fdim = w_up.shape[1]
    return pl.pallas_call(
        functools.partial(_mlp_kernel, final_norm=final_norm),
        grid=(M // tm, Fdim // tf),
        in_specs=[pl.BlockSpec((tm, D), lambda i, f: (i, 0)),
                  pl.BlockSpec((1, D), lambda i, f: (0, 0)),
                  pl.BlockSpec((D, tf), lambda i, f: (0, f)),
                  pl.BlockSpec((tf, D), lambda i, f: (f, 0)),
                  pl.BlockSpec((1, D), lambda i, f: (0, 0))],
        out_specs=pl.BlockSpec((tm, D), lambda i, f: (i, 0)),
        out_shape=jax.ShapeDtypeStruct((M, D), F32),
        scratch_shapes=[pltpu.VMEM((tm, D), BF16), pltpu.VMEM((tm, D), F32)],
        compiler_params=_cparams("parallel", "arbitrary"),
        name="mlp_res",
    )(x, g.reshape(1, D), w_up, w_down, g_final.reshape(1, D))


def _hgrn_lower_bound(lg, layer):
    m = jnp.max(lg, axis=0, keepdims=True)
    e = jnp.exp(lg - m)
    p = e / jnp.sum(e, axis=0, keepdims=True)
    lb = jnp.sum(p[1:layer + 1], axis=0, keepdims=True)
    return jnp.log(lb), jnp.log(1.0 - lb)


def _hgrn_logf(z, lg, layer):
    ls = jnp.minimum(z, 0.0) - jnp.log(1.0 + jnp.exp(-jnp.abs(z)))
    if layer == 0:
        return ls
    log_lb, log1m = _hgrn_lower_bound(lg, layer)
    b2 = log1m + ls
    return jnp.maximum(log_lb, b2) + jnp.log(1.0 + jnp.exp(-jnp.abs(log_lb - b2)))


def _hgrn_sum_matrices(C, n_lev):
    r = lax.broadcasted_iota(jnp.int32, (C, C), 0)
    u = lax.broadcasted_iota(jnp.int32, (C, C), 1)
    mats = [r >= u]
    for lev in range(n_lev):
        h = 1 << lev
        off = r & (2 * h - 1)
        mid = r - off + h
        mats.append(((off >= h) & (u >= mid) & (u <= r)) | ((off < h) & (u > r) & (u < mid)))
    return jnp.concatenate([jnp.where(m, 1.0, 0.0).astype(BF16) for m in mats], axis=0)


def _hgrn_kernel(zq_ref, zf_ref, zi_ref, zg_ref, lg_ref, on_ref, o_ref, s_ref, st_ref, w_ref, b_ref,
                 *, layer, tc, nh):
    t = pl.program_id(2)
    C = HG_CHUNK
    n_lev = C.bit_length() - 1
    heads = range(nh)

    @pl.when(t == 0)
    def _():
        st_ref[...] = jnp.zeros_like(st_ref)
        w_ref[...] = _hgrn_sum_matrices(C, HG_MATRIX_LEVELS)

    r_i = lax.broadcasted_iota(jnp.int32, (C, C), 0)
    c_i = lax.broadcasted_iota(jnp.int32, (C, C), 1)
    row = lax.broadcasted_iota(jnp.int32, (C, HG_K), 0)

    def chunk(ci, sts):
        rows = pl.ds(pl.multiple_of(ci * C, C), C)
        kl = [slice(h * HG_K, (h + 1) * HG_K) for h in heads]
        vl = [slice(h * HG_V, (h + 1) * HG_V) for h in heads]
        q = [zq_ref[0, rows, kl[h]] for h in heads]
        vb = [zi_ref[0, rows, vl[h]].astype(BF16) for h in heads]
        logf = [_hgrn_logf(zf_ref[0, rows, kl[h]], lg_ref[:, kl[h]], layer) for h in heads]
        k = [1.0 - jnp.exp(logf[h]) for h in heads]
        parts = [_split3(logf[h]) for h in heads]
        w_sum, w_lev = w_ref[0:C, :], w_ref[C:, :]
        b = [_dot(w_sum, parts[h][0]) + _dot(w_sum, parts[h][1]) + _dot(w_sum, parts[h][2]) for h in heads]
        for h in heads:
            b_ref[h] = b[h]
        e_low = [jnp.exp(_dot(w_lev, parts[h][0]) + _dot(w_lev, parts[h][1])) for h in heads]
        a = [jnp.where(r_i == c_i, _dot_nt(q[h].astype(BF16), k[h].astype(BF16)), 0.0) for h in heads]
        for lev in range(n_lev):
            half = 1 << lev
            upper = (row & (2 * half - 1)) >= half
            same = (r_i >> (lev + 1)) == (c_i >> (lev + 1))
            if lev < HG_MATRIX_LEVELS:
                e = [e_low[h][lev * C:(lev + 1) * C] for h in heads]
            else:
                nblk = C // (2 * half)
                e = []
                for h in heads:
                    bm = b_ref[h, pl.ds(half - 1, nblk, stride=2 * half), :] if nblk > 1 else b_ref[h, half - 1:half, :]
                    bm = jnp.broadcast_to(bm[:, None, :], (nblk, 2 * half, HG_K)).reshape(C, HG_K)
                    e.append(jnp.exp(jnp.where(upper, b[h] - bm, bm - b[h])))
            qt = [jnp.where(upper, q[h] * e[h], 0.0).astype(BF16) for h in heads]
            kt = [jnp.where(upper, 0.0, k[h] * e[h]).astype(BF16) for h in heads]
            al = [_dot_nt(qt[h], kt[h]) for h in heads]
            a = [a[h] + (jnp.where(same, al[h], 0.0) if 2 * half < C else al[h]) for h in heads]
        qd = [(q[h] * jnp.exp(b[h])).astype(BF16) for h in heads]
        o = [_dot(a[h].astype(BF16), vb[h]) + _dot_nt(qd[h], sts[h].astype(BF16)) for h in heads]
        b_end = [b[h][C - 1:C] for h in heads]
        kd = [(k[h] * jnp.exp(b_end[h] - b[h])).astype(BF16) for h in heads]
        new = tuple(sts[h] * jnp.exp(b_end[h]) + _dot_tn(vb[h], kd[h]) for h in heads)
        for h in heads:
            oh = o[h] * lax.rsqrt(jnp.mean(o[h] * o[h], axis=-1, keepdims=True) + RMS_EPS) * on_ref[:, vl[h]]
            zg = zg_ref[0, rows, vl[h]]
            o_ref[0, rows, vl[h]] = (oh * (zg * _sigmoid(zg))).astype(o_ref.dtype)
        return new

    sts = lax.fori_loop(0, tc // C, chunk, tuple(st_ref[h] for h in heads))
    for h in heads:
        st_ref[h] = sts[h]

    @pl.when(t == pl.num_programs(2) - 1)
    def _():
        for h in heads:
            s_ref[0, h] = st_ref[h].T


def hgrn_prompt(proj, lb_logits, onorm, layer, tc=512, nh=4):
    B, T, _ = proj.shape
    H = HG_HEADS
    hp = H // nh
    n_mats = 1 + HG_MATRIX_LEVELS
    return pl.pallas_call(
        functools.partial(_hgrn_kernel, layer=layer, tc=tc, nh=nh),
        grid=(B, hp, T // tc),
        in_specs=[pl.BlockSpec((1, tc, nh * HG_K), lambda b, h, t: (b, t, h)),
                  pl.BlockSpec((1, tc, nh * HG_K), lambda b, h, t: (b, t, hp + h)),
                  pl.BlockSpec((1, tc, nh * HG_V), lambda b, h, t: (b, t, 2 * hp + h)),
                  pl.BlockSpec((1, tc, nh * HG_V), lambda b, h, t: (b, t, 3 * hp + h)),
                  pl.BlockSpec((N_A, nh * HG_K), lambda b, h, t: (0, h)),
                  pl.BlockSpec((1, nh * HG_V), lambda b, h, t: (0, h))],
        out_specs=[pl.BlockSpec((1, tc, nh * HG_V), lambda b, h, t: (b, t, h)),
                   pl.BlockSpec((1, nh, HG_K, HG_V), lambda b, h, t: (b, h, 0, 0))],
        out_shape=[jax.ShapeDtypeStruct((B, T, H * HG_V), BF16),
                   jax.ShapeDtypeStruct((B, H, HG_K, HG_V), F32)],
        scratch_shapes=[pltpu.VMEM((nh, HG_V, HG_K), F32),
                        pltpu.VMEM((n_mats * HG_CHUNK, HG_CHUNK), BF16),
                        pltpu.VMEM((nh, HG_CHUNK, HG_K), F32)],
        compiler_params=_cparams("parallel", "parallel", "arbitrary"),
        name="hgrn_prompt",
    )(proj, proj, proj, proj, lb_logits, onorm.reshape(1, H * HG_V))


QK_LANES = 2 * NSA_HEAD_DIM
VT_ROWS = NSA_HEAD_DIM + 16


def _kv_proj_kernel(x_ref, g_ref, w_ref, nsa_ref, win_ref, k_ref, vt_ref, wb_ref, *, tpb):
    @pl.when(pl.program_id(0) == 0)
    def _():
        wb_ref[...] = w_ref[...].astype(BF16)

    G, hd = NSA_KV_GROUPS, NSA_HEAD_DIM
    tm = x_ref.shape[0]
    y = _rms(x_ref[...], g_ref[...]).astype(BF16)
    kv = _dot(y, wb_ref[...])
    n_nsa = nsa_ref.shape[1]
    nsa_ref[...] = kv[:, :n_nsa]
    win_ref[...] = kv[:, n_nsa:]
    lane = lax.broadcasted_iota(jnp.int32, (tm, QK_LANES), 1)
    blk = ((pl.program_id(0) % tpb) * tm + lax.broadcasted_iota(jnp.int32, (tm, QK_LANES), 0)) // SLC_BLOCK
    tails = (jnp.where(lane - hd == blk, 1.0, 0.0), jnp.zeros((tm, QK_LANES), F32))
    for n, kind in enumerate((2, 4)):
        for gp in range(G // 2):
            col = (kind * G + 2 * gp) * hd
            pair = kv[:, col:col + 2 * hd]
            for gl, src in enumerate((pair, pltpu.roll(pair, hd, axis=1))):
                k_ref[0, n * G + 2 * gp + gl] = jnp.where(lane < hd, src, tails[n]).astype(BF16)
    ones_row = jnp.where(lax.broadcasted_iota(jnp.int32, (VT_ROWS - hd, tm), 0) == 0, 1.0, 0.0).astype(BF16)
    for n, kind in enumerate((3, 5)):
        for gp in range(G // 2):
            col = (kind * G + 2 * gp) * hd
            t = kv[:, col:col + 2 * hd].T
            for gl in range(2):
                vt_ref[0, n * G + 2 * gp + gl, 0:hd, :] = t[gl * hd:(gl + 1) * hd].astype(BF16)
                vt_ref[0, n * G + 2 * gp + gl, hd:, :] = ones_row


def kv_proj_prompt(x, g, w_kv, B, T, tm=512):
    M, D = x.shape
    N = w_kv.shape[1]
    G, hd = NSA_KV_GROUPS, NSA_HEAD_DIM
    n_nsa = 4 * G * hd
    tpb = T // tm
    assert T // SLC_BLOCK <= QK_LANES - hd
    return pl.pallas_call(
        functools.partial(_kv_proj_kernel, tpb=tpb),
        grid=(M // tm,),
        in_specs=[pl.BlockSpec((tm, D), lambda i: (i, 0)),
                  pl.BlockSpec((1, D), lambda i: (0, 0)),
                  pl.BlockSpec((D, N), lambda i: (0, 0))],
        out_specs=[pl.BlockSpec((tm, n_nsa), lambda i: (i, 0)),
                   pl.BlockSpec((tm, N - n_nsa), lambda i: (i, 0)),
                   pl.BlockSpec((1, 2 * G, tm, QK_LANES), lambda i: (i // tpb, 0, i % tpb, 0)),
                   pl.BlockSpec((1, 2 * G, VT_ROWS, tm), lambda i: (i // tpb, 0, 0, i % tpb))],
        out_shape=[jax.ShapeDtypeStruct((M, n_nsa), F32),
                   jax.ShapeDtypeStruct((M, N - n_nsa), F32),
                   jax.ShapeDtypeStruct((B, 2 * G, T, QK_LANES), BF16),
                   jax.ShapeDtypeStruct((B, 2 * G, VT_ROWS, T), BF16)],
        scratch_shapes=[pltpu.VMEM((D, N), BF16)],
        compiler_params=_cparams("arbitrary"),
        name="kv_proj",
    )(x, g.reshape(1, D), w_kv)


GATE_ROWS = 16
GATE_LANES = 128


def _q_proj_kernel(x_ref, g_ref, w_ref, q_ref, gt_ref, wb_ref):
    @pl.when(pl.program_id(0) == 0)
    def _():
        wb_ref[...] = w_ref[...].astype(BF16)

    y = _rms(x_ref[...], g_ref[...]).astype(BF16)
    pr = _dot(y, wb_ref[...])
    hd = NSA_HEAD_DIM
    nq = NSA_HEADS * hd
    low = lax.broadcasted_iota(jnp.int32, (pr.shape[0], QK_LANES), 1) < hd
    for hp in range(NSA_HEADS // 2):
        pair = pr[:, hp * 2 * hd:(hp + 1) * 2 * hd] * NSA_SCALE
        for hl, src in enumerate((pair, pltpu.roll(pair, hd, axis=1))):
            q_ref[0, 2 * hp + hl] = jnp.where(low, src, 0.0).astype(BF16)
    gates_t = _sigmoid(pr[:, nq:]).T
    for gi in range(NSA_KV_GROUPS):
        gt_ref[0, gi] = gates_t[gi * GATE_ROWS:(gi + 1) * GATE_ROWS]


def _permute_gate_cols(w_q):
    nq = NSA_HEADS * NSA_HEAD_DIM
    d = w_q.shape[0]
    wg = w_q[:, nq:].reshape(d, 3, NSA_KV_GROUPS, NSA_HPG).transpose(0, 2, 1, 3).reshape(d, NSA_KV_GROUPS, 3 * NSA_HPG)
    wg = jnp.pad(wg, ((0, 0), (0, 0), (0, GATE_ROWS - 3 * NSA_HPG))).reshape(d, NSA_KV_GROUPS * GATE_ROWS)
    wg = jnp.pad(wg, ((0, 0), (0, GATE_LANES - NSA_KV_GROUPS * GATE_ROWS)))
    return jnp.concatenate([w_q[:, :nq], wg], axis=1)


def q_proj_prompt(x, g, w_qp, B, T, tm=512):
    M, D = x.shape
    N = w_qp.shape[1]
    tpb = T // tm
    return pl.pallas_call(
        _q_proj_kernel,
        grid=(M // tm,),
        in_specs=[pl.BlockSpec((tm, D), lambda i: (i, 0)),
                  pl.BlockSpec((1, D), lambda i: (0, 0)),
                  pl.BlockSpec((D, N), lambda i: (0, 0))],
        out_specs=[pl.BlockSpec((1, NSA_HEADS, tm, QK_LANES), lambda i: (i // tpb, 0, i % tpb, 0)),
                   pl.BlockSpec((1, NSA_KV_GROUPS, GATE_ROWS, tm), lambda i: (i // tpb, 0, 0, i % tpb))],
        out_shape=[jax.ShapeDtypeStruct((B, NSA_HEADS, T, QK_LANES), BF16),
                   jax.ShapeDtypeStruct((B, NSA_KV_GROUPS, GATE_ROWS, T), F32)],
        scratch_shapes=[pltpu.VMEM((D, N), BF16)],
        compiler_params=_cparams("arbitrary"),
        name="q_proj",
    )(x, g.reshape(1, D), w_qp)


def _cmp_weights(pe_k, w1_k, pe_v, w1_v, w2_k, w2_v):
    half = CMP_STRIDE * NSA_HEAD_DIM

    def ab(w1):
        return jnp.concatenate([w1[:half], w1[half:]], axis=1)

    def big(w1):
        w = w1.reshape(2, CMP_STRIDE, NSA_HEAD_DIM, -1)
        b = jnp.einsum("alds,gh->lgdhas", w, jnp.eye(2, dtype=w1.dtype))
        return b.reshape(CMP_STRIDE * 2 * NSA_HEAD_DIM, 2 * 2 * w.shape[-1])

    wab = jnp.stack([ab(w1_k), ab(w1_v)])
    wbig = jnp.stack([big(w1_k), big(w1_v)])
    pe = jnp.stack([pe_k.reshape(2, half), pe_v.reshape(2, half)])
    w2 = jnp.stack([w2_k, w2_v])
    w2t = jnp.stack([w2_k.T, w2_v.T])
    return wab, wbig, pe, w2, w2t


def _cmp_taps(x_ref, ns):
    return jnp.concatenate([x_ref[pl.ds(l, ns, stride=CMP_STRIDE), :].astype(BF16) for l in range(CMP_STRIDE)], axis=1)


def _cmp_hidden(ab, pe, wab):
    hd = NSA_HEAD_DIM
    n = ab.shape[0]
    nxt = pltpu.roll(ab, n - 1, axis=0)
    pt = _dot(pe.astype(BF16), wab)
    hid = ab[:, :hd] + nxt[:, hd:] + pt[0:1, :hd] + pt[1:2, hd:]
    return hid * _sigmoid(hid)


def _cmp_prompt_kernel(xk_ref, xv_ref, wbig_ref, wab_ref, pe_ref, w2_ref, w2t_ref, kc_ref, vct_ref):
    hd = NSA_HEAD_DIM
    ns = xk_ref.shape[0] // CMP_STRIDE
    for c, x_ref in enumerate((xk_ref, xv_ref)):
        wab = wab_ref[c].astype(BF16)
        ab2 = _dot(_cmp_taps(x_ref, ns), wbig_ref[c].astype(BF16))
        for gl in range(2):
            act = _cmp_hidden(ab2[:, gl * 2 * hd:(gl + 1) * 2 * hd], pe_ref[c], wab).astype(BF16)
            if c == 0:
                kc_ref[0, gl] = _dot(act, w2_ref[c].astype(BF16)).astype(kc_ref.dtype)
            else:
                vct_ref[0, gl] = _dot_nt(w2t_ref[c].astype(BF16), act).astype(vct_ref.dtype)


def compress_prompt(nsa, B, T, wbig, wab, pe, w2, w2t):
    G, hd = NSA_KV_GROUPS, NSA_HEAD_DIM
    ns = T // CMP_STRIDE
    w2 = jnp.pad(w2, ((0, 0), (0, 0), (0, QK_LANES - hd)))
    const = lambda a: pl.BlockSpec(a.shape, lambda b, gp: (0,) * a.ndim)
    return pl.pallas_call(
        _cmp_prompt_kernel,
        grid=(B, G // 2),
        in_specs=[pl.BlockSpec((T, 2 * hd), lambda b, gp: (b, gp)),
                  pl.BlockSpec((T, 2 * hd), lambda b, gp: (b, G // 2 + gp)),
                  const(wbig), const(wab), const(pe), const(w2), const(w2t)],
        out_specs=[pl.BlockSpec((1, 2, ns, QK_LANES), lambda b, gp: (b, gp, 0, 0)),
                   pl.BlockSpec((1, 2, hd, ns), lambda b, gp: (b, gp, 0, 0))],
        out_shape=[jax.ShapeDtypeStruct((B, G, ns, QK_LANES), BF16),
                   jax.ShapeDtypeStruct((B, G, hd, ns), BF16)],
        compiler_params=_cparams("parallel", "parallel"),
        name="compress_prompt",
    )(nsa, nsa, wbig, wab, pe, w2, w2t)


def _nsa_prompt_kernel(q_ref, gt_ref, kc_ref, vct_ref, ks_ref, vst_ref, kw_ref, vwt_ref, o_ref, *, tq):
    i = pl.program_id(2)
    s0 = i * tq
    hpg, hd = NSA_HPG, NSA_HEAD_DIM
    T = ks_ref.shape[2]
    n_cmp = kc_ref.shape[2]
    n_slc = T // SLC_BLOCK
    R = hpg * tq
    Q = q_ref[0].reshape(R, QK_LANES)
    tpos = s0 + lax.broadcasted_iota(jnp.int32, (1, R), 1) % tq

    sc = _dot_nt(kc_ref[0, 0], Q)
    ok_c = lax.broadcasted_iota(jnp.int32, (n_cmp, R), 0) * CMP_STRIDE + (CMP_LEN - 1) <= tpos
    sc = jnp.where(ok_c, sc, NEG_INF)
    e_c = jnp.where(ok_c, jnp.exp(sc - jnp.max(sc, axis=0, keepdims=True)), 0.0)
    p_c = e_c / jnp.maximum(jnp.sum(e_c, axis=0, keepdims=True), TINY)
    o_c = _dot(vct_ref[0, 0], p_c.astype(BF16))

    psum = p_c[:, 0:tq]
    for h in range(1, hpg):
        psum = psum + p_c[:, h * tq:(h + 1) * tq]
    sj = lax.broadcasted_iota(jnp.int32, (n_slc, n_cmp), 0) * SLC_BLOCK
    ci = lax.broadcasted_iota(jnp.int32, (n_slc, n_cmp), 1) * CMP_STRIDE
    ov = jnp.where((ci < sj + SLC_BLOCK) & (ci + CMP_LEN > sj), 1.0, 0.0).astype(BF16)
    hi, mid, lo = _split3(psum)
    imp = _dot(ov, hi) + _dot(ov, mid) + _dot(ov, lo)
    blk = lax.broadcasted_iota(jnp.int32, (n_slc, tq), 0)
    qpos = s0 + lax.broadcasted_iota(jnp.int32, (n_slc, tq), 1)
    cur = qpos // SLC_BLOCK
    forced = (blk == 0) | (blk == cur) | (blk == cur - 1)
    score = jnp.where(forced, FORCED_SCORE, jnp.where(blk <= cur, imp, -1.0))
    rank = jnp.zeros((n_slc, tq), F32)
    for j in range(n_slc):
        cj = score[j:j + 1, :]
        rank = rank + jnp.where((cj > score) | ((cj == score) & (blk > j)), 1.0, 0.0)
    bias = jnp.where(rank < SLC_TOPK, 0.0, NEG_INF)
    bias = jnp.concatenate([jnp.zeros((hd, tq), F32), bias, jnp.zeros((QK_LANES - hd - n_slc, tq), F32)], axis=0)
    bias_t = bias.T.astype(BF16)
    q_sel = Q + jnp.concatenate([bias_t] * hpg, axis=0)

    heads = range(hpg)
    q_heads = [q_sel[h * tq:(h + 1) * tq] for h in heads]
    tpos_h = tpos[:, :tq]

    def chunk(c, carry, diagonal):
        k0 = pl.multiple_of(c * tq, tq)
        ks = ks_ref[0, 0, pl.ds(k0, tq), :]
        vst = vst_ref[0, 0, :, pl.ds(k0, tq)]
        s_heads = [_dot_nt(ks, q_heads[h]) for h in heads]
        new = []
        for h in heads:
            m, acc = carry[h]
            s = s_heads[h]
            if diagonal:
                s = jnp.where(k0 + lax.broadcasted_iota(jnp.int32, (tq, tq), 0) <= tpos_h, s, NEG_INF)
            m_new = jnp.maximum(m, jnp.max(s, axis=0, keepdims=True))
            p = jnp.exp(s - m_new)
            new.append((m_new, jnp.exp(m - m_new) * acc + _dot(vst, p.astype(BF16))))
        return tuple(new)

    carry = tuple((jnp.full((1, tq), NEG_INF, F32), jnp.zeros((VT_ROWS, tq), F32)) for _ in heads)
    carry = lax.fori_loop(0, i, lambda c, cr: chunk(c, cr, False), carry)
    carry = chunk(i, carry, True)
    o_s = jnp.concatenate([acc[:hd] / jnp.maximum(acc[hd:hd + 1], TINY) for _, acc in carry], axis=1)

    ws = pl.multiple_of(jnp.maximum(s0 - WINDOW, 0), tq)
    parts = []
    for j in range(WINDOW // tq + 1):
        k0 = pl.multiple_of(ws + j * tq, tq)
        s = _dot_nt(kw_ref[0, 0, pl.ds(k0, tq), :], Q)
        wpos = k0 + lax.broadcasted_iota(jnp.int32, (tq, R), 0)
        ok = (wpos <= tpos) & (wpos > tpos - WINDOW) if j == 0 else wpos <= tpos
        parts.append((k0, jnp.where(ok, s, NEG_INF)))
    m_w = functools.reduce(jnp.maximum, [jnp.max(s, axis=0, keepdims=True) for _, s in parts])
    acc_w = sum(_dot(vwt_ref[0, 0, :, pl.ds(k0, tq)], jnp.exp(s - m_w).astype(BF16)) for k0, s in parts)
    o_w = acc_w[:hd] / jnp.maximum(acc_w[hd:hd + 1], TINY)

    gt = gt_ref[0, 0]
    outs = []
    for h in range(hpg):
        cols = slice(h * tq, (h + 1) * tq)
        outs.append(gt[h:h + 1] * o_c[:, cols] + gt[hpg + h:hpg + h + 1] * o_s[:, cols]
                    + gt[2 * hpg + h:2 * hpg + h + 1] * o_w[:, cols])
    for pair in range(hpg // 2):
        both = jnp.concatenate(outs[2 * pair:2 * pair + 2], axis=0)
        o_ref[0, :, pair * 2 * hd:(pair + 1) * 2 * hd] = both.T.astype(o_ref.dtype)


def nsa_prompt(q_hm, gates_t, kc, vct, k_hm, vt_hm, tq=256):
    B, _, T, _ = q_hm.shape
    G, hpg, hd = NSA_KV_GROUPS, NSA_HPG, NSA_HEAD_DIM
    n_cmp = kc.shape[2]
    assert WINDOW % tq == 0 and T >= WINDOW + tq
    keys = lambda off: pl.BlockSpec((1, 1, T, QK_LANES), lambda b, g, i: (b, off + g, 0, 0))
    vals = lambda off: pl.BlockSpec((1, 1, VT_ROWS, T), lambda b, g, i: (b, off + g, 0, 0))
    return pl.pallas_call(
        functools.partial(_nsa_prompt_kernel, tq=tq),
        grid=(B, G, T // tq),
        in_specs=[pl.BlockSpec((1, hpg, tq, QK_LANES), lambda b, g, i: (b, g, i, 0)),
                  pl.BlockSpec((1, 1, GATE_ROWS, tq), lambda b, g, i: (b, g, 0, i)),
                  pl.BlockSpec((1, 1, n_cmp, QK_LANES), lambda b, g, i: (b, g, 0, 0)),
                  pl.BlockSpec((1, 1, hd, n_cmp), lambda b, g, i: (b, g, 0, 0)),
                  keys(0), vals(0), keys(G), vals(G)],
        out_specs=pl.BlockSpec((1, tq, hpg * hd), lambda b, g, i: (b, i, g)),
        out_shape=jax.ShapeDtypeStruct((B, T, NSA_HEADS * hd), BF16),
        compiler_params=_cparams("parallel", "parallel", "arbitrary"),
        name="nsa_prompt",
    )(q_hm, gates_t, kc, vct, k_hm, vt_hm, k_hm, vt_hm)


def _row_to_col(row):
    n = row.shape[1]
    eye = lax.broadcasted_iota(jnp.int32, (n, n), 0) == lax.broadcasted_iota(jnp.int32, (n, n), 1)
    return jnp.sum(jnp.where(eye, jnp.broadcast_to(row, (n, n)), 0.0), axis=-1, keepdims=True)


def _hgrn_step_kernel(z_ref, s0_ref, lg_ref, on_ref, o_ref, s_ref, *, layer):
    hk = HG_HEADS * HG_K
    hv = HG_HEADS * HG_V
    for h in range(HG_HEADS):
        kl = slice(h * HG_K, (h + 1) * HG_K)
        vl = slice(h * HG_V, (h + 1) * HG_V)
        q = z_ref[0, :, kl]
        logf = _hgrn_logf(z_ref[0, :, hk + h * HG_K:hk + (h + 1) * HG_K], lg_ref[:, kl], layer)
        f = jnp.exp(logf)
        v = z_ref[0, :, 2 * hk + h * HG_V:2 * hk + (h + 1) * HG_V]
        zg = z_ref[0, :, 2 * hk + hv + h * HG_V:2 * hk + hv + (h + 1) * HG_V]
        s = _row_to_col(f) * s0_ref[0, h] + _row_to_col(1.0 - f) * v
        s_ref[0, h] = s
        o = jnp.sum(_row_to_col(q) * s, axis=0, keepdims=True)
        o = o * lax.rsqrt(jnp.mean(o * o, axis=-1, keepdims=True) + RMS_EPS) * on_ref[:, vl]
        o_ref[0, :, vl] = o * (zg * _sigmoid(zg))


def hgrn_step(proj, s0, lb_logits, onorm, layer):
    B = proj.shape[0]
    H = HG_HEADS
    return pl.pallas_call(
        functools.partial(_hgrn_step_kernel, layer=layer),
        grid=(B,),
        in_specs=[pl.BlockSpec((1, 1, proj.shape[2]), lambda b: (b, 0, 0)),
                  pl.BlockSpec((1, H, HG_K, HG_V), lambda b: (b, 0, 0, 0)),
                  pl.BlockSpec((N_A, H * HG_K), lambda b: (0, 0)),
                  pl.BlockSpec((1, H * HG_V), lambda b: (0, 0))],
        out_specs=[pl.BlockSpec((1, 1, H * HG_V), lambda b: (b, 0, 0)),
                   pl.BlockSpec((1, H, HG_K, HG_V), lambda b: (b, 0, 0, 0))],
        out_shape=[jax.ShapeDtypeStruct((B, 1, H * HG_V), F32),
                   jax.ShapeDtypeStruct((B, H, HG_K, HG_V), F32)],
        compiler_params=_cparams("parallel"),
        name="hgrn_step",
    )(proj, s0, lb_logits, onorm.reshape(1, H * HG_V))


def _cmp_sample_kernel(pt_ref, cache_ref, wbig_ref, wab_ref, pe_ref, w2_ref, o_ref, raw_ref, buf_ref, ab_ref, wb_ref,
                       sem, *, n_pages):
    b = pl.program_id(0)
    nb = pl.num_programs(0)
    G, hd = NSA_KV_GROUPS, NSA_HEAD_DIM
    hp = n_pages // 2
    spp = cache_ref.shape[3] // CMP_STRIDE
    nsh = hp * spp

    def page_copy(bb, half, p):
        src = cache_ref.at[pt_ref[bb, half * hp + p], pl.ds(0, 2)]
        return pltpu.make_async_copy(src, raw_ref.at[half, p], sem.at[half])

    def start_half(bb, half):
        def body(p, c):
            page_copy(bb, half, p).start()
            return c
        lax.fori_loop(0, hp, body, 0)

    def wait_half(bb, half):
        def body(p, c):
            page_copy(bb, half, p).wait()
            return c
        lax.fori_loop(0, hp, body, 0)

    def compute_half(half):
        def to_token_rows(p, c):
            for kind in range(2):
                for gp in range(G // 2):
                    buf_ref[kind * (G // 2) + gp, p] = raw_ref[half, p, kind, gp * 2 * hd:(gp + 1) * 2 * hd, :].T
            return c
        lax.fori_loop(0, hp, to_token_rows, 0)
        for cb in range(G):
            x = jnp.concatenate(
                [buf_ref[cb, :, pl.ds(l, spp, stride=CMP_STRIDE), :].reshape(nsh, 2 * hd).astype(BF16)
                 for l in range(CMP_STRIDE)], axis=1)
            ab_ref[cb, half * nsh:(half + 1) * nsh, :] = _dot(x, wb_ref[cb // (G // 2)])

    @pl.when(b == 0)
    def _():
        start_half(0, 0)
        wb_ref[...] = wbig_ref[...].astype(BF16)

    start_half(b, 1)
    wait_half(b, 0)
    compute_half(0)

    @pl.when(b + 1 < nb)
    def _():
        start_half(b + 1, 0)

    wait_half(b, 1)
    compute_half(1)

    for c in range(2):
        wab = wab_ref[c].astype(BF16)
        w2 = w2_ref[c].astype(BF16)
        for gp in range(G // 2):
            ab2 = ab_ref[c * (G // 2) + gp]
            for gl in range(2):
                act = _cmp_hidden(ab2[:, gl * 2 * hd:(gl + 1) * 2 * hd], pe_ref[c], wab)
                col = (c * G + 2 * gp + gl) * hd
                o_ref[0, :, col:col + hd] = _dot(act.astype(BF16), w2).astype(o_ref.dtype)


def compress_sample(cache_t, page_table, wbig, wab, pe, w2):
    B, n_pages = page_table.shape
    page = cache_t.shape[3]
    G, hd = NSA_KV_GROUPS, NSA_HEAD_DIM
    ns = n_pages * page // CMP_STRIDE
    assert n_pages % 2 == 0 and page == 2 * hd
    const = lambda a: pl.BlockSpec(a.shape, lambda b, pt: (0,) * a.ndim)
    grid_spec = pltpu.PrefetchScalarGridSpec(
        num_scalar_prefetch=1,
        grid=(B,),
        in_specs=[pl.BlockSpec(memory_space=pl.ANY), const(wbig), const(wab), const(pe), const(w2)],
        out_specs=pl.BlockSpec((1, ns, 2 * G * hd), lambda b, pt: (b, 0, 0)),
        scratch_shapes=[pltpu.VMEM((2, n_pages // 2, 2, G * hd, page), F32),
                        pltpu.VMEM((G, n_pages // 2, page, 2 * hd), F32),
                        pltpu.VMEM((G, ns, 4 * hd), F32),
                        pltpu.VMEM(wbig.shape, BF16),
                        pltpu.SemaphoreType.DMA((2,))],
    )
    return pl.pallas_call(
        functools.partial(_cmp_sample_kernel, n_pages=n_pages),
        grid_spec=grid_spec,
        out_shape=jax.ShapeDtypeStruct((B, ns, 2 * G * hd), BF16),
        compiler_params=_cparams("arbitrary"),
        name="compress_sample",
    )(page_table, cache_t, wbig, wab, pe, w2)


def _group_queries(pr_ref, g):
    hd = NSA_HEAD_DIM
    rows = [pr_ref[0, :, (g * NSA_HPG + h) * hd:(g * NSA_HPG + h + 1) * hd] for h in range(NSA_HPG)]
    return jnp.concatenate(rows, axis=0) * NSA_SCALE


def _nsa_sample_select_kernel(pr_ref, cmp_ref, oc_ref, idx_ref, *, t_pos, n_slc, n_pad):
    G, hpg, hd = NSA_KV_GROUPS, NSA_HPG, NSA_HEAD_DIM
    n_cmp = cmp_ref.shape[1]
    cmp = cmp_ref[0]
    ci = lax.broadcasted_iota(jnp.int32, (n_cmp, n_pad), 0) * CMP_STRIDE
    sj = lax.broadcasted_iota(jnp.int32, (n_cmp, n_pad), 1) * SLC_BLOCK
    ov = jnp.where((ci < sj + SLC_BLOCK) & (ci + CMP_LEN > sj), 1.0, 0.0).astype(BF16)
    blk = lax.broadcasted_iota(jnp.int32, (1, n_pad), 1)
    cur = t_pos // SLC_BLOCK
    forced = (blk == 0) | (blk == cur) | (blk == cur - 1)
    jr = lax.broadcasted_iota(jnp.int32, (n_pad, n_pad), 0)
    jc = lax.broadcasted_iota(jnp.int32, (n_pad, n_pad), 1)
    for g in range(G):
        qg = _group_queries(pr_ref, g).astype(BF16)
        sc = _dot_nt(qg, cmp[:, g * hd:(g + 1) * hd])
        e_pos = lax.broadcasted_iota(jnp.int32, (hpg, n_cmp), 1) * CMP_STRIDE + (CMP_LEN - 1)
        p_c = _masked_softmax(sc, e_pos <= t_pos)
        o_c = _dot(p_c.astype(BF16), cmp[:, (G + g) * hd:(G + g + 1) * hd])
        for h in range(hpg):
            col = (g * hpg + h) * hd
            oc_ref[0, :, col:col + hd] = o_c[h:h + 1]
        hi, mid, lo = _split3(jnp.sum(p_c, axis=0, keepdims=True))
        imp = _dot(hi, ov) + _dot(mid, ov) + _dot(lo, ov)
        score = jnp.where(forced, FORCED_SCORE, jnp.where(blk <= cur, imp, -1.0))
        score = jnp.where(blk < n_slc, score, -2.0)
        col_s = _row_to_col(score)
        beats = (col_s > score) | ((col_s == score) & (jr < jc))
        rank = jnp.sum(jnp.where(beats, 1.0, 0.0), axis=0, keepdims=True)
        rr = lax.broadcasted_iota(jnp.int32, (SLC_TOPK, n_pad), 0).astype(F32)
        bsel = jnp.where(jnp.broadcast_to(rank, (SLC_TOPK, n_pad)) == rr,
                         lax.broadcasted_iota(jnp.int32, (SLC_TOPK, n_pad), 1).astype(F32), 0.0)
        idx_ref[0, g * SLC_TOPK:(g + 1) * SLC_TOPK, :] = jnp.sum(bsel, axis=-1, keepdims=True).astype(jnp.int32)


def nsa_sample_select(proj, cmp_s, t_pos, n_slc):
    B = proj.shape[0]
    n_pad = -(-n_slc // 128) * 128
    G = NSA_KV_GROUPS
    nq = NSA_HEADS * NSA_HEAD_DIM
    return pl.pallas_call(
        functools.partial(_nsa_sample_select_kernel, t_pos=t_pos, n_slc=n_slc, n_pad=n_pad),
        grid=(B,),
        in_specs=[pl.BlockSpec((1, 1, proj.shape[2]), lambda b: (b, 0, 0)),
                  pl.BlockSpec((1,) + cmp_s.shape[1:], lambda b: (b, 0, 0))],
        out_specs=[pl.BlockSpec((1, 1, nq), lambda b: (b, 0, 0)),
                   pl.BlockSpec((1, G * SLC_TOPK, 1), lambda b: (b, 0, 0))],
        out_shape=[jax.ShapeDtypeStruct((B, 1, nq), F32),
                   jax.ShapeDtypeStruct((B, G * SLC_TOPK, 1), jnp.int32)],
        compiler_params=_cparams("parallel"),
        name="nsa_sample_select",
    )(proj, cmp_s)


def _nsa_sample_attend_kernel(pt_ref, idx_ref, pr_ref, oc_ref, kvn_ref, win_ref, cache_ref, o_ref,
                              kbuf_ref, vbuf_ref, sem, *, t_pos, past_len):
    b = pl.program_id(0)
    nb = pl.num_programs(0)
    G, hpg, hd = NSA_KV_GROUPS, NSA_HPG, NSA_HEAD_DIM
    n_sel = G * SLC_TOPK
    page = cache_ref.shape[3]
    bpp = page // SLC_BLOCK
    new_blk = past_len // SLC_BLOCK

    def blk_copies(bb, n):
        slot = bb % 2
        j = jnp.minimum(idx_ref[bb, n], new_blk - 1)
        pg = pt_ref[bb, j // bpp]
        rows = pl.ds(pl.multiple_of((n // SLC_TOPK) * hd, hd), hd)
        return (pltpu.make_async_copy(cache_ref.at[pg, 2, rows], kbuf_ref.at[slot, n], sem.at[slot]),
                pltpu.make_async_copy(cache_ref.at[pg, 3, rows], vbuf_ref.at[slot, n], sem.at[slot]))

    def start_all(bb):
        def body(n, c):
            for cp in blk_copies(bb, n):
                cp.start()
            return c
        lax.fori_loop(0, n_sel, body, 0)

    def wait_all(bb):
        def body(n, c):
            for cp in blk_copies(bb, n):
                cp.wait()
            return c
        lax.fori_loop(0, n_sel, body, 0)

    @pl.when(b == 0)
    def _():
        start_all(0)

    @pl.when(b + 1 < nb)
    def _():
        start_all(b + 1)

    wait_all(b)
    slot = b % 2

    nk = SLC_TOPK * page
    w_buf = win_ref.shape[3]
    nq = NSA_HEADS * hd
    gates = _sigmoid(pr_ref[0, :, nq:nq + GATE_LANES])
    kvn = kvn_ref[0]

    def new_row(kind, g):
        return kvn[:, (kind * G + g) * hd:(kind * G + g + 1) * hd].astype(BF16).astype(F32)

    def attend_with_new(qg, s, ok, vt, k_new, v_new):
        s_new = jnp.sum(qg.astype(F32) * k_new, axis=-1, keepdims=True)
        s = jnp.where(ok, s, NEG_INF)
        m = jnp.maximum(jnp.max(s, axis=-1, keepdims=True), s_new)
        e = jnp.where(ok, jnp.exp(s - m), 0.0)
        e_new = jnp.exp(s_new - m)
        den = jnp.maximum(jnp.sum(e, axis=-1, keepdims=True) + e_new, TINY)
        return (_dot_nt(e.astype(BF16), vt) + e_new * v_new) / den

    lane = lax.broadcasted_iota(jnp.int32, (1, nk), 1)
    for g in range(G):
        qg = _group_queries(pr_ref, g).astype(BF16)
        kt = jnp.concatenate([kbuf_ref[slot, g * SLC_TOPK + r] for r in range(SLC_TOPK)], axis=1).astype(BF16)
        vt = jnp.concatenate([vbuf_ref[slot, g * SLC_TOPK + r] for r in range(SLC_TOPK)], axis=1).astype(BF16)
        vis = jnp.zeros((1, nk), jnp.int32)
        for r in range(SLC_TOPK):
            j = idx_ref[b, g * SLC_TOPK + r]
            half = jnp.where(j < new_blk, j % bpp, -1)
            vis = jnp.where(lane // page == r, jnp.where((lane % page) // SLC_BLOCK == half, 1, 0), vis)
        ok = jnp.broadcast_to(vis > 0, (hpg, nk))
        o_s = attend_with_new(qg, _dot(qg, kt), ok, vt, new_row(2, g), new_row(3, g))
        kwt = win_ref[0, 0, g * hd:(g + 1) * hd, :].astype(BF16)
        vwt = win_ref[0, 1, g * hd:(g + 1) * hd, :].astype(BF16)
        wpos = past_len - w_buf + lax.broadcasted_iota(jnp.int32, (hpg, w_buf), 1)
        okw = (wpos <= t_pos) & (wpos > t_pos - WINDOW) & (wpos >= 0)
        o_w = attend_with_new(qg, _dot(qg, kwt), okw, vwt, new_row(4, g), new_row(5, g))
        for h in range(hpg):
            col = (g * hpg + h) * hd
            gc = g * GATE_ROWS + h
            o_h = (gates[:, gc:gc + 1] * oc_ref[0, :, col:col + hd]
                   + gates[:, gc + hpg:gc + hpg + 1] * o_s[h:h + 1]
                   + gates[:, gc + 2 * hpg:gc + 2 * hpg + 1] * o_w[h:h + 1])
            o_ref[0, :, col:col + hd] = o_h


def nsa_sample_attend(proj, o_c, kv_new, win_t, cache_t, page_table, idx, t_pos, past_len):
    B = proj.shape[0]
    G, hd = NSA_KV_GROUPS, NSA_HEAD_DIM
    nq = NSA_HEADS * hd
    page = cache_t.shape[3]
    row = lambda a: pl.BlockSpec((1, 1, a.shape[2]), lambda b, pt, ix: (b, 0, 0))
    grid_spec = pltpu.PrefetchScalarGridSpec(
        num_scalar_prefetch=2,
        grid=(B,),
        in_specs=[row(proj), row(o_c), row(kv_new),
                  pl.BlockSpec((1,) + win_t.shape[1:], lambda b, pt, ix: (b, 0, 0, 0)),
                  pl.BlockSpec(memory_space=pl.ANY)],
        out_specs=pl.BlockSpec((1, 1, nq), lambda b, pt, ix: (b, 0, 0)),
        scratch_shapes=[pltpu.VMEM((2, G * SLC_TOPK, hd, page), F32),
                        pltpu.VMEM((2, G * SLC_TOPK, hd, page), F32),
                        pltpu.SemaphoreType.DMA((2,))],
    )
    return pl.pallas_call(
        functools.partial(_nsa_sample_attend_kernel, t_pos=t_pos, past_len=past_len),
        grid_spec=grid_spec,
        out_shape=jax.ShapeDtypeStruct((B, 1, nq), F32),
        compiler_params=_cparams("arbitrary"),
        name="nsa_sample_attend",
    )(page_table, idx, proj, o_c, kv_new, win_t, cache_t)


def kernel(x_prompt, x_sample, cache_nsa_kv, cache_win_kv, state_hgrn, page_table, norm_mix, norm_mlp, w_mlp_up, w_mlp_down, w_hgrn_in, hgrn_lb_logits, hgrn_onorm, w_hgrn_out, norm_kv, w_kv, cmp_pe_k, cmp_w1_k, cmp_w2_k, cmp_pe_v, cmp_w1_v, cmp_w2_v, w_nsa_q, w_nsa_out, norm_final):
    B, T, D = x_prompt.shape
    Bs, Ts, _ = x_sample.shape
    G, hd = NSA_KV_GROUPS, NSA_HEAD_DIM
    n_pool, page = cache_nsa_kv.shape[:2]
    past_len = page_table.shape[1] * page
    w_buf = cache_win_kv.shape[1]
    assert Ts == 1 and T % 1024 == 0 and T >= WINDOW + 256 and past_len % SLC_BLOCK == 0 and w_buf <= past_len

    wab, wbig, pe, w2, w2t = _cmp_weights(cmp_pe_k, cmp_w1_k, cmp_pe_v, cmp_w1_v, cmp_w2_k, cmp_w2_v)
    wq = [_permute_gate_cols(w_nsa_q[l]) for l in range(DEPTH - N_A)]

    tm = 1024
    x = x_prompt.reshape(B * T, D)
    states_p = []
    for l in range(DEPTH):
        if l == N_A:
            nsa_p, win_p, k_hm, vt_hm = kv_proj_prompt(x, norm_kv, w_kv, B, T)
            kc_p, vct_p = compress_prompt(nsa_p, B, T, wbig, wab, pe, w2, w2t)
        if l < N_A:
            proj = rms_proj(x, norm_mix[l], w_hgrn_in[l], tm, 1024).reshape(B, T, -1)
            o, s_new = hgrn_prompt(proj, hgrn_lb_logits, hgrn_onorm[l], l)
            states_p.append(s_new)
            x = proj_res(o.reshape(B * T, -1), w_hgrn_out[l], x, tm)
        else:
            q_hm, gates_t = q_proj_prompt(x, norm_mix[l], wq[l - N_A], B, T)
            o = nsa_prompt(q_hm, gates_t, kc_p, vct_p, k_hm, vt_hm)
            x = proj_res(o.reshape(B * T, -1), w_nsa_out[l - N_A], x, tm)
        x = mlp_res(x, norm_mlp[l], w_mlp_up[l], w_mlp_down[l], norm_final, tm, 512, l == DEPTH - 1)
    y_prompt = x.reshape(B, T, D)
    nsa_kv_prompt = nsa_p.reshape(B, T, 4, G, hd)
    win_kv_prompt = win_p.reshape(B, T, 2, G, hd)[:, -min(WINDOW, T):]

    t_pos = past_len
    n_slc = -(-(past_len + 1) // SLC_BLOCK)
    xs = x_sample.reshape(Bs, D)
    cache_t = cache_nsa_kv.transpose(0, 2, 3, 4, 1).reshape(n_pool, 4, G * hd, page)
    win_t = cache_win_kv.transpose(0, 2, 3, 4, 1).reshape(Bs, 2, G * hd, w_buf)
    states_s = []
    for l in range(DEPTH):
        if l == N_A:
            kv_s = rms_proj(xs, norm_kv, w_kv, Bs, 512)
            cmp_s = compress_sample(cache_t, page_table, wbig, wab, pe, w2)
        if l < N_A:
            proj = rms_proj(xs, norm_mix[l], w_hgrn_in[l], Bs, 512).reshape(Bs, 1, -1)
            o, s_new = hgrn_step(proj, state_hgrn[l], hgrn_lb_logits, hgrn_onorm[l], l)
            states_s.append(s_new)
        else:
            proj = rms_proj(xs, norm_mix[l], wq[l - N_A], Bs, 384).reshape(Bs, 1, -1)
            o_c, idx = nsa_sample_select(proj, cmp_s, t_pos, n_slc)
            o = nsa_sample_attend(proj, o_c, kv_s.reshape(Bs, 1, -1), win_t, cache_t, page_table,
                                  idx.reshape(Bs, G * SLC_TOPK), t_pos, past_len)
        w_o = w_hgrn_out[l] if l < N_A else w_nsa_out[l - N_A]
        xs = proj_res(o.reshape(Bs, -1), w_o, xs, Bs)
        xs = mlp_res(xs, norm_mlp[l], w_mlp_up[l], w_mlp_down[l], norm_final, Bs, 512, l == DEPTH - 1)
    y_sample = xs.reshape(Bs, 1, D)
    n_nsa = 4 * G * hd
    nsa_kv_sample = kv_s[:, :n_nsa].reshape(Bs, 1, 4, G, hd)
    win_new = kv_s[:, n_nsa:].reshape(Bs, 1, 2, G, hd).astype(cache_win_kv.dtype)
    win_kv_sample = jnp.concatenate([cache_win_kv, win_new], axis=1)[:, -w_buf:]

    return (y_prompt, y_sample, nsa_kv_prompt, nsa_kv_sample, win_kv_prompt, win_kv_sample,
            jnp.stack(states_p), jnp.stack(states_s))
```

```python
import functools

import jax
import jax.numpy as jnp
from jax import lax
from jax.experimental import pallas as pl
from jax.experimental.pallas import tpu as pltpu

F32 = jnp.float32
BF16 = jnp.bfloat16

D_MODEL = 1024
DEPTH = 4
N_A = DEPTH // 2
D_FF = 4 * D_MODEL
RMS_EPS = 1e-6
HG_HEADS = 8
HG_K = 128
HG_V = 128
NSA_HEADS = 16
NSA_HEAD_DIM = 64
NSA_KV_GROUPS = 4
NSA_HPG = NSA_HEADS // NSA_KV_GROUPS
NSA_SCALE = NSA_HEAD_DIM ** -0.5
CMP_LEN = 32
CMP_STRIDE = 16
SLC_BLOCK = 64
SLC_TOPK = 16
WINDOW = 512
FORCED_SCORE = 1e4
NEG_INF = -1e30
TINY = 1e-30

HG_CHUNK = 128
HG_MATRIX_LEVELS = 2
VMEM_LIMIT = 56 * 1024 * 1024


def _cparams(*sem):
    return pltpu.CompilerParams(dimension_semantics=sem, vmem_limit_bytes=VMEM_LIMIT)


def _rms(x, g):
    return x * lax.rsqrt(jnp.mean(x * x, axis=-1, keepdims=True) + RMS_EPS) * g


def _sigmoid(x):
    return 1.0 / (1.0 + jnp.exp(-x))


def _dot(a, b):
    return jnp.dot(a, b, preferred_element_type=F32)


def _dot_nt(a, b):
    return lax.dot_general(a, b, (((1,), (1,)), ((), ())), preferred_element_type=F32)


def _dot_tn(a, b):
    return lax.dot_general(a, b, (((0,), (0,)), ((), ())), preferred_element_type=F32)


def _split3(x):
    hi = x.astype(BF16)
    r1 = x - hi.astype(F32)
    mid = r1.astype(BF16)
    lo = (r1 - mid.astype(F32)).astype(BF16)
    return hi, mid, lo


def _masked_softmax(s, mask):
    s = jnp.where(mask, s, NEG_INF)
    e = jnp.where(mask, jnp.exp(s - jnp.max(s, axis=-1, keepdims=True)), 0.0)
    return e / jnp.maximum(jnp.sum(e, axis=-1, keepdims=True), TINY)


def _rms_proj_kernel(x_ref, g_ref, w_ref, o_ref, y_ref):
    @pl.when(pl.program_id(1) == 0)
    def _():
        y_ref[...] = _rms(x_ref[...], g_ref[...]).astype(BF16)

    o_ref[...] = _dot(y_ref[...], w_ref[...].astype(BF16)).astype(o_ref.dtype)


def rms_proj(x, g, w, tm, tn, out_dtype=F32):
    M, D = x.shape
    N = w.shape[1]
    return pl.pallas_call(
        _rms_proj_kernel,
        grid=(M // tm, N // tn),
        in_specs=[pl.BlockSpec((tm, D), lambda i, j: (i, 0)),
                  pl.BlockSpec((1, D), lambda i, j: (0, 0)),
                  pl.BlockSpec((D, tn), lambda i, j: (0, j))],
        out_specs=pl.BlockSpec((tm, tn), lambda i, j: (i, j)),
        out_shape=jax.ShapeDtypeStruct((M, N), out_dtype),
        scratch_shapes=[pltpu.VMEM((tm, D), BF16)],
        compiler_params=_cparams("parallel", "arbitrary"),
        name="rms_proj",
    )(x, g.reshape(1, D), w)


def _proj_res_kernel(a_ref, w_ref, r_ref, o_ref, wb_ref):
    @pl.when(pl.program_id(0) == 0)
    def _():
        wb_ref[...] = w_ref[...].astype(BF16)

    o_ref[...] = r_ref[...] + _dot(a_ref[...].astype(BF16), wb_ref[...])


def proj_res(a, w, res, tm):
    M, K = a.shape
    N = w.shape[1]
    return pl.pallas_call(
        _proj_res_kernel,
        grid=(M // tm,),
        in_specs=[pl.BlockSpec((tm, K), lambda i: (i, 0)),
                  pl.BlockSpec((K, N), lambda i: (0, 0)),
                  pl.BlockSpec((tm, N), lambda i: (i, 0))],
        out_specs=pl.BlockSpec((tm, N), lambda i: (i, 0)),
        out_shape=jax.ShapeDtypeStruct((M, N), F32),
        scratch_shapes=[pltpu.VMEM((K, N), BF16)],
        compiler_params=_cparams("arbitrary"),
        name="proj_res",
    )(a, w, res)


def _mlp_kernel(x_ref, g_ref, wu_ref, wd_ref, gf_ref, o_ref, y_ref, acc_ref, *, final_norm):
    f = pl.program_id(1)

    @pl.when(f == 0)
    def _():
        y_ref[...] = _rms(x_ref[...], g_ref[...]).astype(BF16)
        acc_ref[...] = jnp.zeros_like(acc_ref)

    h = jnp.maximum(_dot(y_ref[...], wu_ref[...].astype(BF16)), 0.0)
    acc_ref[...] += _dot((h * h).astype(BF16), wd_ref[...].astype(BF16))

    @pl.when(f == pl.num_programs(1) - 1)
    def _():
        out = x_ref[...] + acc_ref[...]
        if final_norm:
            out = _rms(out, gf_ref[...])
        o_ref[...] = out


def mlp_res(x, g, w_up, w_down, g_final, tm, tf, final_norm):
    M, D = x.shape
    Fdim = w_up.shape[1]
    return pl.pallas_call(
        functools.partial(_mlp_kernel, final_norm=final_norm),
        grid=(M // tm, Fdim // tf),
        in_specs=[pl.BlockSpec((tm, D), lambda i, f: (i, 0)),
                  pl.BlockSpec((1, D), lambda i, f: (0, 0)),
                  pl.BlockSpec((D, tf), lambda i, f: (0, f)),
                  pl.BlockSpec((tf, D), lambda i, f: (f, 0)),
                  pl.BlockSpec((1, D), lambda i, f: (0, 0))],
        out_specs=pl.BlockSpec((tm, D), lambda i, f: (i, 0)),
        out_shape=jax.ShapeDtypeStruct((M, D), F32),
        scratch_shapes=[pltpu.VMEM((tm, D), BF16), pltpu.VMEM((tm, D), F32)],
        compiler_params=_cparams("parallel", "arbitrary"),
        name="mlp_res",
    )(x, g.reshape(1, D), w_up, w_down, g_final.reshape(1, D))


def _hgrn_lower_bound(lg, layer):
    m = jnp.max(lg, axis=0, keepdims=True)
    e = jnp.exp(lg - m)
    p = e / jnp.sum(e, axis=0, keepdims=True)
    lb = jnp.sum(p[1:layer + 1], axis=0, keepdims=True)
    return jnp.log(lb), jnp.log(1.0 - lb)


def _hgrn_logf(z, lg, layer):
    ls = jnp.minimum(z, 0.0) - jnp.log(1.0 + jnp.exp(-jnp.abs(z)))
    if layer == 0:
        return ls
    log_lb, log1m = _hgrn_lower_bound(lg, layer)
    b2 = log1m + ls
    return jnp.maximum(log_lb, b2) + jnp.log(1.0 + jnp.exp(-jnp.abs(log_lb - b2)))


def _hgrn_sum_matrices(C, n_lev):
    r = lax.broadcasted_iota(jnp.int32, (C, C), 0)
    u = lax.broadcasted_iota(jnp.int32, (C, C), 1)
    mats = [r >= u]
    for lev in range(n_lev):
        h = 1 << lev
        off = r & (2 * h - 1)
        mid = r - off + h
        mats.append(((off >= h) & (u >= mid) & (u <= r)) | ((off < h) & (u > r) & (u < mid)))
    return jnp.concatenate([jnp.where(m, 1.0, 0.0).astype(BF16) for m in mats], axis=0)


def _hgrn_kernel(zq_ref, zf_ref, zi_ref, zg_ref, lg_ref, on_ref, _states_in, o_ref, s_ref, st_ref, w_ref, b_ref,
                 *, layer, tc, nh):
    t = pl.program_id(2)
    C = HG_CHUNK
    n_lev = C.bit_length() - 1
    heads = range(nh)

    @pl.when(t == 0)
    def _():
        st_ref[...] = jnp.zeros_like(st_ref)
        w_ref[...] = _hgrn_sum_matrices(C, HG_MATRIX_LEVELS)

    r_i = lax.broadcasted_iota(jnp.int32, (C, C), 0)
    c_i = lax.broadcasted_iota(jnp.int32, (C, C), 1)
    row = lax.broadcasted_iota(jnp.int32, (C, HG_K), 0)

    def chunk(ci, sts):
        rows = pl.ds(pl.multiple_of(ci * C, C), C)
        kl = [slice(h * HG_K, (h + 1) * HG_K) for h in heads]
        vl = [slice(h * HG_V, (h + 1) * HG_V) for h in heads]
        q = [zq_ref[0, rows, kl[h]] for h in heads]
        vb = [zi_ref[0, rows, vl[h]].astype(BF16) for h in heads]
        logf = [_hgrn_logf(zf_ref[0, rows, kl[h]], lg_ref[:, kl[h]], layer) for h in heads]
        k = [1.0 - jnp.exp(logf[h]) for h in heads]
        parts = [_split3(logf[h]) for h in heads]
        w_sum, w_lev = w_ref[0:C, :], w_ref[C:, :]
        b = [_dot(w_sum, parts[h][0]) + _dot(w_sum, parts[h][1]) + _dot(w_sum, parts[h][2]) for h in heads]
        for h in heads:
            b_ref[h] = b[h]
        e_low = [jnp.exp(_dot(w_lev, parts[h][0]) + _dot(w_lev, parts[h][1])) for h in heads]
        a = [jnp.where(r_i == c_i, _dot_nt(q[h].astype(BF16), k[h].astype(BF16)), 0.0) for h in heads]
        for lev in range(n_lev):
            half = 1 << lev
            upper = (row & (2 * half - 1)) >= half
            same = (r_i >> (lev + 1)) == (c_i >> (lev + 1))
            if lev < HG_MATRIX_LEVELS:
                e = [e_low[h][lev * C:(lev + 1) * C] for h in heads]
            else:
                nblk = C // (2 * half)
                e = []
                for h in heads:
                    bm = b_ref[h, pl.ds(half - 1, nblk, stride=2 * half), :] if nblk > 1 else b_ref[h, half - 1:half, :]
                    bm = jnp.broadcast_to(bm[:, None, :], (nblk, 2 * half, HG_K)).reshape(C, HG_K)
                    e.append(jnp.exp(jnp.where(upper, b[h] - bm, bm - b[h])))
            qt = [jnp.where(upper, q[h] * e[h], 0.0).astype(BF16) for h in heads]
            kt = [jnp.where(upper, 0.0, k[h] * e[h]).astype(BF16) for h in heads]
            al = [_dot_nt(qt[h], kt[h]) for h in heads]
            a = [a[h] + (jnp.where(same, al[h], 0.0) if 2 * half < C else al[h]) for h in heads]
        qd = [(q[h] * jnp.exp(b[h])).astype(BF16) for h in heads]
        o = [_dot(a[h].astype(BF16), vb[h]) + _dot_nt(qd[h], sts[h].astype(BF16)) for h in heads]
        b_end = [b[h][C - 1:C] for h in heads]
        kd = [(k[h] * jnp.exp(b_end[h] - b[h])).astype(BF16) for h in heads]
        new = tuple(sts[h] * jnp.exp(b_end[h]) + _dot_tn(vb[h], kd[h]) for h in heads)
        for h in heads:
            oh = o[h] * lax.rsqrt(jnp.mean(o[h] * o[h], axis=-1, keepdims=True) + RMS_EPS) * on_ref[:, vl[h]]
            zg = zg_ref[0, rows, vl[h]]
            o_ref[0, rows, vl[h]] = (oh * (zg * _sigmoid(zg))).astype(o_ref.dtype)
        return new

    sts = lax.fori_loop(0, tc // C, chunk, tuple(st_ref[h] for h in heads))
    for h in heads:
        st_ref[h] = sts[h]

    @pl.when(t == pl.num_programs(2) - 1)
    def _():
        for h in heads:
            s_ref[0, 0, h] = st_ref[h].T


def hgrn_prompt(proj, lb_logits, onorm, layer, states, tc=512, nh=4):
    B, T, _ = proj.shape
    H = HG_HEADS
    hp = H // nh
    n_mats = 1 + HG_MATRIX_LEVELS
    alias_spec, alias_arg, aliases = [pl.BlockSpec(memory_space=pl.ANY)], [states], {6: 1}
    return pl.pallas_call(
        functools.partial(_hgrn_kernel, layer=layer, tc=tc, nh=nh),
        grid=(B, hp, T // tc),
        in_specs=[pl.BlockSpec((1, tc, nh * HG_K), lambda b, h, t: (b, t, h)),
                  pl.BlockSpec((1, tc, nh * HG_K), lambda b, h, t: (b, t, hp + h)),
                  pl.BlockSpec((1, tc, nh * HG_V), lambda b, h, t: (b, t, 2 * hp + h)),
                  pl.BlockSpec((1, tc, nh * HG_V), lambda b, h, t: (b, t, 3 * hp + h)),
                  pl.BlockSpec((N_A, nh * HG_K), lambda b, h, t: (0, h)),
                  pl.BlockSpec((1, nh * HG_V), lambda b, h, t: (0, h))] + alias_spec,
        out_specs=[pl.BlockSpec((1, tc, nh * HG_V), lambda b, h, t: (b, t, h)),
                   pl.BlockSpec((1, 1, nh, HG_K, HG_V), lambda b, h, t: (layer, b, h, 0, 0))],
        out_shape=[jax.ShapeDtypeStruct((B, T, H * HG_V), BF16),
                   jax.ShapeDtypeStruct((N_A, B, H, HG_K, HG_V), F32)],
        input_output_aliases=aliases,
        scratch_shapes=[pltpu.VMEM((nh, HG_V, HG_K), F32),
                        pltpu.VMEM((n_mats * HG_CHUNK, HG_CHUNK), BF16),
                        pltpu.VMEM((nh, HG_CHUNK, HG_K), F32)],
        compiler_params=_cparams("parallel", "parallel", "arbitrary"),
        name="hgrn_prompt",
    )(proj, proj, proj, proj, lb_logits, onorm.reshape(1, H * HG_V), *alias_arg)


QK_LANES = 2 * NSA_HEAD_DIM
VT_ROWS = NSA_HEAD_DIM + 16


def _kv_proj_kernel(x_ref, g_ref, w_ref, nsa_ref, win_ref, k_ref, vt_ref, wb_ref, *, tpb):
    @pl.when(pl.program_id(0) == 0)
    def _():
        wb_ref[...] = w_ref[...].astype(BF16)

    G, hd = NSA_KV_GROUPS, NSA_HEAD_DIM
    tm = x_ref.shape[0]
    y = _rms(x_ref[...], g_ref[...]).astype(BF16)
    kv = _dot(y, wb_ref[...])
    n_nsa = nsa_ref.shape[1]
    cols_t = [kv[:, c:c + 2 * hd].T for c in range(0, kv.shape[1], 2 * hd)]
    for n, t in enumerate(cols_t):
        c = n * 2 * hd
        if c < n_nsa:
            nsa_ref[0, c:c + 2 * hd, :] = t
        else:
            win_ref[0, c - n_nsa:c - n_nsa + 2 * hd, :] = t
    lane = lax.broadcasted_iota(jnp.int32, (tm, QK_LANES), 1)
    blk = ((pl.program_id(0) % tpb) * tm + lax.broadcasted_iota(jnp.int32, (tm, QK_LANES), 0)) // SLC_BLOCK
    tails = (jnp.where(lane - hd == blk, 1.0, 0.0), jnp.zeros((tm, QK_LANES), F32))
    for n, kind in enumerate((2, 4)):
        for gp in range(G // 2):
            col = (kind * G + 2 * gp) * hd
            pair = kv[:, col:col + 2 * hd]
            for gl, src in enumerate((pair, pltpu.roll(pair, hd, axis=1))):
                k_ref[0, n * G + 2 * gp + gl] = jnp.where(lane < hd, src, tails[n]).astype(BF16)
    ones_row = jnp.where(lax.broadcasted_iota(jnp.int32, (VT_ROWS - hd, tm), 0) == 0, 1.0, 0.0).astype(BF16)
    for n, kind in enumerate((3, 5)):
        for gp in range(G // 2):
            t = cols_t[(kind * G + 2 * gp) * hd // (2 * hd)]
            for gl in range(2):
                vt_ref[0, n * G + 2 * gp + gl, 0:hd, :] = t[gl * hd:(gl + 1) * hd].astype(BF16)
                vt_ref[0, n * G + 2 * gp + gl, hd:, :] = ones_row


def kv_proj_prompt(x, g, w_kv, B, T, tm=512):
    M, D = x.shape
    N = w_kv.shape[1]
    G, hd = NSA_KV_GROUPS, NSA_HEAD_DIM
    n_nsa = 4 * G * hd
    tpb = T // tm
    assert T // SLC_BLOCK <= QK_LANES - hd
    return pl.pallas_call(
        functools.partial(_kv_proj_kernel, tpb=tpb),
        grid=(M // tm,),
        in_specs=[pl.BlockSpec((tm, D), lambda i: (i, 0)),
                  pl.BlockSpec((1, D), lambda i: (0, 0)),
                  pl.BlockSpec((D, N), lambda i: (0, 0))],
        out_specs=[pl.BlockSpec((1, n_nsa, tm), lambda i: (i // tpb, 0, i % tpb)),
                   pl.BlockSpec((1, N - n_nsa, tm), lambda i: (i // tpb, 0, i % tpb)),
                   pl.BlockSpec((1, 2 * G, tm, QK_LANES), lambda i: (i // tpb, 0, i % tpb, 0)),
                   pl.BlockSpec((1, 2 * G, VT_ROWS, tm), lambda i: (i // tpb, 0, 0, i % tpb))],
        out_shape=[jax.ShapeDtypeStruct((B, n_nsa, T), F32),
                   jax.ShapeDtypeStruct((B, N - n_nsa, T), F32),
                   jax.ShapeDtypeStruct((B, 2 * G, T, QK_LANES), BF16),
                   jax.ShapeDtypeStruct((B, 2 * G, VT_ROWS, T), BF16)],
        scratch_shapes=[pltpu.VMEM((D, N), BF16)],
        compiler_params=_cparams("arbitrary"),
        name="kv_proj",
    )(x, g.reshape(1, D), w_kv)


GATE_ROWS = 16
GATE_LANES = 128


def _q_proj_kernel(x_ref, g_ref, w_ref, q_ref, gt_ref, wb_ref):
    @pl.when(pl.program_id(0) == 0)
    def _():
        wb_ref[...] = w_ref[...].astype(BF16)

    y = _rms(x_ref[...], g_ref[...]).astype(BF16)
    pr = _dot(y, wb_ref[...])
    hd = NSA_HEAD_DIM
    nq = NSA_HEADS * hd
    low = lax.broadcasted_iota(jnp.int32, (pr.shape[0], QK_LANES), 1) < hd
    for hp in range(NSA_HEADS // 2):
        pair = pr[:, hp * 2 * hd:(hp + 1) * 2 * hd] * NSA_SCALE
        for hl, src in enumerate((pair, pltpu.roll(pair, hd, axis=1))):
            q_ref[0, 2 * hp + hl] = jnp.where(low, src, 0.0).astype(BF16)
    gates_t = _sigmoid(pr[:, nq:]).T
    for gi in range(NSA_KV_GROUPS):
        gt_ref[0, gi] = gates_t[gi * GATE_ROWS:(gi + 1) * GATE_ROWS]


def _permute_gate_cols(w_q):
    nq = NSA_HEADS * NSA_HEAD_DIM
    d = w_q.shape[0]
    wg = w_q[:, nq:].reshape(d, 3, NSA_KV_GROUPS, NSA_HPG).transpose(0, 2, 1, 3).reshape(d, NSA_KV_GROUPS, 3 * NSA_HPG)
    wg = jnp.pad(wg, ((0, 0), (0, 0), (0, GATE_ROWS - 3 * NSA_HPG))).reshape(d, NSA_KV_GROUPS * GATE_ROWS)
    wg = jnp.pad(wg, ((0, 0), (0, GATE_LANES - NSA_KV_GROUPS * GATE_ROWS)))
    return jnp.concatenate([w_q[:, :nq], wg], axis=1)


def q_proj_prompt(x, g, w_qp, B, T, tm=512):
    M, D = x.shape
    N = w_qp.shape[1]
    tpb = T // tm
    return pl.pallas_call(
        _q_proj_kernel,
        grid=(M // tm,),
        in_specs=[pl.BlockSpec((tm, D), lambda i: (i, 0)),
                  pl.BlockSpec((1, D), lambda i: (0, 0)),
                  pl.BlockSpec((D, N), lambda i: (0, 0))],
        out_specs=[pl.BlockSpec((1, NSA_HEADS, tm, QK_LANES), lambda i: (i // tpb, 0, i % tpb, 0)),
                   pl.BlockSpec((1, NSA_KV_GROUPS, GATE_ROWS, tm), lambda i: (i // tpb, 0, 0, i % tpb))],
        out_shape=[jax.ShapeDtypeStruct((B, NSA_HEADS, T, QK_LANES), BF16),
                   jax.ShapeDtypeStruct((B, NSA_KV_GROUPS, GATE_ROWS, T), F32)],
        scratch_shapes=[pltpu.VMEM((D, N), BF16)],
        compiler_params=_cparams("arbitrary"),
        name="q_proj",
    )(x, g.reshape(1, D), w_qp)


def _cmp_weights(pe_k, w1_k, pe_v, w1_v, w2_k, w2_v):
    half = CMP_STRIDE * NSA_HEAD_DIM

    def ab(w1):
        return jnp.concatenate([w1[:half], w1[half:]], axis=1)

    def big(w1):
        w = w1.reshape(2, CMP_STRIDE, NSA_HEAD_DIM, -1)
        b = jnp.einsum("alds,gh->lgdhas", w, jnp.eye(2, dtype=w1.dtype))
        return b.reshape(CMP_STRIDE * 2 * NSA_HEAD_DIM, 2 * 2 * w.shape[-1])

    wab = jnp.stack([ab(w1_k), ab(w1_v)])
    wbig = jnp.stack([big(w1_k), big(w1_v)])
    pe = jnp.stack([pe_k.reshape(2, half), pe_v.reshape(2, half)])
    w2 = jnp.stack([w2_k, w2_v])
    w2t = jnp.stack([w2_k.T, w2_v.T])
    return wab, wbig, pe, w2, w2t


def _cmp_taps(x_ref, ns):
    return jnp.concatenate([x_ref[pl.ds(l, ns, stride=CMP_STRIDE), :].astype(BF16) for l in range(CMP_STRIDE)], axis=1)


def _cmp_hidden(ab, pe, wab):
    hd = NSA_HEAD_DIM
    n = ab.shape[0]
    nxt = pltpu.roll(ab, n - 1, axis=0)
    pt = _dot(pe.astype(BF16), wab)
    hid = ab[:, :hd] + nxt[:, hd:] + pt[0:1, :hd] + pt[1:2, hd:]
    return hid * _sigmoid(hid)


def _cmp_prompt_kernel(xk_ref, xv_ref, wbig_ref, wab_ref, pe_ref, w2_ref, w2t_ref, kc_ref, vct_ref, buf_ref):
    hd = NSA_HEAD_DIM
    ns = xk_ref.shape[2] // CMP_STRIDE
    for c, x_ref in enumerate((xk_ref, xv_ref)):
        wab = wab_ref[c].astype(BF16)
        buf_ref[...] = x_ref[0].T
        ab2 = _dot(_cmp_taps(buf_ref, ns), wbig_ref[c].astype(BF16))
        for gl in range(2):
            act = _cmp_hidden(ab2[:, gl * 2 * hd:(gl + 1) * 2 * hd], pe_ref[c], wab).astype(BF16)
            if c == 0:
                kc_ref[0, gl] = _dot(act, w2_ref[c].astype(BF16)).astype(kc_ref.dtype)
            else:
                vct_ref[0, gl] = _dot_nt(w2t_ref[c].astype(BF16), act).astype(vct_ref.dtype)


def compress_prompt(nsa_t, wbig, wab, pe, w2, w2t):
    B, _, T = nsa_t.shape
    G, hd = NSA_KV_GROUPS, NSA_HEAD_DIM
    ns = T // CMP_STRIDE
    w2 = jnp.pad(w2, ((0, 0), (0, 0), (0, QK_LANES - hd)))
    const = lambda a: pl.BlockSpec(a.shape, lambda b, gp: (0,) * a.ndim)
    return pl.pallas_call(
        _cmp_prompt_kernel,
        grid=(B, G // 2),
        in_specs=[pl.BlockSpec((1, 2 * hd, T), lambda b, gp: (b, gp, 0)),
                  pl.BlockSpec((1, 2 * hd, T), lambda b, gp: (b, G // 2 + gp, 0)),
                  const(wbig), const(wab), const(pe), const(w2), const(w2t)],
        out_specs=[pl.BlockSpec((1, 2, ns, QK_LANES), lambda b, gp: (b, gp, 0, 0)),
                   pl.BlockSpec((1, 2, hd, ns), lambda b, gp: (b, gp, 0, 0))],
        out_shape=[jax.ShapeDtypeStruct((B, G, ns, QK_LANES), BF16),
                   jax.ShapeDtypeStruct((B, G, hd, ns), BF16)],
        scratch_shapes=[pltpu.VMEM((T, 2 * hd), F32)],
        compiler_params=_cparams("parallel", "parallel"),
        name="compress_prompt",
    )(nsa_t, nsa_t, wbig, wab, pe, w2, w2t)


def _nsa_prompt_kernel(q_ref, gt_ref, kc_ref, vct_ref, ks_ref, vst_ref, kw_ref, vwt_ref, o_ref, *, tq):
    i = pl.program_id(2)
    s0 = i * tq
    hpg, hd = NSA_HPG, NSA_HEAD_DIM
    T = ks_ref.shape[2]
    n_cmp = kc_ref.shape[2]
    n_slc = T // SLC_BLOCK
    R = hpg * tq
    Q = q_ref[0].reshape(R, QK_LANES)
    tpos = s0 + lax.broadcasted_iota(jnp.int32, (1, R), 1) % tq

    sc = _dot_nt(kc_ref[0, 0], Q)
    ok_c = lax.broadcasted_iota(jnp.int32, (n_cmp, R), 0) * CMP_STRIDE + (CMP_LEN - 1) <= tpos
    sc = jnp.where(ok_c, sc, NEG_INF)
    e_c = jnp.where(ok_c, jnp.exp(sc - jnp.max(sc, axis=0, keepdims=True)), 0.0)
    p_c = e_c / jnp.maximum(jnp.sum(e_c, axis=0, keepdims=True), TINY)
    o_c = _dot(vct_ref[0, 0], p_c.astype(BF16))

    psum = p_c[:, 0:tq]
    for h in range(1, hpg):
        psum = psum + p_c[:, h * tq:(h + 1) * tq]
    sj = lax.broadcasted_iota(jnp.int32, (n_slc, n_cmp), 0) * SLC_BLOCK
    ci = lax.broadcasted_iota(jnp.int32, (n_slc, n_cmp), 1) * CMP_STRIDE
    ov = jnp.where((ci < sj + SLC_BLOCK) & (ci + CMP_LEN > sj), 1.0, 0.0).astype(BF16)
    hi, mid, lo = _split3(psum)
    imp = _dot(ov, hi) + _dot(ov, mid) + _dot(ov, lo)
    blk = lax.broadcasted_iota(jnp.int32, (n_slc, tq), 0)
    qpos = s0 + lax.broadcasted_iota(jnp.int32, (n_slc, tq), 1)
    cur = qpos // SLC_BLOCK
    forced = (blk == 0) | (blk == cur) | (blk == cur - 1)
    score = jnp.where(forced, FORCED_SCORE, jnp.where(blk <= cur, imp, -1.0))
    rank = jnp.zeros((n_slc, tq), F32)
    for j in range(n_slc):
        cj = score[j:j + 1, :]
        rank = rank + jnp.where((cj > score) | ((cj == score) & (blk > j)), 1.0, 0.0)
    bias = jnp.where(rank < SLC_TOPK, 0.0, NEG_INF)
    bias = jnp.concatenate([jnp.zeros((hd, tq), F32), bias, jnp.zeros((QK_LANES - hd - n_slc, tq), F32)], axis=0)
    bias_t = bias.T.astype(BF16)
    q_sel = Q + jnp.concatenate([bias_t] * hpg, axis=0)

    heads = range(hpg)
    q_heads = [q_sel[h * tq:(h + 1) * tq] for h in heads]
    tpos_h = tpos[:, :tq]

    def chunk(c, carry, diagonal):
        k0 = pl.multiple_of(c * tq, tq)
        ks = ks_ref[0, 0, pl.ds(k0, tq), :]
        vst = vst_ref[0, 0, :, pl.ds(k0, tq)]
        s_heads = [_dot_nt(ks, q_heads[h]) for h in heads]
        new = []
        for h in heads:
            m, acc = carry[h]
            s = s_heads[h]
            if diagonal:
                s = jnp.where(k0 + lax.broadcasted_iota(jnp.int32, (tq, tq), 0) <= tpos_h, s, NEG_INF)
            m_new = jnp.maximum(m, jnp.max(s, axis=0, keepdims=True))
            p = jnp.exp(s - m_new)
            new.append((m_new, jnp.exp(m - m_new) * acc + _dot(vst, p.astype(BF16))))
        return tuple(new)

    carry = tuple((jnp.full((1, tq), NEG_INF, F32), jnp.zeros((VT_ROWS, tq), F32)) for _ in heads)
    carry = lax.fori_loop(0, i, lambda c, cr: chunk(c, cr, False), carry)
    carry = chunk(i, carry, True)
    o_s = jnp.concatenate([acc[:hd] / jnp.maximum(acc[hd:hd + 1], TINY) for _, acc in carry], axis=1)

    ws = pl.multiple_of(jnp.maximum(s0 - WINDOW, 0), tq)
    parts = []
    for j in range(WINDOW // tq + 1):
        k0 = pl.multiple_of(ws + j * tq, tq)
        s = _dot_nt(kw_ref[0, 0, pl.ds(k0, tq), :], Q)
        wpos = k0 + lax.broadcasted_iota(jnp.int32, (tq, R), 0)
        ok = (wpos <= tpos) & (wpos > tpos - WINDOW) if j == 0 else wpos <= tpos
        parts.append((k0, jnp.where(ok, s, NEG_INF)))
    m_w = functools.reduce(jnp.maximum, [jnp.max(s, axis=0, keepdims=True) for _, s in parts])
    acc_w = sum(_dot(vwt_ref[0, 0, :, pl.ds(k0, tq)], jnp.exp(s - m_w).astype(BF16)) for k0, s in parts)
    o_w = acc_w[:hd] / jnp.maximum(acc_w[hd:hd + 1], TINY)

    gt = gt_ref[0, 0]
    outs = []
    for h in range(hpg):
        cols = slice(h * tq, (h + 1) * tq)
        outs.append(gt[h:h + 1] * o_c[:, cols] + gt[hpg + h:hpg + h + 1] * o_s[:, cols]
                    + gt[2 * hpg + h:2 * hpg + h + 1] * o_w[:, cols])
    for pair in range(hpg // 2):
        both = jnp.concatenate(outs[2 * pair:2 * pair + 2], axis=0)
        o_ref[0, :, pair * 2 * hd:(pair + 1) * 2 * hd] = both.T.astype(o_ref.dtype)


def nsa_prompt(q_hm, gates_t, kc, vct, k_hm, vt_hm, tq=256):
    B, _, T, _ = q_hm.shape
    G, hpg, hd = NSA_KV_GROUPS, NSA_HPG, NSA_HEAD_DIM
    n_cmp = kc.shape[2]
    assert WINDOW % tq == 0 and T >= WINDOW + tq
    keys = lambda off: pl.BlockSpec((1, 1, T, QK_LANES), lambda b, g, i: (b, off + g, 0, 0))
    vals = lambda off: pl.BlockSpec((1, 1, VT_ROWS, T), lambda b, g, i: (b, off + g, 0, 0))
    return pl.pallas_call(
        functools.partial(_nsa_prompt_kernel, tq=tq),
        grid=(B, G, T // tq),
        in_specs=[pl.BlockSpec((1, hpg, tq, QK_LANES), lambda b, g, i: (b, g, i, 0)),
                  pl.BlockSpec((1, 1, GATE_ROWS, tq), lambda b, g, i: (b, g, 0, i)),
                  pl.BlockSpec((1, 1, n_cmp, QK_LANES), lambda b, g, i: (b, g, 0, 0)),
                  pl.BlockSpec((1, 1, hd, n_cmp), lambda b, g, i: (b, g, 0, 0)),
                  keys(0), vals(0), keys(G), vals(G)],
        out_specs=pl.BlockSpec((1, tq, hpg * hd), lambda b, g, i: (b, i, g)),
        out_shape=jax.ShapeDtypeStruct((B, T, NSA_HEADS * hd), BF16),
        compiler_params=_cparams("parallel", "parallel", "arbitrary"),
        name="nsa_prompt",
    )(q_hm, gates_t, kc, vct, k_hm, vt_hm, k_hm, vt_hm)


def _row_to_col(row):
    n = row.shape[1]
    eye = lax.broadcasted_iota(jnp.int32, (n, n), 0) == lax.broadcasted_iota(jnp.int32, (n, n), 1)
    return jnp.sum(jnp.where(eye, jnp.broadcast_to(row, (n, n)), 0.0), axis=-1, keepdims=True)


def _hgrn_step_kernel(z_ref, s0_ref, lg_ref, on_ref, _states_in, o_ref, s_ref, *, layer):
    s0_ref, s_ref = s0_ref.at[0], s_ref.at[0]
    hk = HG_HEADS * HG_K
    hv = HG_HEADS * HG_V
    for h in range(HG_HEADS):
        kl = slice(h * HG_K, (h + 1) * HG_K)
        vl = slice(h * HG_V, (h + 1) * HG_V)
        q = z_ref[0, :, kl]
        logf = _hgrn_logf(z_ref[0, :, hk + h * HG_K:hk + (h + 1) * HG_K], lg_ref[:, kl], layer)
        f = jnp.exp(logf)
        v = z_ref[0, :, 2 * hk + h * HG_V:2 * hk + (h + 1) * HG_V]
        zg = z_ref[0, :, 2 * hk + hv + h * HG_V:2 * hk + hv + (h + 1) * HG_V]
        s = _row_to_col(f) * s0_ref[0, h] + _row_to_col(1.0 - f) * v
        s_ref[0, h] = s
        o = jnp.sum(_row_to_col(q) * s, axis=0, keepdims=True)
        o = o * lax.rsqrt(jnp.mean(o * o, axis=-1, keepdims=True) + RMS_EPS) * on_ref[:, vl]
        o_ref[0, :, vl] = o * (zg * _sigmoid(zg))


def hgrn_step(proj, s0_all, lb_logits, onorm, layer, states):
    B = proj.shape[0]
    H = HG_HEADS
    alias_spec, alias_arg, aliases = [pl.BlockSpec(memory_space=pl.ANY)], [states], {4: 1}
    state_block = pl.BlockSpec((1, 1, H, HG_K, HG_V), lambda b: (layer, b, 0, 0, 0))
    return pl.pallas_call(
        functools.partial(_hgrn_step_kernel, layer=layer),
        grid=(B,),
        in_specs=[pl.BlockSpec((1, 1, proj.shape[2]), lambda b: (b, 0, 0)),
                  state_block,
                  pl.BlockSpec((N_A, H * HG_K), lambda b: (0, 0)),
                  pl.BlockSpec((1, H * HG_V), lambda b: (0, 0))] + alias_spec,
        out_specs=[pl.BlockSpec((1, 1, H * HG_V), lambda b: (b, 0, 0)), state_block],
        out_shape=[jax.ShapeDtypeStruct((B, 1, H * HG_V), F32),
                   jax.ShapeDtypeStruct(s0_all.shape, F32)],
        input_output_aliases=aliases,
        compiler_params=_cparams("parallel"),
        name="hgrn_step",
    )(proj, s0_all, lb_logits, onorm.reshape(1, H * HG_V), *alias_arg)


def _cmp_sample_kernel(pt_ref, cache_ref, wbig_ref, wab_ref, pe_ref, w2_ref, o_ref, raw_ref, buf_ref, wb_ref,
                       sem, *, n_pages):
    b = pl.program_id(0)
    nb = pl.num_programs(0)
    G, hd = NSA_KV_GROUPS, NSA_HEAD_DIM
    hp = n_pages // 2
    spp = cache_ref.shape[3] // CMP_STRIDE
    ns = n_pages * spp

    def page_copy(bb, half, p):
        src = cache_ref.at[pt_ref[bb, half * hp + p], pl.ds(0, 2)]
        return pltpu.make_async_copy(src, raw_ref.at[half, p], sem.at[half])

    def start_half(bb, half):
        def body(p, c):
            page_copy(bb, half, p).start()
            return c
        lax.fori_loop(0, hp, body, 0)

    def wait_half(bb, half):
        def body(p, c):
            page_copy(bb, half, p).wait()
            return c
        lax.fori_loop(0, hp, body, 0)

    page = cache_ref.shape[3]
    r_i = lax.broadcasted_iota(jnp.int32, (page, page), 0)
    c_i = lax.broadcasted_iota(jnp.int32, (page, page), 1)
    perm = jnp.where(c_i == (r_i % spp) * CMP_STRIDE + r_i // spp, 1.0, 0.0).astype(BF16)

    def to_token_rows(half):
        def body(p, c):
            for kind in range(2):
                moved = _dot_nt(raw_ref[half, p, kind].astype(BF16), perm)
                for gp in range(G // 2):
                    buf_ref[kind * (G // 2) + gp, half * hp + p] = moved[gp * 2 * hd:(gp + 1) * 2 * hd].T
            return c
        lax.fori_loop(0, hp, body, 0, unroll=4)

    @pl.when(b == 0)
    def _():
        start_half(0, 0)
        start_half(0, 1)
        wb_ref[...] = wbig_ref[...].astype(BF16)

    for half in range(2):
        wait_half(b, half)
        to_token_rows(half)

        @pl.when(b + 1 < nb)
        def _():
            start_half(b + 1, half)

    for c in range(2):
        wab = wab_ref[c].astype(BF16)
        w2 = w2_ref[c].astype(BF16)
        for gp in range(G // 2):
            cb = c * (G // 2) + gp
            x = jnp.concatenate(
                [buf_ref[cb, :, l * spp:(l + 1) * spp, :].reshape(ns, 2 * hd).astype(BF16)
                 for l in range(CMP_STRIDE)], axis=1)
            ab2 = _dot(x, wb_ref[c])
            for gl in range(2):
                act = _cmp_hidden(ab2[:, gl * 2 * hd:(gl + 1) * 2 * hd], pe_ref[c], wab)
                col = (c * G + 2 * gp + gl) * hd
                o_ref[0, :, col:col + hd] = _dot(act.astype(BF16), w2).astype(o_ref.dtype)


def compress_sample(cache_t, page_table, wbig, wab, pe, w2):
    B, n_pages = page_table.shape
    page = cache_t.shape[3]
    G, hd = NSA_KV_GROUPS, NSA_HEAD_DIM
    ns = n_pages * page // CMP_STRIDE
    assert n_pages % 2 == 0 and page == 2 * hd
    const = lambda a: pl.BlockSpec(a.shape, lambda b, pt: (0,) * a.ndim)
    grid_spec = pltpu.PrefetchScalarGridSpec(
        num_scalar_prefetch=1,
        grid=(B,),
        in_specs=[pl.BlockSpec(memory_space=pl.ANY), const(wbig), const(wab), const(pe), const(w2)],
        out_specs=pl.BlockSpec((1, ns, 2 * G * hd), lambda b, pt: (b, 0, 0)),
        scratch_shapes=[pltpu.VMEM((2, n_pages // 2, 2, G * hd, page), F32),
                        pltpu.VMEM((G, n_pages, page, 2 * hd), F32),
                        pltpu.VMEM(wbig.shape, BF16),
                        pltpu.SemaphoreType.DMA((2,))],
    )
    return pl.pallas_call(
        functools.partial(_cmp_sample_kernel, n_pages=n_pages),
        grid_spec=grid_spec,
        out_shape=jax.ShapeDtypeStruct((B, ns, 2 * G * hd), BF16),
        compiler_params=_cparams("arbitrary"),
        name="compress_sample",
    )(page_table, cache_t, wbig, wab, pe, w2)


def _group_queries(pr_ref, g):
    hd = NSA_HEAD_DIM
    rows = [pr_ref[0, :, (g * NSA_HPG + h) * hd:(g * NSA_HPG + h + 1) * hd] for h in range(NSA_HPG)]
    return jnp.concatenate(rows, axis=0) * NSA_SCALE


def _nsa_sample_select_kernel(pr_ref, cmp_ref, oc_ref, idx_ref, *, t_pos, n_slc, n_pad):
    G, hpg, hd = NSA_KV_GROUPS, NSA_HPG, NSA_HEAD_DIM
    n_cmp = cmp_ref.shape[1]
    cmp = cmp_ref[0]
    ci = lax.broadcasted_iota(jnp.int32, (n_cmp, n_pad), 0) * CMP_STRIDE
    sj = lax.broadcasted_iota(jnp.int32, (n_cmp, n_pad), 1) * SLC_BLOCK
    ov = jnp.where((ci < sj + SLC_BLOCK) & (ci + CMP_LEN > sj), 1.0, 0.0).astype(BF16)
    blk = lax.broadcasted_iota(jnp.int32, (1, n_pad), 1)
    cur = t_pos // SLC_BLOCK
    forced = (blk == 0) | (blk == cur) | (blk == cur - 1)
    jr = lax.broadcasted_iota(jnp.int32, (n_pad, n_pad), 0)
    jc = lax.broadcasted_iota(jnp.int32, (n_pad, n_pad), 1)
    for g in range(G):
        qg = _group_queries(pr_ref, g).astype(BF16)
        sc = _dot_nt(qg, cmp[:, g * hd:(g + 1) * hd])
        e_pos = lax.broadcasted_iota(jnp.int32, (hpg, n_cmp), 1) * CMP_STRIDE + (CMP_LEN - 1)
        p_c = _masked_softmax(sc, e_pos <= t_pos)
        o_c = _dot(p_c.astype(BF16), cmp[:, (G + g) * hd:(G + g + 1) * hd])
        for h in range(hpg):
            col = (g * hpg + h) * hd
            oc_ref[0, :, col:col + hd] = o_c[h:h + 1]
        hi, mid, lo = _split3(jnp.sum(p_c, axis=0, keepdims=True))
        imp = _dot(hi, ov) + _dot(mid, ov) + _dot(lo, ov)
        score = jnp.where(forced, FORCED_SCORE, jnp.where(blk <= cur, imp, -1.0))
        score = jnp.where(blk < n_slc, score, -2.0)
        col_s = _row_to_col(score)
        beats = (col_s > score) | ((col_s == score) & (jr < jc))
        rank = jnp.sum(jnp.where(beats, 1.0, 0.0), axis=0, keepdims=True)
        rr = lax.broadcasted_iota(jnp.int32, (SLC_TOPK, n_pad), 0).astype(F32)
        bsel = jnp.where(jnp.broadcast_to(rank, (SLC_TOPK, n_pad)) == rr,
                         lax.broadcasted_iota(jnp.int32, (SLC_TOPK, n_pad), 1).astype(F32), 0.0)
        idx_ref[0, g * SLC_TOPK:(g + 1) * SLC_TOPK, :] = jnp.sum(bsel, axis=-1, keepdims=True).astype(jnp.int32)


def nsa_sample_select(proj, cmp_s, t_pos, n_slc):
    B = proj.shape[0]
    n_pad = -(-n_slc // 128) * 128
    G = NSA_KV_GROUPS
    nq = NSA_HEADS * NSA_HEAD_DIM
    return pl.pallas_call(
        functools.partial(_nsa_sample_select_kernel, t_pos=t_pos, n_slc=n_slc, n_pad=n_pad),
        grid=(B,),
        in_specs=[pl.BlockSpec((1, 1, proj.shape[2]), lambda b: (b, 0, 0)),
                  pl.BlockSpec((1,) + cmp_s.shape[1:], lambda b: (b, 0, 0))],
        out_specs=[pl.BlockSpec((1, 1, nq), lambda b: (b, 0, 0)),
                   pl.BlockSpec((1, G * SLC_TOPK, 1), lambda b: (b, 0, 0))],
        out_shape=[jax.ShapeDtypeStruct((B, 1, nq), F32),
                   jax.ShapeDtypeStruct((B, G * SLC_TOPK, 1), jnp.int32)],
        compiler_params=_cparams("parallel"),
        name="nsa_sample_select",
    )(proj, cmp_s)


def _nsa_sample_attend_kernel(pt_ref, idx_ref, pr_ref, oc_ref, kvn_ref, win_ref, cache_ref, o_ref,
                              kbuf_ref, vbuf_ref, sem, *, t_pos, past_len):
    b = pl.program_id(0)
    nb = pl.num_programs(0)
    G, hpg, hd = NSA_KV_GROUPS, NSA_HPG, NSA_HEAD_DIM
    n_sel = G * SLC_TOPK
    page = cache_ref.shape[3]
    bpp = page // SLC_BLOCK
    new_blk = past_len // SLC_BLOCK

    def blk_copies(bb, n):
        slot = bb % 2
        j = jnp.minimum(idx_ref[bb, n], new_blk - 1)
        pg = pt_ref[bb, j // bpp]
        rows = pl.ds(pl.multiple_of((n // SLC_TOPK) * hd, hd), hd)
        return (pltpu.make_async_copy(cache_ref.at[pg, 2, rows], kbuf_ref.at[slot, n], sem.at[slot]),
                pltpu.make_async_copy(cache_ref.at[pg, 3, rows], vbuf_ref.at[slot, n], sem.at[slot]))

    def start_all(bb):
        def body(n, c):
            for cp in blk_copies(bb, n):
                cp.start()
            return c
        lax.fori_loop(0, n_sel, body, 0)

    def wait_all(bb):
        def body(n, c):
            for cp in blk_copies(bb, n):
                cp.wait()
            return c
        lax.fori_loop(0, n_sel, body, 0)

    @pl.when(b == 0)
    def _():
        start_all(0)

    @pl.when(b + 1 < nb)
    def _():
        start_all(b + 1)

    wait_all(b)
    slot = b % 2

    nk = SLC_TOPK * page
    w_buf = win_ref.shape[3]
    nq = NSA_HEADS * hd
    gates = _sigmoid(pr_ref[0, :, nq:nq + GATE_LANES])
    kvn = kvn_ref[0]

    def new_row(kind, g):
        return kvn[:, (kind * G + g) * hd:(kind * G + g + 1) * hd].astype(BF16).astype(F32)

    def attend_with_new(qg, s, ok, vt, k_new, v_new):
        s_new = jnp.sum(qg.astype(F32) * k_new, axis=-1, keepdims=True)
        s = jnp.where(ok, s, NEG_INF)
        m = jnp.maximum(jnp.max(s, axis=-1, keepdims=True), s_new)
        e = jnp.where(ok, jnp.exp(s - m), 0.0)
        e_new = jnp.exp(s_new - m)
        den = jnp.maximum(jnp.sum(e, axis=-1, keepdims=True) + e_new, TINY)
        return (_dot_nt(e.astype(BF16), vt) + e_new * v_new) / den

    lane = lax.broadcasted_iota(jnp.int32, (1, nk), 1)
    for g in range(G):
        qg = _group_queries(pr_ref, g).astype(BF16)
        kt = jnp.concatenate([kbuf_ref[slot, g * SLC_TOPK + r] for r in range(SLC_TOPK)], axis=1).astype(BF16)
        vt = jnp.concatenate([vbuf_ref[slot, g * SLC_TOPK + r] for r in range(SLC_TOPK)], axis=1).astype(BF16)
        vis = jnp.zeros((1, nk), jnp.int32)
        for r in range(SLC_TOPK):
            j = idx_ref[b, g * SLC_TOPK + r]
            half = jnp.where(j < new_blk, j % bpp, -1)
            vis = jnp.where(lane // page == r, jnp.where((lane % page) // SLC_BLOCK == half, 1, 0), vis)
        ok = jnp.broadcast_to(vis > 0, (hpg, nk))
        o_s = attend_with_new(qg, _dot(qg, kt), ok, vt, new_row(2, g), new_row(3, g))
        kwt = win_ref[0, 0, g * hd:(g + 1) * hd, :].astype(BF16)
        vwt = win_ref[0, 1, g * hd:(g + 1) * hd, :].astype(BF16)
        wpos = past_len - w_buf + lax.broadcasted_iota(jnp.int32, (hpg, w_buf), 1)
        okw = (wpos <= t_pos) & (wpos > t_pos - WINDOW) & (wpos >= 0)
        o_w = attend_with_new(qg, _dot(qg, kwt), okw, vwt, new_row(4, g), new_row(5, g))
        for h in range(hpg):
            col = (g * hpg + h) * hd
            gc = g * GATE_ROWS + h
            o_h = (gates[:, gc:gc + 1] * oc_ref[0, :, col:col + hd]
                   + gates[:, gc + hpg:gc + hpg + 1] * o_s[h:h + 1]
                   + gates[:, gc + 2 * hpg:gc + 2 * hpg + 1] * o_w[h:h + 1])
            o_ref[0, :, col:col + hd] = o_h


def nsa_sample_attend(proj, o_c, kv_new, win_t, cache_t, page_table, idx, t_pos, past_len):
    B = proj.shape[0]
    G, hd = NSA_KV_GROUPS, NSA_HEAD_DIM
    nq = NSA_HEADS * hd
    page = cache_t.shape[3]
    row = lambda a: pl.BlockSpec((1, 1, a.shape[2]), lambda b, pt, ix: (b, 0, 0))
    grid_spec = pltpu.PrefetchScalarGridSpec(
        num_scalar_prefetch=2,
        grid=(B,),
        in_specs=[row(proj), row(o_c), row(kv_new),
                  pl.BlockSpec((1,) + win_t.shape[1:], lambda b, pt, ix: (b, 0, 0, 0)),
                  pl.BlockSpec(memory_space=pl.ANY)],
        out_specs=pl.BlockSpec((1, 1, nq), lambda b, pt, ix: (b, 0, 0)),
        scratch_shapes=[pltpu.VMEM((2, G * SLC_TOPK, hd, page), F32),
                        pltpu.VMEM((2, G * SLC_TOPK, hd, page), F32),
                        pltpu.SemaphoreType.DMA((2,))],
    )
    return pl.pallas_call(
        functools.partial(_nsa_sample_attend_kernel, t_pos=t_pos, past_len=past_len),
        grid_spec=grid_spec,
        out_shape=jax.ShapeDtypeStruct((B, 1, nq), F32),
        compiler_params=_cparams("arbitrary"),
        name="nsa_sample_attend",
    )(page_table, idx, proj, o_c, kv_new, win_t, cache_t)


def kernel(x_prompt, x_sample, cache_nsa_kv, cache_win_kv, state_hgrn, page_table, norm_mix, norm_mlp, w_mlp_up, w_mlp_down, w_hgrn_in, hgrn_lb_logits, hgrn_onorm, w_hgrn_out, norm_kv, w_kv, cmp_pe_k, cmp_w1_k, cmp_w2_k, cmp_pe_v, cmp_w1_v, cmp_w2_v, w_nsa_q, w_nsa_out, norm_final):
    B, T, D = x_prompt.shape
    Bs, Ts, _ = x_sample.shape
    G, hd = NSA_KV_GROUPS, NSA_HEAD_DIM
    n_pool, page = cache_nsa_kv.shape[:2]
    past_len = page_table.shape[1] * page
    w_buf = cache_win_kv.shape[1]
    assert Ts == 1 and T % 1024 == 0 and T >= WINDOW + 256 and past_len % SLC_BLOCK == 0 and w_buf <= past_len

    wab, wbig, pe, w2, w2t = _cmp_weights(cmp_pe_k, cmp_w1_k, cmp_pe_v, cmp_w1_v, cmp_w2_k, cmp_w2_v)
    wq = [_permute_gate_cols(w_nsa_q[l]) for l in range(DEPTH - N_A)]

    tm = 1024
    x = x_prompt.reshape(B * T, D)
    states_p = jnp.zeros((N_A, B, HG_HEADS, HG_K, HG_V), F32)
    for l in range(DEPTH):
        if l == N_A:
            nsa_p, win_p, k_hm, vt_hm = kv_proj_prompt(x, norm_kv, w_kv, B, T)
            kc_p, vct_p = compress_prompt(nsa_p, wbig, wab, pe, w2, w2t)
        if l < N_A:
            proj = rms_proj(x, norm_mix[l], w_hgrn_in[l], 2 * tm, 512).reshape(B, T, -1)
            o, states_p = hgrn_prompt(proj, hgrn_lb_logits, hgrn_onorm[l], l, states_p)
            x = proj_res(o.reshape(B * T, -1), w_hgrn_out[l], x, tm)
        else:
            q_hm, gates_t = q_proj_prompt(x, norm_mix[l], wq[l - N_A], B, T)
            o = nsa_prompt(q_hm, gates_t, kc_p, vct_p, k_hm, vt_hm)
            x = proj_res(o.reshape(B * T, -1), w_nsa_out[l - N_A], x, tm)
        x = mlp_res(x, norm_mlp[l], w_mlp_up[l], w_mlp_down[l], norm_final, tm, 1024, l == DEPTH - 1)
    y_prompt = x.reshape(B, T, D)
    nsa_kv_prompt = nsa_p.reshape(B, 4, G, hd, T).transpose(0, 4, 1, 2, 3)
    win_kv_prompt = win_p.reshape(B, 2, G, hd, T)[..., -min(WINDOW, T):].transpose(0, 4, 1, 2, 3)

    t_pos = past_len
    n_slc = -(-(past_len + 1) // SLC_BLOCK)
    xs = x_sample.reshape(Bs, D)
    cache_t = cache_nsa_kv.transpose(0, 2, 3, 4, 1).reshape(n_pool, 4, G * hd, page)
    win_t = cache_win_kv.transpose(0, 2, 3, 4, 1).reshape(Bs, 2, G * hd, w_buf)
    states_s = jnp.zeros(state_hgrn.shape, F32)
    for l in range(DEPTH):
        if l == N_A:
            kv_s = rms_proj(xs, norm_kv, w_kv, Bs, 512)
            cmp_s = compress_sample(cache_t, page_table, wbig, wab, pe, w2)
        if l < N_A:
            proj = rms_proj(xs, norm_mix[l], w_hgrn_in[l], Bs, 512).reshape(Bs, 1, -1)
            o, states_s = hgrn_step(proj, state_hgrn, hgrn_lb_logits, hgrn_onorm[l], l, states_s)
        else:
            proj = rms_proj(xs, norm_mix[l], wq[l - N_A], Bs, 384).reshape(Bs, 1, -1)
            o_c, idx = nsa_sample_select(proj, cmp_s, t_pos, n_slc)
            o = nsa_sample_attend(proj, o_c, kv_s.reshape(Bs, 1, -1), win_t, cache_t, page_table,
                                  idx.reshape(Bs, G * SLC_TOPK), t_pos, past_len)
        w_o = w_hgrn_out[l] if l < N_A else w_nsa_out[l - N_A]
        xs = proj_res(o.reshape(Bs, -1), w_o, xs, Bs)
        xs = mlp_res(xs, norm_mlp[l], w_mlp_up[l], w_mlp_down[l], norm_final, Bs, 512, l == DEPTH - 1)
    y_sample = xs.reshape(Bs, 1, D)
    n_nsa = 4 * G * hd
    nsa_kv_sample = kv_s[:, :n_nsa].reshape(Bs, 1, 4, G, hd)
    win_new = kv_s[:, n_nsa:].reshape(Bs, 1, 2, G, hd).astype(cache_win_kv.dtype)
    win_kv_sample = jnp.concatenate([cache_win_kv, win_new], axis=1)[:, -w_buf:]

    return (y_prompt, y_sample, nsa_kv_prompt, nsa_kv_sample, win_kv_prompt, win_kv_sample,
            states_p, states_s)
```

```python
import functools

import jax
import jax.numpy as jnp
from jax import lax
from jax.experimental import pallas as pl
from jax.experimental.pallas import tpu as pltpu

F32 = jnp.float32
BF16 = jnp.bfloat16

D_MODEL = 1024
DEPTH = 4
N_A = DEPTH // 2
D_FF = 4 * D_MODEL
RMS_EPS = 1e-6
HG_HEADS = 8
HG_K = 128
HG_V = 128
NSA_HEADS = 16
NSA_HEAD_DIM = 64
NSA_KV_GROUPS = 4
NSA_HPG = NSA_HEADS // NSA_KV_GROUPS
NSA_SCALE = NSA_HEAD_DIM ** -0.5
CMP_LEN = 32
CMP_STRIDE = 16
SLC_BLOCK = 64
SLC_TOPK = 16
WINDOW = 512
FORCED_SCORE = 1e4
NEG_INF = -1e30
TINY = 1e-30

HG_CHUNK = 128
HG_MATRIX_LEVELS = 2
VMEM_LIMIT = 56 * 1024 * 1024


def _cparams(*sem):
    return pltpu.CompilerParams(dimension_semantics=sem, vmem_limit_bytes=VMEM_LIMIT)


def _rms(x, g):
    return x * lax.rsqrt(jnp.mean(x * x, axis=-1, keepdims=True) + RMS_EPS) * g


def _sigmoid(x):
    return 1.0 / (1.0 + jnp.exp(-x))


def _dot(a, b):
    return jnp.dot(a, b, preferred_element_type=F32)


def _dot_nt(a, b):
    return lax.dot_general(a, b, (((1,), (1,)), ((), ())), preferred_element_type=F32)


def _dot_tn(a, b):
    return lax.dot_general(a, b, (((0,), (0,)), ((), ())), preferred_element_type=F32)


def _split3(x):
    hi = x.astype(BF16)
    r1 = x - hi.astype(F32)
    mid = r1.astype(BF16)
    lo = (r1 - mid.astype(F32)).astype(BF16)
    return hi, mid, lo


def _masked_softmax(s, mask):
    s = jnp.where(mask, s, NEG_INF)
    e = jnp.where(mask, jnp.exp(s - jnp.max(s, axis=-1, keepdims=True)), 0.0)
    return e / jnp.maximum(jnp.sum(e, axis=-1, keepdims=True), TINY)


def _rms_proj_kernel(x_ref, g_ref, w_ref, o_ref, y_ref):
    @pl.when(pl.program_id(1) == 0)
    def _():
        y_ref[...] = _rms(x_ref[...], g_ref[...]).astype(BF16)

    o_ref[...] = _dot(y_ref[...], w_ref[...].astype(BF16)).astype(o_ref.dtype)


def _stacked(w, layer):
    return (w[None], 0) if w.ndim == 2 else (w, layer)


def rms_proj(x, g, w, tm, tn, out_dtype=F32, layer=0):
    M, D = x.shape
    w, layer = _stacked(w, layer)
    N = w.shape[2]
    return pl.pallas_call(
        _rms_proj_kernel,
        grid=(M // tm, N // tn),
        in_specs=[pl.BlockSpec((tm, D), lambda i, j: (i, 0)),
                  pl.BlockSpec((1, D), lambda i, j: (0, 0)),
                  pl.BlockSpec((None, D, tn), lambda i, j: (layer, 0, j))],
        out_specs=pl.BlockSpec((tm, tn), lambda i, j: (i, j)),
        out_shape=jax.ShapeDtypeStruct((M, N), out_dtype),
        scratch_shapes=[pltpu.VMEM((tm, D), BF16)],
        compiler_params=_cparams("parallel", "arbitrary"),
        name="rms_proj",
    )(x, g.reshape(1, D), w)


def _proj_res_kernel(a_ref, w_ref, r_ref, o_ref, wb_ref):
    @pl.when(pl.program_id(0) == 0)
    def _():
        wb_ref[...] = w_ref[...].astype(BF16)

    o_ref[...] = r_ref[...] + _dot(a_ref[...].astype(BF16), wb_ref[...])


def proj_res(a, w, res, tm, layer=0):
    M, K = a.shape
    w, layer = _stacked(w, layer)
    N = w.shape[2]
    return pl.pallas_call(
        _proj_res_kernel,
        grid=(M // tm,),
        in_specs=[pl.BlockSpec((tm, K), lambda i: (i, 0)),
                  pl.BlockSpec((None, K, N), lambda i: (layer, 0, 0)),
                  pl.BlockSpec((tm, N), lambda i: (i, 0))],
        out_specs=pl.BlockSpec((tm, N), lambda i: (i, 0)),
        out_shape=jax.ShapeDtypeStruct((M, N), F32),
        scratch_shapes=[pltpu.VMEM((K, N), BF16)],
        compiler_params=_cparams("arbitrary"),
        name="proj_res",
    )(a, w, res)


def _mlp_kernel(x_ref, g_ref, wu_ref, wd_ref, gf_ref, o_ref, y_ref, acc_ref, *, final_norm):
    f = pl.program_id(1)

    @pl.when(f == 0)
    def _():
        y_ref[...] = _rms(x_ref[...], g_ref[...]).astype(BF16)
        acc_ref[...] = jnp.zeros_like(acc_ref)

    h = jnp.maximum(_dot(y_ref[...], wu_ref[...].astype(BF16)), 0.0)
    acc_ref[...] += _dot((h * h).astype(BF16), wd_ref[...].astype(BF16))

    @pl.when(f == pl.num_programs(1) - 1)
    def _():
        out = x_ref[...] + acc_ref[...]
        if final_norm:
            out = _rms(out, gf_ref[...])
        o_ref[...] = out


def mlp_res(x, g, w_up, w_down, g_final, tm, tf, final_norm, layer):
    M, D = x.shape
    Fdim = w_up.shape[2]
    return pl.pallas_call(
        functools.partial(_mlp_kernel, final_norm=final_norm),
        grid=(M // tm, Fdim // tf),
        in_specs=[pl.BlockSpec((tm, D), lambda i, f: (i, 0)),
                  pl.BlockSpec((1, D), lambda i, f: (0, 0)),
                  pl.BlockSpec((None, D, tf), lambda i, f: (layer, 0, f)),
                  pl.BlockSpec((None, tf, D), lambda i, f: (layer, f, 0)),
                  pl.BlockSpec((1, D), lambda i, f: (0, 0))],
        out_specs=pl.BlockSpec((tm, D), lambda i, f: (i, 0)),
        out_shape=jax.ShapeDtypeStruct((M, D), F32),
        scratch_shapes=[pltpu.VMEM((tm, D), BF16), pltpu.VMEM((tm, D), F32)],
        compiler_params=_cparams("parallel", "arbitrary"),
        name="mlp_res",
    )(x, g.reshape(1, D), w_up, w_down, g_final.reshape(1, D))


def _hgrn_lower_bound(lg, layer):
    m = jnp.max(lg, axis=0, keepdims=True)
    e = jnp.exp(lg - m)
    p = e / jnp.sum(e, axis=0, keepdims=True)
    lb = jnp.sum(p[1:layer + 1], axis=0, keepdims=True)
    return jnp.log(lb), jnp.log(1.0 - lb)


def _hgrn_logf(z, lg, layer):
    ls = jnp.minimum(z, 0.0) - jnp.log(1.0 + jnp.exp(-jnp.abs(z)))
    if layer == 0:
        return ls
    log_lb, log1m = _hgrn_lower_bound(lg, layer)
    b2 = log1m + ls
    return jnp.maximum(log_lb, b2) + jnp.log(1.0 + jnp.exp(-jnp.abs(log_lb - b2)))


def _hgrn_sum_matrices(C, n_lev):
    r = lax.broadcasted_iota(jnp.int32, (C, C), 0)
    u = lax.broadcasted_iota(jnp.int32, (C, C), 1)
    mats = [r >= u]
    for lev in range(n_lev):
        h = 1 << lev
        off = r & (2 * h - 1)
        mid = r - off + h
        mats.append(((off >= h) & (u >= mid) & (u <= r)) | ((off < h) & (u > r) & (u < mid)))
    return jnp.concatenate([jnp.where(m, 1.0, 0.0).astype(BF16) for m in mats], axis=0)


def _hgrn_kernel(zq_ref, zf_ref, zi_ref, zg_ref, lg_ref, on_ref, _states_in, o_ref, s_ref, st_ref, w_ref, b_ref,
                 *, layer, tc, nh):
    t = pl.program_id(2)
    C = HG_CHUNK
    n_lev = C.bit_length() - 1
    heads = range(nh)

    @pl.when(t == 0)
    def _():
        st_ref[...] = jnp.zeros_like(st_ref)
        w_ref[...] = _hgrn_sum_matrices(C, HG_MATRIX_LEVELS)

    r_i = lax.broadcasted_iota(jnp.int32, (C, C), 0)
    c_i = lax.broadcasted_iota(jnp.int32, (C, C), 1)
    row = lax.broadcasted_iota(jnp.int32, (C, HG_K), 0)

    def chunk(ci, sts):
        rows = pl.ds(pl.multiple_of(ci * C, C), C)
        kl = [slice(h * HG_K, (h + 1) * HG_K) for h in heads]
        vl = [slice(h * HG_V, (h + 1) * HG_V) for h in heads]
        q = [zq_ref[0, rows, kl[h]] for h in heads]
        vb = [zi_ref[0, rows, vl[h]].astype(BF16) for h in heads]
        logf = [_hgrn_logf(zf_ref[0, rows, kl[h]], lg_ref[:, kl[h]], layer) for h in heads]
        k = [1.0 - jnp.exp(logf[h]) for h in heads]
        parts = [_split3(logf[h]) for h in heads]
        w_sum, w_lev = w_ref[0:C, :], w_ref[C:, :]
        b = [_dot(w_sum, parts[h][0]) + _dot(w_sum, parts[h][1]) + _dot(w_sum, parts[h][2]) for h in heads]
        for h in heads:
            b_ref[h] = b[h]
        e_low = [jnp.exp(_dot(w_lev, parts[h][0]) + _dot(w_lev, parts[h][1])) for h in heads]
        a = [jnp.where(r_i == c_i, _dot_nt(q[h].astype(BF16), k[h].astype(BF16)), 0.0) for h in heads]
        for lev in range(n_lev):
            half = 1 << lev
            upper = (row & (2 * half - 1)) >= half
            same = (r_i >> (lev + 1)) == (c_i >> (lev + 1))
            if lev < HG_MATRIX_LEVELS:
                e = [e_low[h][lev * C:(lev + 1) * C] for h in heads]
            else:
                nblk = C // (2 * half)
                e = []
                for h in heads:
                    bm = b_ref[h, pl.ds(half - 1, nblk, stride=2 * half), :] if nblk > 1 else b_ref[h, half - 1:half, :]
                    bm = jnp.broadcast_to(bm[:, None, :], (nblk, 2 * half, HG_K)).reshape(C, HG_K)
                    e.append(jnp.exp(jnp.where(upper, b[h] - bm, bm - b[h])))
            qt = [jnp.where(upper, q[h] * e[h], 0.0).astype(BF16) for h in heads]
            kt = [jnp.where(upper, 0.0, k[h] * e[h]).astype(BF16) for h in heads]
            al = [_dot_nt(qt[h], kt[h]) for h in heads]
            a = [a[h] + (jnp.where(same, al[h], 0.0) if 2 * half < C else al[h]) for h in heads]
        qd = [(q[h] * jnp.exp(b[h])).astype(BF16) for h in heads]
        o = [_dot(a[h].astype(BF16), vb[h]) + _dot_nt(qd[h], sts[h].astype(BF16)) for h in heads]
        b_end = [b[h][C - 1:C] for h in heads]
        kd = [(k[h] * jnp.exp(b_end[h] - b[h])).astype(BF16) for h in heads]
        new = tuple(sts[h] * jnp.exp(b_end[h]) + _dot_tn(vb[h], kd[h]) for h in heads)
        for h in heads:
            oh = o[h] * lax.rsqrt(jnp.mean(o[h] * o[h], axis=-1, keepdims=True) + RMS_EPS) * on_ref[:, vl[h]]
            zg = zg_ref[0, rows, vl[h]]
            o_ref[0, rows, vl[h]] = (oh * (zg * _sigmoid(zg))).astype(o_ref.dtype)
        return new

    sts = lax.fori_loop(0, tc // C, chunk, tuple(st_ref[h] for h in heads))
    for h in heads:
        st_ref[h] = sts[h]

    @pl.when(t == pl.num_programs(2) - 1)
    def _():
        for h in heads:
            s_ref[0, 0, h] = st_ref[h].T


def hgrn_prompt(proj, lb_logits, onorm, layer, states, tc=512, nh=4):
    B, T, _ = proj.shape
    H = HG_HEADS
    hp = H // nh
    n_mats = 1 + HG_MATRIX_LEVELS
    alias_spec, alias_arg, aliases = [pl.BlockSpec(memory_space=pl.ANY)], [states], {6: 1}
    return pl.pallas_call(
        functools.partial(_hgrn_kernel, layer=layer, tc=tc, nh=nh),
        grid=(B, hp, T // tc),
        in_specs=[pl.BlockSpec((1, tc, nh * HG_K), lambda b, h, t: (b, t, h)),
                  pl.BlockSpec((1, tc, nh * HG_K), lambda b, h, t: (b, t, hp + h)),
                  pl.BlockSpec((1, tc, nh * HG_V), lambda b, h, t: (b, t, 2 * hp + h)),
                  pl.BlockSpec((1, tc, nh * HG_V), lambda b, h, t: (b, t, 3 * hp + h)),
                  pl.BlockSpec((N_A, nh * HG_K), lambda b, h, t: (0, h)),
                  pl.BlockSpec((1, nh * HG_V), lambda b, h, t: (0, h))] + alias_spec,
        out_specs=[pl.BlockSpec((1, tc, nh * HG_V), lambda b, h, t: (b, t, h)),
                   pl.BlockSpec((1, 1, nh, HG_K, HG_V), lambda b, h, t: (layer, b, h, 0, 0))],
        out_shape=[jax.ShapeDtypeStruct((B, T, H * HG_V), BF16),
                   jax.ShapeDtypeStruct((N_A, B, H, HG_K, HG_V), F32)],
        input_output_aliases=aliases,
        scratch_shapes=[pltpu.VMEM((nh, HG_V, HG_K), F32),
                        pltpu.VMEM((n_mats * HG_CHUNK, HG_CHUNK), BF16),
                        pltpu.VMEM((nh, HG_CHUNK, HG_K), F32)],
        compiler_params=_cparams("parallel", "parallel", "arbitrary"),
        name="hgrn_prompt",
    )(proj, proj, proj, proj, lb_logits, onorm.reshape(1, H * HG_V), *alias_arg)


QK_LANES = 2 * NSA_HEAD_DIM
VT_ROWS = NSA_HEAD_DIM + 16


def _kv_proj_kernel(x_ref, g_ref, w_ref, nsa_ref, win_ref, k_ref, vt_ref, wb_ref, *, tpb):
    @pl.when(pl.program_id(0) == 0)
    def _():
        wb_ref[...] = w_ref[...].astype(BF16)

    G, hd = NSA_KV_GROUPS, NSA_HEAD_DIM
    tm = x_ref.shape[0]
    y = _rms(x_ref[...], g_ref[...]).astype(BF16)
    kv = _dot(y, wb_ref[...])
    n_nsa = nsa_ref.shape[1]
    cols_t = [kv[:, c:c + 2 * hd].T for c in range(0, kv.shape[1], 2 * hd)]
    for n, t in enumerate(cols_t):
        c = n * 2 * hd
        if c < n_nsa:
            nsa_ref[0, c:c + 2 * hd, :] = t
        else:
            win_ref[0, c - n_nsa:c - n_nsa + 2 * hd, :] = t
    lane = lax.broadcasted_iota(jnp.int32, (tm, QK_LANES), 1)
    blk = ((pl.program_id(0) % tpb) * tm + lax.broadcasted_iota(jnp.int32, (tm, QK_LANES), 0)) // SLC_BLOCK
    tails = (jnp.where(lane - hd == blk, 1.0, 0.0), jnp.zeros((tm, QK_LANES), F32))
    for n, kind in enumerate((2, 4)):
        for gp in range(G // 2):
            col = (kind * G + 2 * gp) * hd
            pair = kv[:, col:col + 2 * hd]
            for gl, src in enumerate((pair, pltpu.roll(pair, hd, axis=1))):
                k_ref[0, n * G + 2 * gp + gl] = jnp.where(lane < hd, src, tails[n]).astype(BF16)
    ones_row = jnp.where(lax.broadcasted_iota(jnp.int32, (VT_ROWS - hd, tm), 0) == 0, 1.0, 0.0).astype(BF16)
    for n, kind in enumerate((3, 5)):
        for gp in range(G // 2):
            t = cols_t[(kind * G + 2 * gp) * hd // (2 * hd)]
            for gl in range(2):
                vt_ref[0, n * G + 2 * gp + gl, 0:hd, :] = t[gl * hd:(gl + 1) * hd].astype(BF16)
                vt_ref[0, n * G + 2 * gp + gl, hd:, :] = ones_row


def kv_proj_prompt(x, g, w_kv, B, T, tm=512):
    M, D = x.shape
    N = w_kv.shape[1]
    G, hd = NSA_KV_GROUPS, NSA_HEAD_DIM
    n_nsa = 4 * G * hd
    tpb = T // tm
    assert T // SLC_BLOCK <= QK_LANES - hd
    return pl.pallas_call(
        functools.partial(_kv_proj_kernel, tpb=tpb),
        grid=(M // tm,),
        in_specs=[pl.BlockSpec((tm, D), lambda i: (i, 0)),
                  pl.BlockSpec((1, D), lambda i: (0, 0)),
                  pl.BlockSpec((D, N), lambda i: (0, 0))],
        out_specs=[pl.BlockSpec((1, n_nsa, tm), lambda i: (i // tpb, 0, i % tpb)),
                   pl.BlockSpec((1, N - n_nsa, tm), lambda i: (i // tpb, 0, i % tpb)),
                   pl.BlockSpec((1, 2 * G, tm, QK_LANES), lambda i: (i // tpb, 0, i % tpb, 0)),
                   pl.BlockSpec((1, 2 * G, VT_ROWS, tm), lambda i: (i // tpb, 0, 0, i % tpb))],
        out_shape=[jax.ShapeDtypeStruct((B, n_nsa, T), F32),
                   jax.ShapeDtypeStruct((B, N - n_nsa, T), F32),
                   jax.ShapeDtypeStruct((B, 2 * G, T, QK_LANES), BF16),
                   jax.ShapeDtypeStruct((B, 2 * G, VT_ROWS, T), BF16)],
        scratch_shapes=[pltpu.VMEM((D, N), BF16)],
        compiler_params=_cparams("arbitrary"),
        name="kv_proj",
    )(x, g.reshape(1, D), w_kv)


GATE_ROWS = 16
GATE_LANES = 128


def _q_proj_kernel(x_ref, g_ref, w_ref, q_ref, gt_ref, wb_ref):
    @pl.when(pl.program_id(0) == 0)
    def _():
        wb_ref[...] = w_ref[...].astype(BF16)

    y = _rms(x_ref[...], g_ref[...]).astype(BF16)
    pr = _dot(y, wb_ref[...])
    hd = NSA_HEAD_DIM
    nq = NSA_HEADS * hd
    low = lax.broadcasted_iota(jnp.int32, (pr.shape[0], QK_LANES), 1) < hd
    for hp in range(NSA_HEADS // 2):
        pair = pr[:, hp * 2 * hd:(hp + 1) * 2 * hd] * NSA_SCALE
        for hl, src in enumerate((pair, pltpu.roll(pair, hd, axis=1))):
            q_ref[0, 2 * hp + hl] = jnp.where(low, src, 0.0).astype(BF16)
    gates_t = _sigmoid(pr[:, nq:]).T
    for gi in range(NSA_KV_GROUPS):
        gt_ref[0, gi] = gates_t[gi * GATE_ROWS:(gi + 1) * GATE_ROWS]


def _permute_gate_cols(w_q):
    nq = NSA_HEADS * NSA_HEAD_DIM
    d = w_q.shape[0]
    wg = w_q[:, nq:].reshape(d, 3, NSA_KV_GROUPS, NSA_HPG).transpose(0, 2, 1, 3).reshape(d, NSA_KV_GROUPS, 3 * NSA_HPG)
    wg = jnp.pad(wg, ((0, 0), (0, 0), (0, GATE_ROWS - 3 * NSA_HPG))).reshape(d, NSA_KV_GROUPS * GATE_ROWS)
    wg = jnp.pad(wg, ((0, 0), (0, GATE_LANES - NSA_KV_GROUPS * GATE_ROWS)))
    return jnp.concatenate([w_q[:, :nq], wg], axis=1)


def q_proj_prompt(x, g, w_qp, B, T, tm=512):
    M, D = x.shape
    N = w_qp.shape[1]
    tpb = T // tm
    return pl.pallas_call(
        _q_proj_kernel,
        grid=(M // tm,),
        in_specs=[pl.BlockSpec((tm, D), lambda i: (i, 0)),
                  pl.BlockSpec((1, D), lambda i: (0, 0)),
                  pl.BlockSpec((D, N), lambda i: (0, 0))],
        out_specs=[pl.BlockSpec((1, NSA_HEADS, tm, QK_LANES), lambda i: (i // tpb, 0, i % tpb, 0)),
                   pl.BlockSpec((1, NSA_KV_GROUPS, GATE_ROWS, tm), lambda i: (i // tpb, 0, 0, i % tpb))],
        out_shape=[jax.ShapeDtypeStruct((B, NSA_HEADS, T, QK_LANES), BF16),
                   jax.ShapeDtypeStruct((B, NSA_KV_GROUPS, GATE_ROWS, T), F32)],
        scratch_shapes=[pltpu.VMEM((D, N), BF16)],
        compiler_params=_cparams("arbitrary"),
        name="q_proj",
    )(x, g.reshape(1, D), w_qp)


def _cmp_weights(pe_k, w1_k, pe_v, w1_v, w2_k, w2_v):
    half = CMP_STRIDE * NSA_HEAD_DIM

    def ab(w1):
        return jnp.concatenate([w1[:half], w1[half:]], axis=1)

    def big(w1):
        w = w1.reshape(2, CMP_STRIDE, NSA_HEAD_DIM, -1)
        b = jnp.einsum("alds,gh->lgdhas", w, jnp.eye(2, dtype=w1.dtype))
        return b.reshape(CMP_STRIDE * 2 * NSA_HEAD_DIM, 2 * 2 * w.shape[-1])

    wab = jnp.stack([ab(w1_k), ab(w1_v)])
    wbig = jnp.stack([big(w1_k), big(w1_v)])
    pe = jnp.stack([pe_k.reshape(2, half), pe_v.reshape(2, half)])
    w2 = jnp.stack([w2_k, w2_v])
    w2t = jnp.stack([w2_k.T, w2_v.T])
    return wab, wbig, pe, w2, w2t


def _cmp_taps(x_ref, ns):
    return jnp.concatenate([x_ref[pl.ds(l, ns, stride=CMP_STRIDE), :].astype(BF16) for l in range(CMP_STRIDE)], axis=1)


def _cmp_hidden(ab, pe, wab):
    hd = NSA_HEAD_DIM
    n = ab.shape[0]
    nxt = pltpu.roll(ab, n - 1, axis=0)
    pt = _dot(pe.astype(BF16), wab)
    hid = ab[:, :hd] + nxt[:, hd:] + pt[0:1, :hd] + pt[1:2, hd:]
    return hid * _sigmoid(hid)


def _cmp_prompt_kernel(xk_ref, xv_ref, wbig_ref, wab_ref, pe_ref, w2_ref, w2t_ref, kc_ref, vct_ref, buf_ref):
    hd = NSA_HEAD_DIM
    ns = xk_ref.shape[2] // CMP_STRIDE
    for c, x_ref in enumerate((xk_ref, xv_ref)):
        wab = wab_ref[c].astype(BF16)
        buf_ref[...] = x_ref[0].T
        ab2 = _dot(_cmp_taps(buf_ref, ns), wbig_ref[c].astype(BF16))
        for gl in range(2):
            act = _cmp_hidden(ab2[:, gl * 2 * hd:(gl + 1) * 2 * hd], pe_ref[c], wab).astype(BF16)
            if c == 0:
                kc_ref[0, gl] = _dot(act, w2_ref[c].astype(BF16)).astype(kc_ref.dtype)
            else:
                vct_ref[0, gl] = _dot_nt(w2t_ref[c].astype(BF16), act).astype(vct_ref.dtype)


def compress_prompt(nsa_t, wbig, wab, pe, w2, w2t):
    B, _, T = nsa_t.shape
    G, hd = NSA_KV_GROUPS, NSA_HEAD_DIM
    ns = T // CMP_STRIDE
    w2 = jnp.pad(w2, ((0, 0), (0, 0), (0, QK_LANES - hd)))
    const = lambda a: pl.BlockSpec(a.shape, lambda b, gp: (0,) * a.ndim)
    return pl.pallas_call(
        _cmp_prompt_kernel,
        grid=(B, G // 2),
        in_specs=[pl.BlockSpec((1, 2 * hd, T), lambda b, gp: (b, gp, 0)),
                  pl.BlockSpec((1, 2 * hd, T), lambda b, gp: (b, G // 2 + gp, 0)),
                  const(wbig), const(wab), const(pe), const(w2), const(w2t)],
        out_specs=[pl.BlockSpec((1, 2, ns, QK_LANES), lambda b, gp: (b, gp, 0, 0)),
                   pl.BlockSpec((1, 2, hd, ns), lambda b, gp: (b, gp, 0, 0))],
        out_shape=[jax.ShapeDtypeStruct((B, G, ns, QK_LANES), BF16),
                   jax.ShapeDtypeStruct((B, G, hd, ns), BF16)],
        scratch_shapes=[pltpu.VMEM((T, 2 * hd), F32)],
        compiler_params=_cparams("parallel", "parallel"),
        name="compress_prompt",
    )(nsa_t, nsa_t, wbig, wab, pe, w2, w2t)


def _nsa_prompt_kernel(q_ref, gt_ref, kc_ref, vct_ref, ks_ref, vst_ref, kw_ref, vwt_ref, o_ref, *, tq):
    i = pl.program_id(2)
    s0 = i * tq
    hpg, hd = NSA_HPG, NSA_HEAD_DIM
    T = ks_ref.shape[2]
    n_cmp = kc_ref.shape[2]
    n_slc = T // SLC_BLOCK
    R = hpg * tq
    Q = q_ref[0].reshape(R, QK_LANES)
    tpos = s0 + lax.broadcasted_iota(jnp.int32, (1, R), 1) % tq

    sc = _dot_nt(kc_ref[0, 0], Q)
    ok_c = lax.broadcasted_iota(jnp.int32, (n_cmp, R), 0) * CMP_STRIDE + (CMP_LEN - 1) <= tpos
    sc = jnp.where(ok_c, sc, NEG_INF)
    e_c = jnp.where(ok_c, jnp.exp(sc - jnp.max(sc, axis=0, keepdims=True)), 0.0)
    p_c = e_c / jnp.maximum(jnp.sum(e_c, axis=0, keepdims=True), TINY)
    o_c = _dot(vct_ref[0, 0], p_c.astype(BF16))

    psum = p_c[:, 0:tq]
    for h in range(1, hpg):
        psum = psum + p_c[:, h * tq:(h + 1) * tq]
    sj = lax.broadcasted_iota(jnp.int32, (n_slc, n_cmp), 0) * SLC_BLOCK
    ci = lax.broadcasted_iota(jnp.int32, (n_slc, n_cmp), 1) * CMP_STRIDE
    ov = jnp.where((ci < sj + SLC_BLOCK) & (ci + CMP_LEN > sj), 1.0, 0.0).astype(BF16)
    hi, mid, lo = _split3(psum)
    imp = _dot(ov, hi) + _dot(ov, mid) + _dot(ov, lo)
    blk = lax.broadcasted_iota(jnp.int32, (n_slc, tq), 0)
    qpos = s0 + lax.broadcasted_iota(jnp.int32, (n_slc, tq), 1)
    cur = qpos // SLC_BLOCK
    forced = (blk == 0) | (blk == cur) | (blk == cur - 1)
    score = jnp.where(forced, FORCED_SCORE, jnp.where(blk <= cur, imp, -1.0))
    rank = jnp.zeros((n_slc, tq), F32)
    for j in range(n_slc):
        cj = score[j:j + 1, :]
        rank = rank + jnp.where((cj > score) | ((cj == score) & (blk > j)), 1.0, 0.0)
    bias = jnp.where(rank < SLC_TOPK, 0.0, NEG_INF)
    bias = jnp.concatenate([jnp.zeros((hd, tq), F32), bias, jnp.zeros((QK_LANES - hd - n_slc, tq), F32)], axis=0)
    bias_t = bias.T.astype(BF16)
    q_sel = Q + jnp.concatenate([bias_t] * hpg, axis=0)

    heads = range(hpg)
    q_heads = [q_sel[h * tq:(h + 1) * tq] for h in heads]
    tpos_h = tpos[:, :tq]

    def chunk(c, carry, diagonal):
        k0 = pl.multiple_of(c * tq, tq)
        ks = ks_ref[0, 0, pl.ds(k0, tq), :]
        vst = vst_ref[0, 0, :, pl.ds(k0, tq)]
        s_heads = [_dot_nt(ks, q_heads[h]) for h in heads]
        new = []
        for h in heads:
            m, acc = carry[h]
            s = s_heads[h]
            if diagonal:
                s = jnp.where(k0 + lax.broadcasted_iota(jnp.int32, (tq, tq), 0) <= tpos_h, s, NEG_INF)
            m_new = jnp.maximum(m, jnp.max(s, axis=0, keepdims=True))
            p = jnp.exp(s - m_new)
            new.append((m_new, jnp.exp(m - m_new) * acc + _dot(vst, p.astype(BF16))))
        return tuple(new)

    carry = tuple((jnp.full((1, tq), NEG_INF, F32), jnp.zeros((VT_ROWS, tq), F32)) for _ in heads)
    carry = lax.fori_loop(0, i, lambda c, cr: chunk(c, cr, False), carry)
    carry = chunk(i, carry, True)
    o_s = jnp.concatenate([acc[:hd] / jnp.maximum(acc[hd:hd + 1], TINY) for _, acc in carry], axis=1)

    ws = pl.multiple_of(jnp.maximum(s0 - WINDOW, 0), tq)
    parts = []
    for j in range(WINDOW // tq + 1):
        k0 = pl.multiple_of(ws + j * tq, tq)
        s = _dot_nt(kw_ref[0, 0, pl.ds(k0, tq), :], Q)
        wpos = k0 + lax.broadcasted_iota(jnp.int32, (tq, R), 0)
        ok = (wpos <= tpos) & (wpos > tpos - WINDOW) if j == 0 else wpos <= tpos
        parts.append((k0, jnp.where(ok, s, NEG_INF)))
    m_w = functools.reduce(jnp.maximum, [jnp.max(s, axis=0, keepdims=True) for _, s in parts])
    acc_w = sum(_dot(vwt_ref[0, 0, :, pl.ds(k0, tq)], jnp.exp(s - m_w).astype(BF16)) for k0, s in parts)
    o_w = acc_w[:hd] / jnp.maximum(acc_w[hd:hd + 1], TINY)

    gt = gt_ref[0, 0]
    outs = []
    for h in range(hpg):
        cols = slice(h * tq, (h + 1) * tq)
        outs.append(gt[h:h + 1] * o_c[:, cols] + gt[hpg + h:hpg + h + 1] * o_s[:, cols]
                    + gt[2 * hpg + h:2 * hpg + h + 1] * o_w[:, cols])
    for pair in range(hpg // 2):
        both = jnp.concatenate(outs[2 * pair:2 * pair + 2], axis=0)
        o_ref[0, :, pair * 2 * hd:(pair + 1) * 2 * hd] = both.T.astype(o_ref.dtype)


def nsa_prompt(q_hm, gates_t, kc, vct, k_hm, vt_hm, tq=256):
    B, _, T, _ = q_hm.shape
    G, hpg, hd = NSA_KV_GROUPS, NSA_HPG, NSA_HEAD_DIM
    n_cmp = kc.shape[2]
    assert WINDOW % tq == 0 and T >= WINDOW + tq
    keys = lambda off: pl.BlockSpec((1, 1, T, QK_LANES), lambda b, g, i: (b, off + g, 0, 0))
    vals = lambda off: pl.BlockSpec((1, 1, VT_ROWS, T), lambda b, g, i: (b, off + g, 0, 0))
    return pl.pallas_call(
        functools.partial(_nsa_prompt_kernel, tq=tq),
        grid=(B, G, T // tq),
        in_specs=[pl.BlockSpec((1, hpg, tq, QK_LANES), lambda b, g, i: (b, g, i, 0)),
                  pl.BlockSpec((1, 1, GATE_ROWS, tq), lambda b, g, i: (b, g, 0, i)),
                  pl.BlockSpec((1, 1, n_cmp, QK_LANES), lambda b, g, i: (b, g, 0, 0)),
                  pl.BlockSpec((1, 1, hd, n_cmp), lambda b, g, i: (b, g, 0, 0)),
                  keys(0), vals(0), keys(G), vals(G)],
        out_specs=pl.BlockSpec((1, tq, hpg * hd), lambda b, g, i: (b, i, g)),
        out_shape=jax.ShapeDtypeStruct((B, T, NSA_HEADS * hd), BF16),
        compiler_params=_cparams("parallel", "parallel", "arbitrary"),
        name="nsa_prompt",
    )(q_hm, gates_t, kc, vct, k_hm, vt_hm, k_hm, vt_hm)


def _row_to_col(row):
    n = row.shape[1]
    eye = lax.broadcasted_iota(jnp.int32, (n, n), 0) == lax.broadcasted_iota(jnp.int32, (n, n), 1)
    return jnp.sum(jnp.where(eye, jnp.broadcast_to(row, (n, n)), 0.0), axis=-1, keepdims=True)


def _hgrn_step_kernel(z_ref, s0_ref, lg_ref, on_ref, _states_in, o_ref, s_ref, *, layer):
    s0_ref, s_ref = s0_ref.at[0], s_ref.at[0]
    hk = HG_HEADS * HG_K
    hv = HG_HEADS * HG_V
    for h in range(HG_HEADS):
        kl = slice(h * HG_K, (h + 1) * HG_K)
        vl = slice(h * HG_V, (h + 1) * HG_V)
        q = z_ref[0, :, kl]
        logf = _hgrn_logf(z_ref[0, :, hk + h * HG_K:hk + (h + 1) * HG_K], lg_ref[:, kl], layer)
        f = jnp.exp(logf)
        v = z_ref[0, :, 2 * hk + h * HG_V:2 * hk + (h + 1) * HG_V]
        zg = z_ref[0, :, 2 * hk + hv + h * HG_V:2 * hk + hv + (h + 1) * HG_V]
        s = _row_to_col(f) * s0_ref[0, h] + _row_to_col(1.0 - f) * v
        s_ref[0, h] = s
        o = jnp.sum(_row_to_col(q) * s, axis=0, keepdims=True)
        o = o * lax.rsqrt(jnp.mean(o * o, axis=-1, keepdims=True) + RMS_EPS) * on_ref[:, vl]
        o_ref[0, :, vl] = o * (zg * _sigmoid(zg))


def hgrn_step(proj, s0_all, lb_logits, onorm, layer, states):
    B = proj.shape[0]
    H = HG_HEADS
    alias_spec, alias_arg, aliases = [pl.BlockSpec(memory_space=pl.ANY)], [states], {4: 1}
    state_block = pl.BlockSpec((1, 1, H, HG_K, HG_V), lambda b: (layer, b, 0, 0, 0))
    return pl.pallas_call(
        functools.partial(_hgrn_step_kernel, layer=layer),
        grid=(B,),
        in_specs=[pl.BlockSpec((1, 1, proj.shape[2]), lambda b: (b, 0, 0)),
                  state_block,
                  pl.BlockSpec((N_A, H * HG_K), lambda b: (0, 0)),
                  pl.BlockSpec((1, H * HG_V), lambda b: (0, 0))] + alias_spec,
        out_specs=[pl.BlockSpec((1, 1, H * HG_V), lambda b: (b, 0, 0)), state_block],
        out_shape=[jax.ShapeDtypeStruct((B, 1, H * HG_V), F32),
                   jax.ShapeDtypeStruct(s0_all.shape, F32)],
        input_output_aliases=aliases,
        compiler_params=_cparams("parallel"),
        name="hgrn_step",
    )(proj, s0_all, lb_logits, onorm.reshape(1, H * HG_V), *alias_arg)


def _cmp_sample_kernel(pt_ref, cache_ref, wbig_ref, wab_ref, pe_ref, w2_ref, o_ref, raw_ref, buf_ref, wb_ref,
                       sem, *, n_pages):
    b = pl.program_id(0)
    nb = pl.num_programs(0)
    G, hd = NSA_KV_GROUPS, NSA_HEAD_DIM
    hp = n_pages // 2
    spp = cache_ref.shape[3] // CMP_STRIDE
    ns = n_pages * spp

    def page_copy(bb, half, p):
        src = cache_ref.at[pt_ref[bb, half * hp + p], pl.ds(0, 2)]
        return pltpu.make_async_copy(src, raw_ref.at[half, p], sem.at[half])

    def start_half(bb, half):
        def body(p, c):
            page_copy(bb, half, p).start()
            return c
        lax.fori_loop(0, hp, body, 0)

    def wait_half(bb, half):
        def body(p, c):
            page_copy(bb, half, p).wait()
            return c
        lax.fori_loop(0, hp, body, 0)

    page = cache_ref.shape[3]
    r_i = lax.broadcasted_iota(jnp.int32, (page, page), 0)
    c_i = lax.broadcasted_iota(jnp.int32, (page, page), 1)
    perm = jnp.where(c_i == (r_i % spp) * CMP_STRIDE + r_i // spp, 1.0, 0.0).astype(BF16)

    def to_token_rows(half):
        def body(p, c):
            for kind in range(2):
                moved = _dot_nt(raw_ref[half, p, kind].astype(BF16), perm)
                for gp in range(G // 2):
                    buf_ref[kind * (G // 2) + gp, half * hp + p] = moved[gp * 2 * hd:(gp + 1) * 2 * hd].T
            return c
        lax.fori_loop(0, hp, body, 0, unroll=4)

    @pl.when(b == 0)
    def _():
        start_half(0, 0)
        start_half(0, 1)
        wb_ref[...] = wbig_ref[...].astype(BF16)

    for half in range(2):
        wait_half(b, half)
        to_token_rows(half)

        @pl.when(b + 1 < nb)
        def _():
            start_half(b + 1, half)

    for c in range(2):
        wab = wab_ref[c].astype(BF16)
        w2 = w2_ref[c].astype(BF16)
        for gp in range(G // 2):
            cb = c * (G // 2) + gp
            x = jnp.concatenate(
                [buf_ref[cb, :, l * spp:(l + 1) * spp, :].reshape(ns, 2 * hd).astype(BF16)
                 for l in range(CMP_STRIDE)], axis=1)
            ab2 = _dot(x, wb_ref[c])
            for gl in range(2):
                act = _cmp_hidden(ab2[:, gl * 2 * hd:(gl + 1) * 2 * hd], pe_ref[c], wab)
                col = (c * G + 2 * gp + gl) * hd
                o_ref[0, :, col:col + hd] = _dot(act.astype(BF16), w2).astype(o_ref.dtype)


def compress_sample(cache_t, page_table, wbig, wab, pe, w2):
    B, n_pages = page_table.shape
    page = cache_t.shape[3]
    G, hd = NSA_KV_GROUPS, NSA_HEAD_DIM
    ns = n_pages * page // CMP_STRIDE
    assert n_pages % 2 == 0 and page == 2 * hd
    const = lambda a: pl.BlockSpec(a.shape, lambda b, pt: (0,) * a.ndim)
    grid_spec = pltpu.PrefetchScalarGridSpec(
        num_scalar_prefetch=1,
        grid=(B,),
        in_specs=[pl.BlockSpec(memory_space=pl.ANY), const(wbig), const(wab), const(pe), const(w2)],
        out_specs=pl.BlockSpec((1, ns, 2 * G * hd), lambda b, pt: (b, 0, 0)),
        scratch_shapes=[pltpu.VMEM((2, n_pages // 2, 2, G * hd, page), F32),
                        pltpu.VMEM((G, n_pages, page, 2 * hd), F32),
                        pltpu.VMEM(wbig.shape, BF16),
                        pltpu.SemaphoreType.DMA((2,))],
    )
    return pl.pallas_call(
        functools.partial(_cmp_sample_kernel, n_pages=n_pages),
        grid_spec=grid_spec,
        out_shape=jax.ShapeDtypeStruct((B, ns, 2 * G * hd), BF16),
        compiler_params=_cparams("arbitrary"),
        name="compress_sample",
    )(page_table, cache_t, wbig, wab, pe, w2)


def _group_queries(pr_ref, g):
    hd = NSA_HEAD_DIM
    rows = [pr_ref[0, :, (g * NSA_HPG + h) * hd:(g * NSA_HPG + h + 1) * hd] for h in range(NSA_HPG)]
    return jnp.concatenate(rows, axis=0) * NSA_SCALE


def _nsa_sample_select_kernel(pr_ref, cmp_ref, oc_ref, idx_ref, *, t_pos, n_slc, n_pad):
    G, hpg, hd = NSA_KV_GROUPS, NSA_HPG, NSA_HEAD_DIM
    n_cmp = cmp_ref.shape[1]
    cmp = cmp_ref[0]
    ci = lax.broadcasted_iota(jnp.int32, (n_cmp, n_pad), 0) * CMP_STRIDE
    sj = lax.broadcasted_iota(jnp.int32, (n_cmp, n_pad), 1) * SLC_BLOCK
    ov = jnp.where((ci < sj + SLC_BLOCK) & (ci + CMP_LEN > sj), 1.0, 0.0).astype(BF16)
    blk = lax.broadcasted_iota(jnp.int32, (1, n_pad), 1)
    cur = t_pos // SLC_BLOCK
    forced = (blk == 0) | (blk == cur) | (blk == cur - 1)
    jr = lax.broadcasted_iota(jnp.int32, (n_pad, n_pad), 0)
    jc = lax.broadcasted_iota(jnp.int32, (n_pad, n_pad), 1)
    for g in range(G):
        qg = _group_queries(pr_ref, g).astype(BF16)
        sc = _dot_nt(qg, cmp[:, g * hd:(g + 1) * hd])
        e_pos = lax.broadcasted_iota(jnp.int32, (hpg, n_cmp), 1) * CMP_STRIDE + (CMP_LEN - 1)
        p_c = _masked_softmax(sc, e_pos <= t_pos)
        o_c = _dot(p_c.astype(BF16), cmp[:, (G + g) * hd:(G + g + 1) * hd])
        for h in range(hpg):
            col = (g * hpg + h) * hd
            oc_ref[0, :, col:col + hd] = o_c[h:h + 1]
        hi, mid, lo = _split3(jnp.sum(p_c, axis=0, keepdims=True))
        imp = _dot(hi, ov) + _dot(mid, ov) + _dot(lo, ov)
        score = jnp.where(forced, FORCED_SCORE, jnp.where(blk <= cur, imp, -1.0))
        score = jnp.where(blk < n_slc, score, -2.0)
        col_s = _row_to_col(score)
        beats = (col_s > score) | ((col_s == score) & (jr < jc))
        rank = jnp.sum(jnp.where(beats, 1.0, 0.0), axis=0, keepdims=True)
        rr = lax.broadcasted_iota(jnp.int32, (SLC_TOPK, n_pad), 0).astype(F32)
        bsel = jnp.where(jnp.broadcast_to(rank, (SLC_TOPK, n_pad)) == rr,
                         lax.broadcasted_iota(jnp.int32, (SLC_TOPK, n_pad), 1).astype(F32), 0.0)
        idx_ref[0, g * SLC_TOPK:(g + 1) * SLC_TOPK, :] = jnp.sum(bsel, axis=-1, keepdims=True).astype(jnp.int32)


def nsa_sample_select(proj, cmp_s, t_pos, n_slc):
    B = proj.shape[0]
    n_pad = -(-n_slc // 128) * 128
    G = NSA_KV_GROUPS
    nq = NSA_HEADS * NSA_HEAD_DIM
    return pl.pallas_call(
        functools.partial(_nsa_sample_select_kernel, t_pos=t_pos, n_slc=n_slc, n_pad=n_pad),
        grid=(B,),
        in_specs=[pl.BlockSpec((1, 1, proj.shape[2]), lambda b: (b, 0, 0)),
                  pl.BlockSpec((1,) + cmp_s.shape[1:], lambda b: (b, 0, 0))],
        out_specs=[pl.BlockSpec((1, 1, nq), lambda b: (b, 0, 0)),
                   pl.BlockSpec((1, G * SLC_TOPK, 1), lambda b: (b, 0, 0))],
        out_shape=[jax.ShapeDtypeStruct((B, 1, nq), F32),
                   jax.ShapeDtypeStruct((B, G * SLC_TOPK, 1), jnp.int32)],
        compiler_params=_cparams("parallel"),
        name="nsa_sample_select",
    )(proj, cmp_s)


def _nsa_sample_attend_kernel(pt_ref, idx_ref, pr_ref, oc_ref, kvn_ref, win_ref, cache_ref, o_ref,
                              kvbuf_ref, sem, *, t_pos, past_len):
    b = pl.program_id(0)
    nb = pl.num_programs(0)
    G, hpg, hd = NSA_KV_GROUPS, NSA_HPG, NSA_HEAD_DIM
    n_sel = G * SLC_TOPK
    page = cache_ref.shape[3]
    bpp = page // SLC_BLOCK
    new_blk = past_len // SLC_BLOCK

    def blk_copy(bb, n):
        slot = bb % 2
        j = jnp.minimum(idx_ref[bb, n], new_blk - 1)
        pg = pt_ref[bb, j // bpp]
        rows = pl.ds(pl.multiple_of((n // SLC_TOPK) * hd, hd), hd)
        return pltpu.make_async_copy(cache_ref.at[pg, pl.ds(2, 2), rows], kvbuf_ref.at[slot, n], sem.at[slot])

    def start_all(bb):
        def body(n, c):
            blk_copy(bb, n).start()
            return c
        lax.fori_loop(0, n_sel, body, 0)

    def wait_all(bb):
        def body(n, c):
            blk_copy(bb, n).wait()
            return c
        lax.fori_loop(0, n_sel, body, 0)

    @pl.when(b == 0)
    def _():
        start_all(0)

    @pl.when(b + 1 < nb)
    def _():
        start_all(b + 1)

    wait_all(b)
    slot = b % 2

    nk = SLC_TOPK * page
    w_buf = win_ref.shape[3]
    nq = NSA_HEADS * hd
    gates = _sigmoid(pr_ref[0, :, nq:nq + GATE_LANES])
    kvn = kvn_ref[0]

    def new_row(kind, g):
        return kvn[:, (kind * G + g) * hd:(kind * G + g + 1) * hd].astype(BF16).astype(F32)

    def attend_with_new(qg, s, ok, vt, k_new, v_new):
        s_new = jnp.sum(qg.astype(F32) * k_new, axis=-1, keepdims=True)
        s = jnp.where(ok, s, NEG_INF)
        m = jnp.maximum(jnp.max(s, axis=-1, keepdims=True), s_new)
        e = jnp.where(ok, jnp.exp(s - m), 0.0)
        e_new = jnp.exp(s_new - m)
        den = jnp.maximum(jnp.sum(e, axis=-1, keepdims=True) + e_new, TINY)
        return (_dot_nt(e.astype(BF16), vt) + e_new * v_new) / den

    lane = lax.broadcasted_iota(jnp.int32, (1, nk), 1)
    for g in range(G):
        qg = _group_queries(pr_ref, g).astype(BF16)
        kt = jnp.concatenate([kvbuf_ref[slot, g * SLC_TOPK + r, 0] for r in range(SLC_TOPK)], axis=1).astype(BF16)
        vt = jnp.concatenate([kvbuf_ref[slot, g * SLC_TOPK + r, 1] for r in range(SLC_TOPK)], axis=1).astype(BF16)
        vis = jnp.zeros((1, nk), jnp.int32)
        for r in range(SLC_TOPK):
            j = idx_ref[b, g * SLC_TOPK + r]
            half = jnp.where(j < new_blk, j % bpp, -1)
            vis = jnp.where(lane // page == r, jnp.where((lane % page) // SLC_BLOCK == half, 1, 0), vis)
        ok = jnp.broadcast_to(vis > 0, (hpg, nk))
        o_s = attend_with_new(qg, _dot(qg, kt), ok, vt, new_row(2, g), new_row(3, g))
        kwt = win_ref[0, 0, g * hd:(g + 1) * hd, :].astype(BF16)
        vwt = win_ref[0, 1, g * hd:(g + 1) * hd, :].astype(BF16)
        wpos = past_len - w_buf + lax.broadcasted_iota(jnp.int32, (hpg, w_buf), 1)
        okw = (wpos <= t_pos) & (wpos > t_pos - WINDOW) & (wpos >= 0)
        o_w = attend_with_new(qg, _dot(qg, kwt), okw, vwt, new_row(4, g), new_row(5, g))
        for h in range(hpg):
            col = (g * hpg + h) * hd
            gc = g * GATE_ROWS + h
            o_h = (gates[:, gc:gc + 1] * oc_ref[0, :, col:col + hd]
                   + gates[:, gc + hpg:gc + hpg + 1] * o_s[h:h + 1]
                   + gates[:, gc + 2 * hpg:gc + 2 * hpg + 1] * o_w[h:h + 1])
            o_ref[0, :, col:col + hd] = o_h


def nsa_sample_attend(proj, o_c, kv_new, win_t, cache_t, page_table, idx, t_pos, past_len):
    B = proj.shape[0]
    G, hd = NSA_KV_GROUPS, NSA_HEAD_DIM
    nq = NSA_HEADS * hd
    page = cache_t.shape[3]
    row = lambda a: pl.BlockSpec((1, 1, a.shape[2]), lambda b, pt, ix: (b, 0, 0))
    grid_spec = pltpu.PrefetchScalarGridSpec(
        num_scalar_prefetch=2,
        grid=(B,),
        in_specs=[row(proj), row(o_c), row(kv_new),
                  pl.BlockSpec((1,) + win_t.shape[1:], lambda b, pt, ix: (b, 0, 0, 0)),
                  pl.BlockSpec(memory_space=pl.ANY)],
        out_specs=pl.BlockSpec((1, 1, nq), lambda b, pt, ix: (b, 0, 0)),
        scratch_shapes=[pltpu.VMEM((2, G * SLC_TOPK, 2, hd, page), F32),
                        pltpu.SemaphoreType.DMA((2,))],
    )
    return pl.pallas_call(
        functools.partial(_nsa_sample_attend_kernel, t_pos=t_pos, past_len=past_len),
        grid_spec=grid_spec,
        out_shape=jax.ShapeDtypeStruct((B, 1, nq), F32),
        compiler_params=_cparams("arbitrary"),
        name="nsa_sample_attend",
    )(page_table, idx, proj, o_c, kv_new, win_t, cache_t)


def kernel(x_prompt, x_sample, cache_nsa_kv, cache_win_kv, state_hgrn, page_table, norm_mix, norm_mlp, w_mlp_up, w_mlp_down, w_hgrn_in, hgrn_lb_logits, hgrn_onorm, w_hgrn_out, norm_kv, w_kv, cmp_pe_k, cmp_w1_k, cmp_w2_k, cmp_pe_v, cmp_w1_v, cmp_w2_v, w_nsa_q, w_nsa_out, norm_final):
    B, T, D = x_prompt.shape
    Bs, Ts, _ = x_sample.shape
    G, hd = NSA_KV_GROUPS, NSA_HEAD_DIM
    n_pool, page = cache_nsa_kv.shape[:2]
    past_len = page_table.shape[1] * page
    w_buf = cache_win_kv.shape[1]
    assert Ts == 1 and T % 1024 == 0 and T >= WINDOW + 256 and past_len % SLC_BLOCK == 0 and w_buf <= past_len

    wab, wbig, pe, w2, w2t = _cmp_weights(cmp_pe_k, cmp_w1_k, cmp_pe_v, cmp_w1_v, cmp_w2_k, cmp_w2_v)
    wq = [_permute_gate_cols(w_nsa_q[l]) for l in range(DEPTH - N_A)]

    tm = 1024
    x = x_prompt.reshape(B * T, D)
    states_p = jnp.zeros((N_A, B, HG_HEADS, HG_K, HG_V), F32)
    for l in range(DEPTH):
        if l == N_A:
            nsa_p, win_p, k_hm, vt_hm = kv_proj_prompt(x, norm_kv, w_kv, B, T)
            kc_p, vct_p = compress_prompt(nsa_p, wbig, wab, pe, w2, w2t)
        if l < N_A:
            proj = rms_proj(x, norm_mix[l], w_hgrn_in, 2 * tm, 512, layer=l).reshape(B, T, -1)
            o, states_p = hgrn_prompt(proj, hgrn_lb_logits, hgrn_onorm[l], l, states_p)
            x = proj_res(o.reshape(B * T, -1), w_hgrn_out, x, tm, layer=l)
        else:
            q_hm, gates_t = q_proj_prompt(x, norm_mix[l], wq[l - N_A], B, T)
            o = nsa_prompt(q_hm, gates_t, kc_p, vct_p, k_hm, vt_hm)
            x = proj_res(o.reshape(B * T, -1), w_nsa_out, x, tm, layer=l - N_A)
        x = mlp_res(x, norm_mlp[l], w_mlp_up, w_mlp_down, norm_final, tm, 1024, l == DEPTH - 1, l)
    y_prompt = x.reshape(B, T, D)
    nsa_kv_prompt = nsa_p.reshape(B, 4, G, hd, T).transpose(0, 4, 1, 2, 3)
    win_kv_prompt = win_p.reshape(B, 2, G, hd, T)[..., -min(WINDOW, T):].transpose(0, 4, 1, 2, 3)

    t_pos = past_len
    n_slc = -(-(past_len + 1) // SLC_BLOCK)
    xs = x_sample.reshape(Bs, D)
    cache_t = cache_nsa_kv.transpose(0, 2, 3, 4, 1).reshape(n_pool, 4, G * hd, page)
    win_t = cache_win_kv.transpose(0, 2, 3, 4, 1).reshape(Bs, 2, G * hd, w_buf)
    states_s = jnp.zeros(state_hgrn.shape, F32)
    for l in range(DEPTH):
        if l == N_A:
            kv_s = rms_proj(xs, norm_kv, w_kv, Bs, 512)
            cmp_s = compress_sample(cache_t, page_table, wbig, wab, pe, w2)
        if l < N_A:
            proj = rms_proj(xs, norm_mix[l], w_hgrn_in, Bs, 512, layer=l).reshape(Bs, 1, -1)
            o, states_s = hgrn_step(proj, state_hgrn, hgrn_lb_logits, hgrn_onorm[l], l, states_s)
        else:
            proj = rms_proj(xs, norm_mix[l], wq[l - N_A], Bs, 384).reshape(Bs, 1, -1)
            o_c, idx = nsa_sample_select(proj, cmp_s, t_pos, n_slc)
            o = nsa_sample_attend(proj, o_c, kv_s.reshape(Bs, 1, -1), win_t, cache_t, page_table,
                                  idx.reshape(Bs, G * SLC_TOPK), t_pos, past_len)
        w_o, lo = (w_hgrn_out, l) if l < N_A else (w_nsa_out, l - N_A)
        xs = proj_res(o.reshape(Bs, -1), w_o, xs, Bs, layer=lo)
        xs = mlp_res(xs, norm_mlp[l], w_mlp_up, w_mlp_down, norm_final, Bs, 512, l == DEPTH - 1, l)
    y_sample = xs.reshape(Bs, 1, D)
    n_nsa = 4 * G * hd
    nsa_kv_sample = kv_s[:, :n_nsa].reshape(Bs, 1, 4, G, hd)
    win_new = kv_s[:, n_nsa:].reshape(Bs, 1, 2, G, hd).astype(cache_win_kv.dtype)
    win_kv_sample = jnp.concatenate([cache_win_kv, win_new], axis=1)[:, -w_buf:]

    return (y_prompt, y_sample, nsa_kv_prompt, nsa_kv_sample, win_kv_prompt, win_kv_sample,
            states_p, states_s)
```

```python
import functools

import jax
import jax.numpy as jnp
from jax import lax
from jax.experimental import pallas as pl
from jax.experimental.pallas import tpu as pltpu

F32 = jnp.float32
BF16 = jnp.bfloat16

D_MODEL = 1024
DEPTH = 4
N_A = DEPTH // 2
D_FF = 4 * D_MODEL
RMS_EPS = 1e-6
HG_HEADS = 8
HG_K = 128
HG_V = 128
NSA_HEADS = 16
NSA_HEAD_DIM = 64
NSA_KV_GROUPS = 4
NSA_HPG = NSA_HEADS // NSA_KV_GROUPS
NSA_SCALE = NSA_HEAD_DIM ** -0.5
CMP_LEN = 32
CMP_STRIDE = 16
SLC_BLOCK = 64
SLC_TOPK = 16
WINDOW = 512
FORCED_SCORE = 1e4
NEG_INF = -1e30
TINY = 1e-30

HG_CHUNK = 128
HG_MATRIX_LEVELS = 2
VMEM_LIMIT = 56 * 1024 * 1024


def _cparams(*sem):
    return pltpu.CompilerParams(dimension_semantics=sem, vmem_limit_bytes=VMEM_LIMIT)


def _rms(x, g):
    return x * lax.rsqrt(jnp.mean(x * x, axis=-1, keepdims=True) + RMS_EPS) * g


def _sigmoid(x):
    return 1.0 / (1.0 + jnp.exp(-x))


def _dot(a, b):
    return jnp.dot(a, b, preferred_element_type=F32)


def _dot_nt(a, b):
    return lax.dot_general(a, b, (((1,), (1,)), ((), ())), preferred_element_type=F32)


def _dot_tn(a, b):
    return lax.dot_general(a, b, (((0,), (0,)), ((), ())), preferred_element_type=F32)


def _split3(x):
    hi = x.astype(BF16)
    r1 = x - hi.astype(F32)
    mid = r1.astype(BF16)
    lo = (r1 - mid.astype(F32)).astype(BF16)
    return hi, mid, lo


def _masked_softmax(s, mask):
    s = jnp.where(mask, s, NEG_INF)
    e = jnp.where(mask, jnp.exp(s - jnp.max(s, axis=-1, keepdims=True)), 0.0)
    return e / jnp.maximum(jnp.sum(e, axis=-1, keepdims=True), TINY)


def _rms_proj_kernel(x_ref, g_ref, w_ref, o_ref, y_ref):
    @pl.when(pl.program_id(1) == 0)
    def _():
        y_ref[...] = _rms(x_ref[...], g_ref[...]).astype(BF16)

    o_ref[...] = _dot(y_ref[...], w_ref[...].astype(BF16)).astype(o_ref.dtype)


def _stacked(w, layer):
    return (w[None], 0) if w.ndim == 2 else (w, layer)


def rms_proj(x, g, w, tm, tn, out_dtype=F32, layer=0):
    M, D = x.shape
    w, layer = _stacked(w, layer)
    N = w.shape[2]
    return pl.pallas_call(
        _rms_proj_kernel,
        grid=(M // tm, N // tn),
        in_specs=[pl.BlockSpec((tm, D), lambda i, j: (i, 0)),
                  pl.BlockSpec((1, D), lambda i, j: (0, 0)),
                  pl.BlockSpec((None, D, tn), lambda i, j: (layer, 0, j))],
        out_specs=pl.BlockSpec((tm, tn), lambda i, j: (i, j)),
        out_shape=jax.ShapeDtypeStruct((M, N), out_dtype),
        scratch_shapes=[pltpu.VMEM((tm, D), BF16)],
        compiler_params=_cparams("parallel", "arbitrary"),
        name="rms_proj",
    )(x, g.reshape(1, D), w)


def _proj_res_kernel(a_ref, w_ref, r_ref, o_ref, wb_ref):
    @pl.when(pl.program_id(0) == 0)
    def _():
        wb_ref[...] = w_ref[...].astype(BF16)

    o_ref[...] = r_ref[...] + _dot(a_ref[...].astype(BF16), wb_ref[...])


def proj_res(a, w, res, tm, layer=0):
    M, K = a.shape
    w, layer = _stacked(w, layer)
    N = w.shape[2]
    return pl.pallas_call(
        _proj_res_kernel,
        grid=(M // tm,),
        in_specs=[pl.BlockSpec((tm, K), lambda i: (i, 0)),
                  pl.BlockSpec((None, K, N), lambda i: (layer, 0, 0)),
                  pl.BlockSpec((tm, N), lambda i: (i, 0))],
        out_specs=pl.BlockSpec((tm, N), lambda i: (i, 0)),
        out_shape=jax.ShapeDtypeStruct((M, N), F32),
        scratch_shapes=[pltpu.VMEM((K, N), BF16)],
        compiler_params=_cparams("arbitrary"),
        name="proj_res",
    )(a, w, res)


def _mlp_kernel(x_ref, g_ref, wu_ref, wd_ref, gf_ref, o_ref, y_ref, acc_ref, *, final_norm):
    f = pl.program_id(1)

    @pl.when(f == 0)
    def _():
        y_ref[...] = _rms(x_ref[...], g_ref[...]).astype(BF16)
        acc_ref[...] = jnp.zeros_like(acc_ref)

    h = jnp.maximum(_dot(y_ref[...], wu_ref[...].astype(BF16)), 0.0)
    acc_ref[...] += _dot((h * h).astype(BF16), wd_ref[...].astype(BF16))

    @pl.when(f == pl.num_programs(1) - 1)
    def _():
        out = x_ref[...] + acc_ref[...]
        if final_norm:
            out = _rms(out, gf_ref[...])
        o_ref[...] = out


def mlp_res(x, g, w_up, w_down, g_final, tm, tf, final_norm, layer):
    M, D = x.shape
    Fdim = w_up.shape[2]
    return pl.pallas_call(
        functools.partial(_mlp_kernel, final_norm=final_norm),
        grid=(M // tm, Fdim // tf),
        in_specs=[pl.BlockSpec((tm, D), lambda i, f: (i, 0)),
                  pl.BlockSpec((1, D), lambda i, f: (0, 0)),
                  pl.BlockSpec((None, D, tf), lambda i, f: (layer, 0, f)),
                  pl.BlockSpec((None, tf, D), lambda i, f: (layer, f, 0)),
                  pl.BlockSpec((1, D), lambda i, f: (0, 0))],
        out_specs=pl.BlockSpec((tm, D), lambda i, f: (i, 0)),
        out_shape=jax.ShapeDtypeStruct((M, D), F32),
        scratch_shapes=[pltpu.VMEM((tm, D), BF16), pltpu.VMEM((tm, D), F32)],
        compiler_params=_cparams("parallel", "arbitrary"),
        name="mlp_res",
    )(x, g.reshape(1, D), w_up, w_down, g_final.reshape(1, D))


def _hgrn_lower_bound(lg, layer):
    m = jnp.max(lg, axis=0, keepdims=True)
    e = jnp.exp(lg - m)
    p = e / jnp.sum(e, axis=0, keepdims=True)
    lb = jnp.sum(p[1:layer + 1], axis=0, keepdims=True)
    return jnp.log(lb), jnp.log(1.0 - lb)


def _hgrn_logf(z, lg, layer):
    ls = jnp.minimum(z, 0.0) - jnp.log(1.0 + jnp.exp(-jnp.abs(z)))
    if layer == 0:
        return ls
    log_lb, log1m = _hgrn_lower_bound(lg, layer)
    b2 = log1m + ls
    return jnp.maximum(log_lb, b2) + jnp.log(1.0 + jnp.exp(-jnp.abs(log_lb - b2)))


def _hgrn_sum_matrices(C, n_lev):
    r = lax.broadcasted_iota(jnp.int32, (C, C), 0)
    u = lax.broadcasted_iota(jnp.int32, (C, C), 1)
    mats = [r >= u]
    for lev in range(n_lev):
        h = 1 << lev
        off = r & (2 * h - 1)
        mid = r - off + h
        mats.append(((off >= h) & (u >= mid) & (u <= r)) | ((off < h) & (u > r) & (u < mid)))
    return jnp.concatenate([jnp.where(m, 1.0, 0.0).astype(BF16) for m in mats], axis=0)


def _hgrn_kernel(zq_ref, zf_ref, zi_ref, zg_ref, lg_ref, on_ref, _states_in, o_ref, s_ref, st_ref, w_ref, b_ref,
                 *, layer, tc, nh):
    t = pl.program_id(2)
    C = HG_CHUNK
    n_lev = C.bit_length() - 1
    heads = range(nh)

    @pl.when(t == 0)
    def _():
        st_ref[...] = jnp.zeros_like(st_ref)
        w_ref[...] = _hgrn_sum_matrices(C, HG_MATRIX_LEVELS)

    r_i = lax.broadcasted_iota(jnp.int32, (C, C), 0)
    c_i = lax.broadcasted_iota(jnp.int32, (C, C), 1)
    row = lax.broadcasted_iota(jnp.int32, (C, HG_K), 0)

    def chunk(ci, sts):
        rows = pl.ds(pl.multiple_of(ci * C, C), C)
        kl = [slice(h * HG_K, (h + 1) * HG_K) for h in heads]
        vl = [slice(h * HG_V, (h + 1) * HG_V) for h in heads]
        q = [zq_ref[0, rows, kl[h]] for h in heads]
        vb = [zi_ref[0, rows, vl[h]].astype(BF16) for h in heads]
        logf = [_hgrn_logf(zf_ref[0, rows, kl[h]], lg_ref[:, kl[h]], layer) for h in heads]
        k = [1.0 - jnp.exp(logf[h]) for h in heads]
        parts = [_split3(logf[h]) for h in heads]
        w_sum, w_lev = w_ref[0:C, :], w_ref[C:, :]
        b = [_dot(w_sum, parts[h][0]) + _dot(w_sum, parts[h][1]) + _dot(w_sum, parts[h][2]) for h in heads]
        for h in heads:
            b_ref[h] = b[h]
        e_low = [jnp.exp(_dot(w_lev, parts[h][0]) + _dot(w_lev, parts[h][1])) for h in heads]
        a = [jnp.where(r_i == c_i, _dot_nt(q[h].astype(BF16), k[h].astype(BF16)), 0.0) for h in heads]
        for lev in range(n_lev):
            half = 1 << lev
            upper = (row & (2 * half - 1)) >= half
            same = (r_i >> (lev + 1)) == (c_i >> (lev + 1))
            if lev < HG_MATRIX_LEVELS:
                e = [e_low[h][lev * C:(lev + 1) * C] for h in heads]
            else:
                nblk = C // (2 * half)
                e = []
                for h in heads:
                    bm = b_ref[h, pl.ds(half - 1, nblk, stride=2 * half), :] if nblk > 1 else b_ref[h, half - 1:half, :]
                    bm = jnp.broadcast_to(bm[:, None, :], (nblk, 2 * half, HG_K)).reshape(C, HG_K)
                    e.append(jnp.exp(jnp.where(upper, b[h] - bm, bm - b[h])))
            qt = [jnp.where(upper, q[h] * e[h], 0.0).astype(BF16) for h in heads]
            kt = [jnp.where(upper, 0.0, k[h] * e[h]).astype(BF16) for h in heads]
            al = [_dot_nt(qt[h], kt[h]) for h in heads]
            a = [a[h] + (jnp.where(same, al[h], 0.0) if 2 * half < C else al[h]) for h in heads]
        qd = [(q[h] * jnp.exp(b[h])).astype(BF16) for h in heads]
        o = [_dot(a[h].astype(BF16), vb[h]) + _dot_nt(qd[h], sts[h].astype(BF16)) for h in heads]
        b_end = [b[h][C - 1:C] for h in heads]
        kd = [(k[h] * jnp.exp(b_end[h] - b[h])).astype(BF16) for h in heads]
        new = tuple(sts[h] * jnp.exp(b_end[h]) + _dot_tn(vb[h], kd[h]) for h in heads)
        for h in heads:
            oh = o[h] * lax.rsqrt(jnp.mean(o[h] * o[h], axis=-1, keepdims=True) + RMS_EPS) * on_ref[:, vl[h]]
            zg = zg_ref[0, rows, vl[h]]
            o_ref[0, rows, vl[h]] = (oh * (zg * _sigmoid(zg))).astype(o_ref.dtype)
        return new

    sts = lax.fori_loop(0, tc // C, chunk, tuple(st_ref[h] for h in heads))
    for h in heads:
        st_ref[h] = sts[h]

    @pl.when(t == pl.num_programs(2) - 1)
    def _():
        for h in heads:
            s_ref[0, 0, h] = st_ref[h].T


def hgrn_prompt(proj, lb_logits, onorm, layer, states, tc=512, nh=8):
    B, T, _ = proj.shape
    H = HG_HEADS
    hp = H // nh
    n_mats = 1 + HG_MATRIX_LEVELS
    alias_spec, alias_arg, aliases = [pl.BlockSpec(memory_space=pl.ANY)], [states], {6: 1}
    return pl.pallas_call(
        functools.partial(_hgrn_kernel, layer=layer, tc=tc, nh=nh),
        grid=(B, hp, T // tc),
        in_specs=[pl.BlockSpec((1, tc, nh * HG_K), lambda b, h, t: (b, t, h)),
                  pl.BlockSpec((1, tc, nh * HG_K), lambda b, h, t: (b, t, hp + h)),
                  pl.BlockSpec((1, tc, nh * HG_V), lambda b, h, t: (b, t, 2 * hp + h)),
                  pl.BlockSpec((1, tc, nh * HG_V), lambda b, h, t: (b, t, 3 * hp + h)),
                  pl.BlockSpec((N_A, nh * HG_K), lambda b, h, t: (0, h)),
                  pl.BlockSpec((1, nh * HG_V), lambda b, h, t: (0, h))] + alias_spec,
        out_specs=[pl.BlockSpec((1, tc, nh * HG_V), lambda b, h, t: (b, t, h)),
                   pl.BlockSpec((1, 1, nh, HG_K, HG_V), lambda b, h, t: (layer, b, h, 0, 0))],
        out_shape=[jax.ShapeDtypeStruct((B, T, H * HG_V), BF16),
                   jax.ShapeDtypeStruct((N_A, B, H, HG_K, HG_V), F32)],
        input_output_aliases=aliases,
        scratch_shapes=[pltpu.VMEM((nh, HG_V, HG_K), F32),
                        pltpu.VMEM((n_mats * HG_CHUNK, HG_CHUNK), BF16),
                        pltpu.VMEM((nh, HG_CHUNK, HG_K), F32)],
        compiler_params=_cparams("parallel", "parallel", "arbitrary"),
        name="hgrn_prompt",
    )(proj, proj, proj, proj, lb_logits, onorm.reshape(1, H * HG_V), *alias_arg)


QK_LANES = 2 * NSA_HEAD_DIM
VT_ROWS = NSA_HEAD_DIM + 16


def _kv_proj_kernel(x_ref, g_ref, w_ref, nsa_ref, win_ref, k_ref, vt_ref, wb_ref, *, tpb):
    @pl.when(pl.program_id(0) == 0)
    def _():
        wb_ref[...] = w_ref[...].astype(BF16)

    G, hd = NSA_KV_GROUPS, NSA_HEAD_DIM
    tm = x_ref.shape[0]
    y = _rms(x_ref[...], g_ref[...]).astype(BF16)
    kv = _dot(y, wb_ref[...])
    n_nsa = nsa_ref.shape[1]
    cols_t = [kv[:, c:c + 2 * hd].T for c in range(0, kv.shape[1], 2 * hd)]
    for n, t in enumerate(cols_t):
        c = n * 2 * hd
        if c < n_nsa:
            nsa_ref[0, c:c + 2 * hd, :] = t
        else:
            win_ref[0, c - n_nsa:c - n_nsa + 2 * hd, :] = t
    lane = lax.broadcasted_iota(jnp.int32, (tm, QK_LANES), 1)
    blk = ((pl.program_id(0) % tpb) * tm + lax.broadcasted_iota(jnp.int32, (tm, QK_LANES), 0)) // SLC_BLOCK
    tails = (jnp.where(lane - hd == blk, 1.0, 0.0), jnp.zeros((tm, QK_LANES), F32))
    for n, kind in enumerate((2, 4)):
        for gp in range(G // 2):
            col = (kind * G + 2 * gp) * hd
            pair = kv[:, col:col + 2 * hd]
            for gl, src in enumerate((pair, pltpu.roll(pair, hd, axis=1))):
                k_ref[0, n * G + 2 * gp + gl] = jnp.where(lane < hd, src, tails[n]).astype(BF16)
    ones_row = jnp.where(lax.broadcasted_iota(jnp.int32, (VT_ROWS - hd, tm), 0) == 0, 1.0, 0.0).astype(BF16)
    for n, kind in enumerate((3, 5)):
        for gp in range(G // 2):
            t = cols_t[(kind * G + 2 * gp) * hd // (2 * hd)]
            for gl in range(2):
                vt_ref[0, n * G + 2 * gp + gl, 0:hd, :] = t[gl * hd:(gl + 1) * hd].astype(BF16)
                vt_ref[0, n * G + 2 * gp + gl, hd:, :] = ones_row


def kv_proj_prompt(x, g, w_kv, B, T, tm=512):
    M, D = x.shape
    N = w_kv.shape[1]
    G, hd = NSA_KV_GROUPS, NSA_HEAD_DIM
    n_nsa = 4 * G * hd
    tpb = T // tm
    assert T // SLC_BLOCK <= QK_LANES - hd
    return pl.pallas_call(
        functools.partial(_kv_proj_kernel, tpb=tpb),
        grid=(M // tm,),
        in_specs=[pl.BlockSpec((tm, D), lambda i: (i, 0)),
                  pl.BlockSpec((1, D), lambda i: (0, 0)),
                  pl.BlockSpec((D, N), lambda i: (0, 0))],
        out_specs=[pl.BlockSpec((1, n_nsa, tm), lambda i: (i // tpb, 0, i % tpb)),
                   pl.BlockSpec((1, N - n_nsa, tm), lambda i: (i // tpb, 0, i % tpb)),
                   pl.BlockSpec((1, 2 * G, tm, QK_LANES), lambda i: (i // tpb, 0, i % tpb, 0)),
                   pl.BlockSpec((1, 2 * G, VT_ROWS, tm), lambda i: (i // tpb, 0, 0, i % tpb))],
        out_shape=[jax.ShapeDtypeStruct((B, n_nsa, T), F32),
                   jax.ShapeDtypeStruct((B, N - n_nsa, T), F32),
                   jax.ShapeDtypeStruct((B, 2 * G, T, QK_LANES), BF16),
                   jax.ShapeDtypeStruct((B, 2 * G, VT_ROWS, T), BF16)],
        scratch_shapes=[pltpu.VMEM((D, N), BF16)],
        compiler_params=_cparams("arbitrary"),
        name="kv_proj",
    )(x, g.reshape(1, D), w_kv)


GATE_ROWS = 16
GATE_LANES = 128


def _q_proj_kernel(x_ref, g_ref, w_ref, q_ref, gt_ref, wb_ref):
    @pl.when(pl.program_id(0) == 0)
    def _():
        wb_ref[...] = w_ref[...].astype(BF16)

    y = _rms(x_ref[...], g_ref[...]).astype(BF16)
    pr = _dot(y, wb_ref[...])
    hd = NSA_HEAD_DIM
    nq = NSA_HEADS * hd
    low = lax.broadcasted_iota(jnp.int32, (pr.shape[0], QK_LANES), 1) < hd
    for hp in range(NSA_HEADS // 2):
        pair = pr[:, hp * 2 * hd:(hp + 1) * 2 * hd] * NSA_SCALE
        for hl, src in enumerate((pair, pltpu.roll(pair, hd, axis=1))):
            q_ref[0, 2 * hp + hl] = jnp.where(low, src, 0.0).astype(BF16)
    gates_t = _sigmoid(pr[:, nq:]).T
    for gi in range(NSA_KV_GROUPS):
        gt_ref[0, gi] = gates_t[gi * GATE_ROWS:(gi + 1) * GATE_ROWS]


def _permute_gate_cols(w_q):
    nq = NSA_HEADS * NSA_HEAD_DIM
    d = w_q.shape[0]
    wg = w_q[:, nq:].reshape(d, 3, NSA_KV_GROUPS, NSA_HPG).transpose(0, 2, 1, 3).reshape(d, NSA_KV_GROUPS, 3 * NSA_HPG)
    wg = jnp.pad(wg, ((0, 0), (0, 0), (0, GATE_ROWS - 3 * NSA_HPG))).reshape(d, NSA_KV_GROUPS * GATE_ROWS)
    wg = jnp.pad(wg, ((0, 0), (0, GATE_LANES - NSA_KV_GROUPS * GATE_ROWS)))
    return jnp.concatenate([w_q[:, :nq], wg], axis=1)


def q_proj_prompt(x, g, w_qp, B, T, tm=512):
    M, D = x.shape
    N = w_qp.shape[1]
    tpb = T // tm
    return pl.pallas_call(
        _q_proj_kernel,
        grid=(M // tm,),
        in_specs=[pl.BlockSpec((tm, D), lambda i: (i, 0)),
                  pl.BlockSpec((1, D), lambda i: (0, 0)),
                  pl.BlockSpec((D, N), lambda i: (0, 0))],
        out_specs=[pl.BlockSpec((1, NSA_HEADS, tm, QK_LANES), lambda i: (i // tpb, 0, i % tpb, 0)),
                   pl.BlockSpec((1, NSA_KV_GROUPS, GATE_ROWS, tm), lambda i: (i // tpb, 0, 0, i % tpb))],
        out_shape=[jax.ShapeDtypeStruct((B, NSA_HEADS, T, QK_LANES), BF16),
                   jax.ShapeDtypeStruct((B, NSA_KV_GROUPS, GATE_ROWS, T), F32)],
        scratch_shapes=[pltpu.VMEM((D, N), BF16)],
        compiler_params=_cparams("arbitrary"),
        name="q_proj",
    )(x, g.reshape(1, D), w_qp)


def _cmp_weights(pe_k, w1_k, pe_v, w1_v, w2_k, w2_v):
    half = CMP_STRIDE * NSA_HEAD_DIM

    def ab(w1):
        return jnp.concatenate([w1[:half], w1[half:]], axis=1)

    def big(w1):
        w = w1.reshape(2, CMP_STRIDE, NSA_HEAD_DIM, -1)
        b = jnp.einsum("alds,gh->lgdhas", w, jnp.eye(2, dtype=w1.dtype))
        return b.reshape(CMP_STRIDE * 2 * NSA_HEAD_DIM, 2 * 2 * w.shape[-1])

    wab = jnp.stack([ab(w1_k), ab(w1_v)])
    wbig = jnp.stack([big(w1_k), big(w1_v)])
    pe = jnp.stack([pe_k.reshape(2, half), pe_v.reshape(2, half)])
    w2 = jnp.stack([w2_k, w2_v])
    w2t = jnp.stack([w2_k.T, w2_v.T])
    return wab, wbig, pe, w2, w2t


def _cmp_taps(x_ref, ns):
    return jnp.concatenate([x_ref[pl.ds(l, ns, stride=CMP_STRIDE), :].astype(BF16) for l in range(CMP_STRIDE)], axis=1)


def _cmp_hidden(ab, pe, wab):
    hd = NSA_HEAD_DIM
    n = ab.shape[0]
    nxt = pltpu.roll(ab, n - 1, axis=0)
    pt = _dot(pe.astype(BF16), wab)
    hid = ab[:, :hd] + nxt[:, hd:] + pt[0:1, :hd] + pt[1:2, hd:]
    return hid * _sigmoid(hid)


def _cmp_prompt_kernel(xk_ref, xv_ref, wbig_ref, wab_ref, pe_ref, w2_ref, w2t_ref, kc_ref, vct_ref, buf_ref):
    hd = NSA_HEAD_DIM
    ns = xk_ref.shape[2] // CMP_STRIDE
    for c, x_ref in enumerate((xk_ref, xv_ref)):
        wab = wab_ref[c].astype(BF16)
        buf_ref[...] = x_ref[0].T
        ab2 = _dot(_cmp_taps(buf_ref, ns), wbig_ref[c].astype(BF16))
        for gl in range(2):
            act = _cmp_hidden(ab2[:, gl * 2 * hd:(gl + 1) * 2 * hd], pe_ref[c], wab).astype(BF16)
            if c == 0:
                kc_ref[0, gl] = _dot(act, w2_ref[c].astype(BF16)).astype(kc_ref.dtype)
            else:
                vct_ref[0, gl] = _dot_nt(w2t_ref[c].astype(BF16), act).astype(vct_ref.dtype)


def compress_prompt(nsa_t, wbig, wab, pe, w2, w2t):
    B, _, T = nsa_t.shape
    G, hd = NSA_KV_GROUPS, NSA_HEAD_DIM
    ns = T // CMP_STRIDE
    w2 = jnp.pad(w2, ((0, 0), (0, 0), (0, QK_LANES - hd)))
    const = lambda a: pl.BlockSpec(a.shape, lambda b, gp: (0,) * a.ndim)
    return pl.pallas_call(
        _cmp_prompt_kernel,
        grid=(B, G // 2),
        in_specs=[pl.BlockSpec((1, 2 * hd, T), lambda b, gp: (b, gp, 0)),
                  pl.BlockSpec((1, 2 * hd, T), lambda b, gp: (b, G // 2 + gp, 0)),
                  const(wbig), const(wab), const(pe), const(w2), const(w2t)],
        out_specs=[pl.BlockSpec((1, 2, ns, QK_LANES), lambda b, gp: (b, gp, 0, 0)),
                   pl.BlockSpec((1, 2, hd, ns), lambda b, gp: (b, gp, 0, 0))],
        out_shape=[jax.ShapeDtypeStruct((B, G, ns, QK_LANES), BF16),
                   jax.ShapeDtypeStruct((B, G, hd, ns), BF16)],
        scratch_shapes=[pltpu.VMEM((T, 2 * hd), F32)],
        compiler_params=_cparams("parallel", "parallel"),
        name="compress_prompt",
    )(nsa_t, nsa_t, wbig, wab, pe, w2, w2t)


def _nsa_prompt_kernel(q_ref, gt_ref, kc_ref, vct_ref, ks_ref, vst_ref, kw_ref, vwt_ref, o_ref, *, tq):
    i = pl.program_id(2)
    s0 = i * tq
    hpg, hd = NSA_HPG, NSA_HEAD_DIM
    T = ks_ref.shape[2]
    n_cmp = kc_ref.shape[2]
    n_slc = T // SLC_BLOCK
    R = hpg * tq
    Q = q_ref[0].reshape(R, QK_LANES)
    tpos = s0 + lax.broadcasted_iota(jnp.int32, (1, R), 1) % tq

    sc = _dot_nt(kc_ref[0, 0], Q)
    ok_c = lax.broadcasted_iota(jnp.int32, (n_cmp, R), 0) * CMP_STRIDE + (CMP_LEN - 1) <= tpos
    sc = jnp.where(ok_c, sc, NEG_INF)
    e_c = jnp.where(ok_c, jnp.exp(sc - jnp.max(sc, axis=0, keepdims=True)), 0.0)
    p_c = e_c / jnp.maximum(jnp.sum(e_c, axis=0, keepdims=True), TINY)
    o_c = _dot(vct_ref[0, 0], p_c.astype(BF16))

    psum = p_c[:, 0:tq]
    for h in range(1, hpg):
        psum = psum + p_c[:, h * tq:(h + 1) * tq]
    sj = lax.broadcasted_iota(jnp.int32, (n_slc, n_cmp), 0) * SLC_BLOCK
    ci = lax.broadcasted_iota(jnp.int32, (n_slc, n_cmp), 1) * CMP_STRIDE
    ov = jnp.where((ci < sj + SLC_BLOCK) & (ci + CMP_LEN > sj), 1.0, 0.0).astype(BF16)
    hi, mid, lo = _split3(psum)
    imp = _dot(ov, hi) + _dot(ov, mid) + _dot(ov, lo)
    blk = lax.broadcasted_iota(jnp.int32, (n_slc, tq), 0)
    qpos = s0 + lax.broadcasted_iota(jnp.int32, (n_slc, tq), 1)
    cur = qpos // SLC_BLOCK
    forced = (blk == 0) | (blk == cur) | (blk == cur - 1)
    score = jnp.where(forced, FORCED_SCORE, jnp.where(blk <= cur, imp, -1.0))
    rank = jnp.zeros((n_slc, tq), F32)
    for j in range(n_slc):
        cj = score[j:j + 1, :]
        rank = rank + jnp.where((cj > score) | ((cj == score) & (blk > j)), 1.0, 0.0)
    bias = jnp.where(rank < SLC_TOPK, 0.0, NEG_INF)
    bias = jnp.concatenate([jnp.zeros((hd, tq), F32), bias, jnp.zeros((QK_LANES - hd - n_slc, tq), F32)], axis=0)
    bias_t = bias.T.astype(BF16)
    q_sel = Q + jnp.concatenate([bias_t] * hpg, axis=0)

    heads = range(hpg)
    q_heads = [q_sel[h * tq:(h + 1) * tq] for h in heads]
    tpos_h = tpos[:, :tq]

    def chunk(c, carry, diagonal):
        k0 = pl.multiple_of(c * tq, tq)
        ks = ks_ref[0, 0, pl.ds(k0, tq), :]
        vst = vst_ref[0, 0, :, pl.ds(k0, tq)]
        s_heads = [_dot_nt(ks, q_heads[h]) for h in heads]
        new = []
        for h in heads:
            m, acc = carry[h]
            s = s_heads[h]
            if diagonal:
                s = jnp.where(k0 + lax.broadcasted_iota(jnp.int32, (tq, tq), 0) <= tpos_h, s, NEG_INF)
            m_new = jnp.maximum(m, jnp.max(s, axis=0, keepdims=True))
            p = jnp.exp(s - m_new)
            new.append((m_new, jnp.exp(m - m_new) * acc + _dot(vst, p.astype(BF16))))
        return tuple(new)

    carry = tuple((jnp.full((1, tq), NEG_INF, F32), jnp.zeros((VT_ROWS, tq), F32)) for _ in heads)
    carry = lax.fori_loop(0, i, lambda c, cr: chunk(c, cr, False), carry)
    carry = chunk(i, carry, True)
    o_s = jnp.concatenate([acc[:hd] / jnp.maximum(acc[hd:hd + 1], TINY) for _, acc in carry], axis=1)

    ws = pl.multiple_of(jnp.maximum(s0 - WINDOW, 0), tq)
    parts = []
    for j in range(WINDOW // tq + 1):
        k0 = pl.multiple_of(ws + j * tq, tq)
        s = _dot_nt(kw_ref[0, 0, pl.ds(k0, tq), :], Q)
        wpos = k0 + lax.broadcasted_iota(jnp.int32, (tq, R), 0)
        ok = (wpos <= tpos) & (wpos > tpos - WINDOW) if j == 0 else wpos <= tpos
        parts.append((k0, jnp.where(ok, s, NEG_INF)))
    m_w = functools.reduce(jnp.maximum, [jnp.max(s, axis=0, keepdims=True) for _, s in parts])
    acc_w = sum(_dot(vwt_ref[0, 0, :, pl.ds(k0, tq)], jnp.exp(s - m_w).astype(BF16)) for k0, s in parts)
    o_w = acc_w[:hd] / jnp.maximum(acc_w[hd:hd + 1], TINY)

    gt = gt_ref[0, 0]
    outs = []
    for h in range(hpg):
        cols = slice(h * tq, (h + 1) * tq)
        outs.append(gt[h:h + 1] * o_c[:, cols] + gt[hpg + h:hpg + h + 1] * o_s[:, cols]
                    + gt[2 * hpg + h:2 * hpg + h + 1] * o_w[:, cols])
    for pair in range(hpg // 2):
        both = jnp.concatenate(outs[2 * pair:2 * pair + 2], axis=0)
        o_ref[0, :, pair * 2 * hd:(pair + 1) * 2 * hd] = both.T.astype(o_ref.dtype)


def nsa_prompt(q_hm, gates_t, kc, vct, k_hm, vt_hm, tq=512):
    B, _, T, _ = q_hm.shape
    G, hpg, hd = NSA_KV_GROUPS, NSA_HPG, NSA_HEAD_DIM
    n_cmp = kc.shape[2]
    assert WINDOW % tq == 0 and T >= WINDOW + tq
    keys = lambda off: pl.BlockSpec((1, 1, T, QK_LANES), lambda b, g, i: (b, off + g, 0, 0))
    vals = lambda off: pl.BlockSpec((1, 1, VT_ROWS, T), lambda b, g, i: (b, off + g, 0, 0))
    return pl.pallas_call(
        functools.partial(_nsa_prompt_kernel, tq=tq),
        grid=(B, G, T // tq),
        in_specs=[pl.BlockSpec((1, hpg, tq, QK_LANES), lambda b, g, i: (b, g, i, 0)),
                  pl.BlockSpec((1, 1, GATE_ROWS, tq), lambda b, g, i: (b, g, 0, i)),
                  pl.BlockSpec((1, 1, n_cmp, QK_LANES), lambda b, g, i: (b, g, 0, 0)),
                  pl.BlockSpec((1, 1, hd, n_cmp), lambda b, g, i: (b, g, 0, 0)),
                  keys(0), vals(0), keys(G), vals(G)],
        out_specs=pl.BlockSpec((1, tq, hpg * hd), lambda b, g, i: (b, i, g)),
        out_shape=jax.ShapeDtypeStruct((B, T, NSA_HEADS * hd), BF16),
        compiler_params=_cparams("parallel", "parallel", "arbitrary"),
        name="nsa_prompt",
    )(q_hm, gates_t, kc, vct, k_hm, vt_hm, k_hm, vt_hm)


def _row_to_col(row):
    n = row.shape[1]
    eye = lax.broadcasted_iota(jnp.int32, (n, n), 0) == lax.broadcasted_iota(jnp.int32, (n, n), 1)
    return jnp.sum(jnp.where(eye, jnp.broadcast_to(row, (n, n)), 0.0), axis=-1, keepdims=True)


def _hgrn_step_kernel(z_ref, s0_ref, lg_ref, on_ref, _states_in, o_ref, s_ref, *, layer):
    s0_ref, s_ref = s0_ref.at[0], s_ref.at[0]
    hk = HG_HEADS * HG_K
    hv = HG_HEADS * HG_V
    for h in range(HG_HEADS):
        kl = slice(h * HG_K, (h + 1) * HG_K)
        vl = slice(h * HG_V, (h + 1) * HG_V)
        q = z_ref[0, :, kl]
        logf = _hgrn_logf(z_ref[0, :, hk + h * HG_K:hk + (h + 1) * HG_K], lg_ref[:, kl], layer)
        f = jnp.exp(logf)
        v = z_ref[0, :, 2 * hk + h * HG_V:2 * hk + (h + 1) * HG_V]
        zg = z_ref[0, :, 2 * hk + hv + h * HG_V:2 * hk + hv + (h + 1) * HG_V]
        s = _row_to_col(f) * s0_ref[0, h] + _row_to_col(1.0 - f) * v
        s_ref[0, h] = s
        o = jnp.sum(_row_to_col(q) * s, axis=0, keepdims=True)
        o = o * lax.rsqrt(jnp.mean(o * o, axis=-1, keepdims=True) + RMS_EPS) * on_ref[:, vl]
        o_ref[0, :, vl] = o * (zg * _sigmoid(zg))


def hgrn_step(proj, s0_all, lb_logits, onorm, layer, states):
    B = proj.shape[0]
    H = HG_HEADS
    alias_spec, alias_arg, aliases = [pl.BlockSpec(memory_space=pl.ANY)], [states], {4: 1}
    state_block = pl.BlockSpec((1, 1, H, HG_K, HG_V), lambda b: (layer, b, 0, 0, 0))
    return pl.pallas_call(
        functools.partial(_hgrn_step_kernel, layer=layer),
        grid=(B,),
        in_specs=[pl.BlockSpec((1, 1, proj.shape[2]), lambda b: (b, 0, 0)),
                  state_block,
                  pl.BlockSpec((N_A, H * HG_K), lambda b: (0, 0)),
                  pl.BlockSpec((1, H * HG_V), lambda b: (0, 0))] + alias_spec,
        out_specs=[pl.BlockSpec((1, 1, H * HG_V), lambda b: (b, 0, 0)), state_block],
        out_shape=[jax.ShapeDtypeStruct((B, 1, H * HG_V), F32),
                   jax.ShapeDtypeStruct(s0_all.shape, F32)],
        input_output_aliases=aliases,
        compiler_params=_cparams("parallel"),
        name="hgrn_step",
    )(proj, s0_all, lb_logits, onorm.reshape(1, H * HG_V), *alias_arg)


def _cmp_sample_kernel(pt_ref, cache_ref, wbig_ref, wab_ref, pe_ref, w2_ref, o_ref, raw_ref, buf_ref, wb_ref,
                       sem, *, n_pages):
    b = pl.program_id(0)
    nb = pl.num_programs(0)
    G, hd = NSA_KV_GROUPS, NSA_HEAD_DIM
    hp = n_pages // 2
    spp = cache_ref.shape[3] // CMP_STRIDE
    ns = n_pages * spp

    def page_copy(bb, half, p):
        src = cache_ref.at[pt_ref[bb, half * hp + p], pl.ds(0, 2)]
        return pltpu.make_async_copy(src, raw_ref.at[half, p], sem.at[half])

    def start_half(bb, half):
        def body(p, c):
            page_copy(bb, half, p).start()
            return c
        lax.fori_loop(0, hp, body, 0)

    def wait_half(bb, half):
        def body(p, c):
            page_copy(bb, half, p).wait()
            return c
        lax.fori_loop(0, hp, body, 0)

    page = cache_ref.shape[3]
    r_i = lax.broadcasted_iota(jnp.int32, (page, page), 0)
    c_i = lax.broadcasted_iota(jnp.int32, (page, page), 1)
    perm = jnp.where(c_i == (r_i % spp) * CMP_STRIDE + r_i // spp, 1.0, 0.0).astype(BF16)

    def to_token_rows(half):
        def body(p, c):
            for kind in range(2):
                moved = _dot_nt(raw_ref[half, p, kind].astype(BF16), perm)
                for gp in range(G // 2):
                    buf_ref[kind * (G // 2) + gp, half * hp + p] = moved[gp * 2 * hd:(gp + 1) * 2 * hd].T
            return c
        lax.fori_loop(0, hp, body, 0, unroll=4)

    @pl.when(b == 0)
    def _():
        start_half(0, 0)
        start_half(0, 1)
        wb_ref[...] = wbig_ref[...].astype(BF16)

    for half in range(2):
        wait_half(b, half)
        to_token_rows(half)

        @pl.when(b + 1 < nb)
        def _():
            start_half(b + 1, half)

    for c in range(2):
        wab = wab_ref[c].astype(BF16)
        w2 = w2_ref[c].astype(BF16)
        for gp in range(G // 2):
            cb = c * (G // 2) + gp
            x = jnp.concatenate(
                [buf_ref[cb, :, l * spp:(l + 1) * spp, :].reshape(ns, 2 * hd).astype(BF16)
                 for l in range(CMP_STRIDE)], axis=1)
            ab2 = _dot(x, wb_ref[c])
            for gl in range(2):
                act = _cmp_hidden(ab2[:, gl * 2 * hd:(gl + 1) * 2 * hd], pe_ref[c], wab)
                col = (c * G + 2 * gp + gl) * hd
                o_ref[0, :, col:col + hd] = _dot(act.astype(BF16), w2).astype(o_ref.dtype)


def compress_sample(cache_t, page_table, wbig, wab, pe, w2):
    B, n_pages = page_table.shape
    page = cache_t.shape[3]
    G, hd = NSA_KV_GROUPS, NSA_HEAD_DIM
    ns = n_pages * page // CMP_STRIDE
    assert n_pages % 2 == 0 and page == 2 * hd
    const = lambda a: pl.BlockSpec(a.shape, lambda b, pt: (0,) * a.ndim)
    grid_spec = pltpu.PrefetchScalarGridSpec(
        num_scalar_prefetch=1,
        grid=(B,),
        in_specs=[pl.BlockSpec(memory_space=pl.ANY), const(wbig), const(wab), const(pe), const(w2)],
        out_specs=pl.BlockSpec((1, ns, 2 * G * hd), lambda b, pt: (b, 0, 0)),
        scratch_shapes=[pltpu.VMEM((2, n_pages // 2, 2, G * hd, page), F32),
                        pltpu.VMEM((G, n_pages, page, 2 * hd), F32),
                        pltpu.VMEM(wbig.shape, BF16),
                        pltpu.SemaphoreType.DMA((2,))],
    )
    return pl.pallas_call(
        functools.partial(_cmp_sample_kernel, n_pages=n_pages),
        grid_spec=grid_spec,
        out_shape=jax.ShapeDtypeStruct((B, ns, 2 * G * hd), BF16),
        compiler_params=_cparams("arbitrary"),
        name="compress_sample",
    )(page_table, cache_t, wbig, wab, pe, w2)


def _group_queries(pr_ref, g):
    hd = NSA_HEAD_DIM
    rows = [pr_ref[0, :, (g * NSA_HPG + h) * hd:(g * NSA_HPG + h + 1) * hd] for h in range(NSA_HPG)]
    return jnp.concatenate(rows, axis=0) * NSA_SCALE


def _nsa_sample_select_kernel(pr_ref, cmp_ref, oc_ref, idx_ref, *, t_pos, n_slc, n_pad):
    G, hpg, hd = NSA_KV_GROUPS, NSA_HPG, NSA_HEAD_DIM
    n_cmp = cmp_ref.shape[1]
    cmp = cmp_ref[0]
    ci = lax.broadcasted_iota(jnp.int32, (n_cmp, n_pad), 0) * CMP_STRIDE
    sj = lax.broadcasted_iota(jnp.int32, (n_cmp, n_pad), 1) * SLC_BLOCK
    ov = jnp.where((ci < sj + SLC_BLOCK) & (ci + CMP_LEN > sj), 1.0, 0.0).astype(BF16)
    blk = lax.broadcasted_iota(jnp.int32, (1, n_pad), 1)
    cur = t_pos // SLC_BLOCK
    forced = (blk == 0) | (blk == cur) | (blk == cur - 1)
    jr = lax.broadcasted_iota(jnp.int32, (n_pad, n_pad), 0)
    jc = lax.broadcasted_iota(jnp.int32, (n_pad, n_pad), 1)
    for g in range(G):
        qg = _group_queries(pr_ref, g).astype(BF16)
        sc = _dot_nt(qg, cmp[:, g * hd:(g + 1) * hd])
        e_pos = lax.broadcasted_iota(jnp.int32, (hpg, n_cmp), 1) * CMP_STRIDE + (CMP_LEN - 1)
        p_c = _masked_softmax(sc, e_pos <= t_pos)
        o_c = _dot(p_c.astype(BF16), cmp[:, (G + g) * hd:(G + g + 1) * hd])
        for h in range(hpg):
            col = (g * hpg + h) * hd
            oc_ref[0, :, col:col + hd] = o_c[h:h + 1]
        hi, mid, lo = _split3(jnp.sum(p_c, axis=0, keepdims=True))
        imp = _dot(hi, ov) + _dot(mid, ov) + _dot(lo, ov)
        score = jnp.where(forced, FORCED_SCORE, jnp.where(blk <= cur, imp, -1.0))
        score = jnp.where(blk < n_slc, score, -2.0)
        col_s = _row_to_col(score)
        beats = (col_s > score) | ((col_s == score) & (jr < jc))
        rank = jnp.sum(jnp.where(beats, 1.0, 0.0), axis=0, keepdims=True)
        rr = lax.broadcasted_iota(jnp.int32, (SLC_TOPK, n_pad), 0).astype(F32)
        bsel = jnp.where(jnp.broadcast_to(rank, (SLC_TOPK, n_pad)) == rr,
                         lax.broadcasted_iota(jnp.int32, (SLC_TOPK, n_pad), 1).astype(F32), 0.0)
        idx_ref[0, g * SLC_TOPK:(g + 1) * SLC_TOPK, :] = jnp.sum(bsel, axis=-1, keepdims=True).astype(jnp.int32)


def nsa_sample_select(proj, cmp_s, t_pos, n_slc):
    B = proj.shape[0]
    n_pad = -(-n_slc // 128) * 128
    G = NSA_KV_GROUPS
    nq = NSA_HEADS * NSA_HEAD_DIM
    return pl.pallas_call(
        functools.partial(_nsa_sample_select_kernel, t_pos=t_pos, n_slc=n_slc, n_pad=n_pad),
        grid=(B,),
        in_specs=[pl.BlockSpec((1, 1, proj.shape[2]), lambda b: (b, 0, 0)),
                  pl.BlockSpec((1,) + cmp_s.shape[1:], lambda b: (b, 0, 0))],
        out_specs=[pl.BlockSpec((1, 1, nq), lambda b: (b, 0, 0)),
                   pl.BlockSpec((1, G * SLC_TOPK, 1), lambda b: (b, 0, 0))],
        out_shape=[jax.ShapeDtypeStruct((B, 1, nq), F32),
                   jax.ShapeDtypeStruct((B, G * SLC_TOPK, 1), jnp.int32)],
        compiler_params=_cparams("parallel"),
        name="nsa_sample_select",
    )(proj, cmp_s)


def _nsa_sample_attend_kernel(pt_ref, idx_ref, pr_ref, oc_ref, kvn_ref, win_ref, cache_ref, o_ref,
                              kvbuf_ref, sem, *, t_pos, past_len):
    b = pl.program_id(0)
    nb = pl.num_programs(0)
    G, hpg, hd = NSA_KV_GROUPS, NSA_HPG, NSA_HEAD_DIM
    n_sel = G * SLC_TOPK
    page = cache_ref.shape[3]
    bpp = page // SLC_BLOCK
    new_blk = past_len // SLC_BLOCK

    def blk_copy(bb, n):
        slot = bb % 2
        j = jnp.minimum(idx_ref[bb, n], new_blk - 1)
        pg = pt_ref[bb, j // bpp]
        rows = pl.ds(pl.multiple_of((n // SLC_TOPK) * hd, hd), hd)
        return pltpu.make_async_copy(cache_ref.at[pg, pl.ds(2, 2), rows], kvbuf_ref.at[slot, n], sem.at[slot])

    def start_all(bb):
        def body(n, c):
            blk_copy(bb, n).start()
            return c
        lax.fori_loop(0, n_sel, body, 0)

    def wait_all(bb):
        def body(n, c):
            blk_copy(bb, n).wait()
            return c
        lax.fori_loop(0, n_sel, body, 0)

    @pl.when(b == 0)
    def _():
        start_all(0)

    @pl.when(b + 1 < nb)
    def _():
        start_all(b + 1)

    wait_all(b)
    slot = b % 2

    nk = SLC_TOPK * page
    w_buf = win_ref.shape[3]
    nq = NSA_HEADS * hd
    gates = _sigmoid(pr_ref[0, :, nq:nq + GATE_LANES])
    kvn = kvn_ref[0]

    def new_row(kind, g):
        return kvn[:, (kind * G + g) * hd:(kind * G + g + 1) * hd].astype(BF16).astype(F32)

    def attend_with_new(qg, s, ok, vt, k_new, v_new):
        s_new = jnp.sum(qg.astype(F32) * k_new, axis=-1, keepdims=True)
        s = jnp.where(ok, s, NEG_INF)
        m = jnp.maximum(jnp.max(s, axis=-1, keepdims=True), s_new)
        e = jnp.where(ok, jnp.exp(s - m), 0.0)
        e_new = jnp.exp(s_new - m)
        den = jnp.maximum(jnp.sum(e, axis=-1, keepdims=True) + e_new, TINY)
        return (_dot_nt(e.astype(BF16), vt) + e_new * v_new) / den

    lane = lax.broadcasted_iota(jnp.int32, (1, nk), 1)
    for g in range(G):
        qg = _group_queries(pr_ref, g).astype(BF16)
        kt = jnp.concatenate([kvbuf_ref[slot, g * SLC_TOPK + r, 0] for r in range(SLC_TOPK)], axis=1).astype(BF16)
        vt = jnp.concatenate([kvbuf_ref[slot, g * SLC_TOPK + r, 1] for r in range(SLC_TOPK)], axis=1).astype(BF16)
        vis = jnp.zeros((1, nk), jnp.int32)
        for r in range(SLC_TOPK):
            j = idx_ref[b, g * SLC_TOPK + r]
            half = jnp.where(j < new_blk, j % bpp, -1)
            vis = jnp.where(lane // page == r, jnp.where((lane % page) // SLC_BLOCK == half, 1, 0), vis)
        ok = jnp.broadcast_to(vis > 0, (hpg, nk))
        o_s = attend_with_new(qg, _dot(qg, kt), ok, vt, new_row(2, g), new_row(3, g))
        kwt = win_ref[0, 0, g * hd:(g + 1) * hd, :].astype(BF16)
        vwt = win_ref[0, 1, g * hd:(g + 1) * hd, :].astype(BF16)
        wpos = past_len - w_buf + lax.broadcasted_iota(jnp.int32, (hpg, w_buf), 1)
        okw = (wpos <= t_pos) & (wpos > t_pos - WINDOW) & (wpos >= 0)
        o_w = attend_with_new(qg, _dot(qg, kwt), okw, vwt, new_row(4, g), new_row(5, g))
        for h in range(hpg):
            col = (g * hpg + h) * hd
            gc = g * GATE_ROWS + h
            o_h = (gates[:, gc:gc + 1] * oc_ref[0, :, col:col + hd]
                   + gates[:, gc + hpg:gc + hpg + 1] * o_s[h:h + 1]
                   + gates[:, gc + 2 * hpg:gc + 2 * hpg + 1] * o_w[h:h + 1])
            o_ref[0, :, col:col + hd] = o_h


def nsa_sample_attend(proj, o_c, kv_new, win_t, cache_t, page_table, idx, t_pos, past_len):
    B = proj.shape[0]
    G, hd = NSA_KV_GROUPS, NSA_HEAD_DIM
    nq = NSA_HEADS * hd
    page = cache_t.shape[3]
    row = lambda a: pl.BlockSpec((1, 1, a.shape[2]), lambda b, pt, ix: (b, 0, 0))
    grid_spec = pltpu.PrefetchScalarGridSpec(
        num_scalar_prefetch=2,
        grid=(B,),
        in_specs=[row(proj), row(o_c), row(kv_new),
                  pl.BlockSpec((1,) + win_t.shape[1:], lambda b, pt, ix: (b, 0, 0, 0)),
                  pl.BlockSpec(memory_space=pl.ANY)],
        out_specs=pl.BlockSpec((1, 1, nq), lambda b, pt, ix: (b, 0, 0)),
        scratch_shapes=[pltpu.VMEM((2, G * SLC_TOPK, 2, hd, page), F32),
                        pltpu.SemaphoreType.DMA((2,))],
    )
    return pl.pallas_call(
        functools.partial(_nsa_sample_attend_kernel, t_pos=t_pos, past_len=past_len),
        grid_spec=grid_spec,
        out_shape=jax.ShapeDtypeStruct((B, 1, nq), F32),
        compiler_params=_cparams("arbitrary"),
        name="nsa_sample_attend",
    )(page_table, idx, proj, o_c, kv_new, win_t, cache_t)


def kernel(x_prompt, x_sample, cache_nsa_kv, cache_win_kv, state_hgrn, page_table, norm_mix, norm_mlp, w_mlp_up, w_mlp_down, w_hgrn_in, hgrn_lb_logits, hgrn_onorm, w_hgrn_out, norm_kv, w_kv, cmp_pe_k, cmp_w1_k, cmp_w2_k, cmp_pe_v, cmp_w1_v, cmp_w2_v, w_nsa_q, w_nsa_out, norm_final):
    B, T, D = x_prompt.shape
    Bs, Ts, _ = x_sample.shape
    G, hd = NSA_KV_GROUPS, NSA_HEAD_DIM
    n_pool, page = cache_nsa_kv.shape[:2]
    past_len = page_table.shape[1] * page
    w_buf = cache_win_kv.shape[1]
    assert Ts == 1 and T % 1024 == 0 and T >= WINDOW + 256 and past_len % SLC_BLOCK == 0 and w_buf <= past_len

    wab, wbig, pe, w2, w2t = _cmp_weights(cmp_pe_k, cmp_w1_k, cmp_pe_v, cmp_w1_v, cmp_w2_k, cmp_w2_v)
    wq = [_permute_gate_cols(w_nsa_q[l]) for l in range(DEPTH - N_A)]

    tm = 1024
    x = x_prompt.reshape(B * T, D)
    states_p = jnp.zeros((N_A, B, HG_HEADS, HG_K, HG_V), F32)
    for l in range(DEPTH):
        if l == N_A:
            nsa_p, win_p, k_hm, vt_hm = kv_proj_prompt(x, norm_kv, w_kv, B, T)
            kc_p, vct_p = compress_prompt(nsa_p, wbig, wab, pe, w2, w2t)
        if l < N_A:
            proj = rms_proj(x, norm_mix[l], w_hgrn_in, 2 * tm, 512, layer=l).reshape(B, T, -1)
            o, states_p = hgrn_prompt(proj, hgrn_lb_logits, hgrn_onorm[l], l, states_p)
            x = proj_res(o.reshape(B * T, -1), w_hgrn_out, x, tm, layer=l)
        else:
            q_hm, gates_t = q_proj_prompt(x, norm_mix[l], wq[l - N_A], B, T)
            o = nsa_prompt(q_hm, gates_t, kc_p, vct_p, k_hm, vt_hm)
            x = proj_res(o.reshape(B * T, -1), w_nsa_out, x, tm, layer=l - N_A)
        x = mlp_res(x, norm_mlp[l], w_mlp_up, w_mlp_down, norm_final, tm, 1024, l == DEPTH - 1, l)
    y_prompt = x.reshape(B, T, D)
    nsa_kv_prompt = nsa_p.reshape(B, 4, G, hd, T).transpose(0, 4, 1, 2, 3)
    win_kv_prompt = win_p.reshape(B, 2, G, hd, T)[..., -min(WINDOW, T):].transpose(0, 4, 1, 2, 3)

    t_pos = past_len
    n_slc = -(-(past_len + 1) // SLC_BLOCK)
    xs = x_sample.reshape(Bs, D)
    cache_t = cache_nsa_kv.transpose(0, 2, 3, 4, 1).reshape(n_pool, 4, G * hd, page)
    win_t = cache_win_kv.transpose(0, 2, 3, 4, 1).reshape(Bs, 2, G * hd, w_buf)
    states_s = jnp.zeros(state_hgrn.shape, F32)
    for l in range(DEPTH):
        if l == N_A:
            kv_s = rms_proj(xs, norm_kv, w_kv, Bs, 512)
            cmp_s = compress_sample(cache_t, page_table, wbig, wab, pe, w2)
        if l < N_A:
            proj = rms_proj(xs, norm_mix[l], w_hgrn_in, Bs, 512, layer=l).reshape(Bs, 1, -1)
            o, states_s = hgrn_step(proj, state_hgrn, hgrn_lb_logits, hgrn_onorm[l], l, states_s)
        else:
            proj = rms_proj(xs, norm_mix[l], wq[l - N_A], Bs, 384).reshape(Bs, 1, -1)
            o_c, idx = nsa_sample_select(proj, cmp_s, t_pos, n_slc)
            o = nsa_sample_attend(proj, o_c, kv_s.reshape(Bs, 1, -1), win_t, cache_t, page_table,
                                  idx.reshape(Bs, G * SLC_TOPK), t_pos, past_len)
        w_o, lo = (w_hgrn_out, l) if l < N_A else (w_nsa_out, l - N_A)
        xs = proj_res(o.reshape(Bs, -1), w_o, xs, Bs, layer=lo)
        xs = mlp_res(xs, norm_mlp[l], w_mlp_up, w_mlp_down, norm_final, Bs, 512, l == DEPTH - 1, l)
    y_sample = xs.reshape(Bs, 1, D)
    n_nsa = 4 * G * hd
    nsa_kv_sample = kv_s[:, :n_nsa].reshape(Bs, 1, 4, G, hd)
    win_new = kv_s[:, n_nsa:].reshape(Bs, 1, 2, G, hd).astype(cache_win_kv.dtype)
    win_kv_sample = jnp.concatenate([cache_win_kv, win_new], axis=1)[:, -w_buf:]

    return (y_prompt, y_sample, nsa_kv_prompt, nsa_kv_sample, win_kv_prompt, win_kv_sample,
            states_p, states_s)
```

```python
import functools

import jax
import jax.numpy as jnp
from jax import lax
from jax.experimental import pallas as pl
from jax.experimental.pallas import tpu as pltpu

F32 = jnp.float32
BF16 = jnp.bfloat16

D_MODEL = 1024
DEPTH = 4
N_A = DEPTH // 2
D_FF = 4 * D_MODEL
RMS_EPS = 1e-6
HG_HEADS = 8
HG_K = 128
HG_V = 128
NSA_HEADS = 16
NSA_HEAD_DIM = 64
NSA_KV_GROUPS = 4
NSA_HPG = NSA_HEADS // NSA_KV_GROUPS
NSA_SCALE = NSA_HEAD_DIM ** -0.5
CMP_LEN = 32
CMP_STRIDE = 16
SLC_BLOCK = 64
SLC_TOPK = 16
WINDOW = 512
FORCED_SCORE = 1e4
NEG_INF = -1e30
TINY = 1e-30

HG_CHUNK = 128
HG_MATRIX_LEVELS = 2
VMEM_LIMIT = 56 * 1024 * 1024


def _cparams(*sem):
    return pltpu.CompilerParams(dimension_semantics=sem, vmem_limit_bytes=VMEM_LIMIT)


def _rms(x, g):
    return x * lax.rsqrt(jnp.mean(x * x, axis=-1, keepdims=True) + RMS_EPS) * g


def _sigmoid(x):
    return 1.0 / (1.0 + jnp.exp(-x))


def _dot(a, b):
    return jnp.dot(a, b, preferred_element_type=F32)


def _dot_nt(a, b):
    return lax.dot_general(a, b, (((1,), (1,)), ((), ())), preferred_element_type=F32)


def _dot_tn(a, b):
    return lax.dot_general(a, b, (((0,), (0,)), ((), ())), preferred_element_type=F32)


def _split3(x):
    hi = x.astype(BF16)
    r1 = x - hi.astype(F32)
    mid = r1.astype(BF16)
    lo = (r1 - mid.astype(F32)).astype(BF16)
    return hi, mid, lo


def _masked_softmax(s, mask):
    s = jnp.where(mask, s, NEG_INF)
    e = jnp.where(mask, jnp.exp(s - jnp.max(s, axis=-1, keepdims=True)), 0.0)
    return e / jnp.maximum(jnp.sum(e, axis=-1, keepdims=True), TINY)


def _rms_proj_kernel(x_ref, g_ref, w_ref, o_ref, y_ref):
    @pl.when(pl.program_id(1) == 0)
    def _():
        y_ref[...] = _rms(x_ref[...], g_ref[...]).astype(BF16)

    o_ref[...] = _dot(y_ref[...], w_ref[...].astype(BF16)).astype(o_ref.dtype)


def _stacked(w, layer):
    return (w[None], 0) if w.ndim == 2 else (w, layer)


def rms_proj(x, g, w, tm, tn, out_dtype=F32, layer=0):
    M, D = x.shape
    w, layer = _stacked(w, layer)
    N = w.shape[2]
    return pl.pallas_call(
        _rms_proj_kernel,
        grid=(M // tm, N // tn),
        in_specs=[pl.BlockSpec((tm, D), lambda i, j: (i, 0)),
                  pl.BlockSpec((1, D), lambda i, j: (0, 0)),
                  pl.BlockSpec((None, D, tn), lambda i, j: (layer, 0, j))],
        out_specs=pl.BlockSpec((tm, tn), lambda i, j: (i, j)),
        out_shape=jax.ShapeDtypeStruct((M, N), out_dtype),
        scratch_shapes=[pltpu.VMEM((tm, D), BF16)],
        compiler_params=_cparams("parallel", "arbitrary"),
        name="rms_proj",
    )(x, g.reshape(1, D), w)


def _proj_res_kernel(a_ref, w_ref, r_ref, o_ref, wb_ref):
    @pl.when(pl.program_id(0) == 0)
    def _():
        wb_ref[...] = w_ref[...].astype(BF16)

    o_ref[...] = r_ref[...] + _dot(a_ref[...].astype(BF16), wb_ref[...])


def proj_res(a, w, res, tm, layer=0):
    M, K = a.shape
    w, layer = _stacked(w, layer)
    N = w.shape[2]
    return pl.pallas_call(
        _proj_res_kernel,
        grid=(M // tm,),
        in_specs=[pl.BlockSpec((tm, K), lambda i: (i, 0)),
                  pl.BlockSpec((None, K, N), lambda i: (layer, 0, 0)),
                  pl.BlockSpec((tm, N), lambda i: (i, 0))],
        out_specs=pl.BlockSpec((tm, N), lambda i: (i, 0)),
        out_shape=jax.ShapeDtypeStruct((M, N), F32),
        scratch_shapes=[pltpu.VMEM((K, N), BF16)],
        compiler_params=_cparams("arbitrary"),
        name="proj_res",
    )(a, w, res)


def _mlp_kernel(x_ref, g_ref, wu_ref, wd_ref, gf_ref, o_ref, y_ref, acc_ref, *, final_norm):
    f = pl.program_id(1)

    @pl.when(f == 0)
    def _():
        y_ref[...] = _rms(x_ref[...], g_ref[...]).astype(BF16)
        acc_ref[...] = jnp.zeros_like(acc_ref)

    h = jnp.maximum(_dot(y_ref[...], wu_ref[...].astype(BF16)), 0.0)
    acc_ref[...] += _dot((h * h).astype(BF16), wd_ref[...].astype(BF16))

    @pl.when(f == pl.num_programs(1) - 1)
    def _():
        out = x_ref[...] + acc_ref[...]
        if final_norm:
            out = _rms(out, gf_ref[...])
        o_ref[...] = out


def mlp_res(x, g, w_up, w_down, g_final, tm, tf, final_norm, layer):
    M, D = x.shape
    Fdim = w_up.shape[2]
    return pl.pallas_call(
        functools.partial(_mlp_kernel, final_norm=final_norm),
        grid=(M // tm, Fdim // tf),
        in_specs=[pl.BlockSpec((tm, D), lambda i, f: (i, 0)),
                  pl.BlockSpec((1, D), lambda i, f: (0, 0)),
                  pl.BlockSpec((None, D, tf), lambda i, f: (layer, 0, f)),
                  pl.BlockSpec((None, tf, D), lambda i, f: (layer, f, 0)),
                  pl.BlockSpec((1, D), lambda i, f: (0, 0))],
        out_specs=pl.BlockSpec((tm, D), lambda i, f: (i, 0)),
        out_shape=jax.ShapeDtypeStruct((M, D), F32),
        scratch_shapes=[pltpu.VMEM((tm, D), BF16), pltpu.VMEM((tm, D), F32)],
        compiler_params=_cparams("parallel", "arbitrary"),
        name="mlp_res",
    )(x, g.reshape(1, D), w_up, w_down, g_final.reshape(1, D))


def _hgrn_lower_bound(lg, layer):
    m = jnp.max(lg, axis=0, keepdims=True)
    e = jnp.exp(lg - m)
    p = e / jnp.sum(e, axis=0, keepdims=True)
    lb = jnp.sum(p[1:layer + 1], axis=0, keepdims=True)
    return jnp.log(lb), jnp.log(1.0 - lb)


def _hgrn_logf(z, lg, layer):
    ls = jnp.minimum(z, 0.0) - jnp.log(1.0 + jnp.exp(-jnp.abs(z)))
    if layer == 0:
        return ls
    log_lb, log1m = _hgrn_lower_bound(lg, layer)
    b2 = log1m + ls
    return jnp.maximum(log_lb, b2) + jnp.log(1.0 + jnp.exp(-jnp.abs(log_lb - b2)))


def _hgrn_sum_matrices(C, n_lev):
    r = lax.broadcasted_iota(jnp.int32, (C, C), 0)
    u = lax.broadcasted_iota(jnp.int32, (C, C), 1)
    mats = [r >= u]
    for lev in range(n_lev):
        h = 1 << lev
        off = r & (2 * h - 1)
        mid = r - off + h
        mats.append(((off >= h) & (u >= mid) & (u <= r)) | ((off < h) & (u > r) & (u < mid)))
    return jnp.concatenate([jnp.where(m, 1.0, 0.0).astype(BF16) for m in mats], axis=0)


def _hgrn_kernel(zq_ref, zf_ref, zi_ref, zg_ref, lg_ref, on_ref, _states_in, o_ref, s_ref, st_ref, w_ref, b_ref,
                 *, layer, tc, nh):
    t = pl.program_id(2)
    C = HG_CHUNK
    n_lev = C.bit_length() - 1
    heads = range(nh)

    @pl.when(t == 0)
    def _():
        st_ref[...] = jnp.zeros_like(st_ref)
        w_ref[...] = _hgrn_sum_matrices(C, HG_MATRIX_LEVELS)

    r_i = lax.broadcasted_iota(jnp.int32, (C, C), 0)
    c_i = lax.broadcasted_iota(jnp.int32, (C, C), 1)
    row = lax.broadcasted_iota(jnp.int32, (C, HG_K), 0)

    def chunk(ci, sts):
        rows = pl.ds(pl.multiple_of(ci * C, C), C)
        kl = [slice(h * HG_K, (h + 1) * HG_K) for h in heads]
        vl = [slice(h * HG_V, (h + 1) * HG_V) for h in heads]
        q = [zq_ref[0, rows, kl[h]] for h in heads]
        vb = [zi_ref[0, rows, vl[h]].astype(BF16) for h in heads]
        logf = [_hgrn_logf(zf_ref[0, rows, kl[h]], lg_ref[:, kl[h]], layer) for h in heads]
        k = [1.0 - jnp.exp(logf[h]) for h in heads]
        parts = [_split3(logf[h]) for h in heads]
        w_sum, w_lev = w_ref[0:C, :], w_ref[C:, :]
        b = [_dot(w_sum, parts[h][0]) + _dot(w_sum, parts[h][1]) + _dot(w_sum, parts[h][2]) for h in heads]
        for h in heads:
            b_ref[h] = b[h]
        e_low = [jnp.exp(_dot(w_lev, parts[h][0]) + _dot(w_lev, parts[h][1])) for h in heads]
        a = [jnp.where(r_i == c_i, _dot_nt(q[h].astype(BF16), k[h].astype(BF16)), 0.0) for h in heads]
        for lev in range(n_lev):
            half = 1 << lev
            upper = (row & (2 * half - 1)) >= half
            same = (r_i >> (lev + 1)) == (c_i >> (lev + 1))
            if lev < HG_MATRIX_LEVELS:
                e = [e_low[h][lev * C:(lev + 1) * C] for h in heads]
            else:
                nblk = C // (2 * half)
                e = []
                for h in heads:
                    bm = b_ref[h, pl.ds(half - 1, nblk, stride=2 * half), :] if nblk > 1 else b_ref[h, half - 1:half, :]
                    bm = jnp.broadcast_to(bm[:, None, :], (nblk, 2 * half, HG_K)).reshape(C, HG_K)
                    e.append(jnp.exp(jnp.where(upper, b[h] - bm, bm - b[h])))
            qt = [jnp.where(upper, q[h] * e[h], 0.0).astype(BF16) for h in heads]
            kt = [jnp.where(upper, 0.0, k[h] * e[h]).astype(BF16) for h in heads]
            al = [_dot_nt(qt[h], kt[h]) for h in heads]
            a = [a[h] + (jnp.where(same, al[h], 0.0) if 2 * half < C else al[h]) for h in heads]
        qd = [(q[h] * jnp.exp(b[h])).astype(BF16) for h in heads]
        o = [_dot(a[h].astype(BF16), vb[h]) + _dot_nt(qd[h], sts[h].astype(BF16)) for h in heads]
        b_end = [b[h][C - 1:C] for h in heads]
        kd = [(k[h] * jnp.exp(b_end[h] - b[h])).astype(BF16) for h in heads]
        new = tuple(sts[h] * jnp.exp(b_end[h]) + _dot_tn(vb[h], kd[h]) for h in heads)
        for h in heads:
            oh = o[h] * lax.rsqrt(jnp.mean(o[h] * o[h], axis=-1, keepdims=True) + RMS_EPS) * on_ref[:, vl[h]]
            zg = zg_ref[0, rows, vl[h]]
            o_ref[0, rows, vl[h]] = (oh * (zg * _sigmoid(zg))).astype(o_ref.dtype)
        return new

    sts = lax.fori_loop(0, tc // C, chunk, tuple(st_ref[h] for h in heads))
    for h in heads:
        st_ref[h] = sts[h]

    @pl.when(t == pl.num_programs(2) - 1)
    def _():
        for h in heads:
            s_ref[0, 0, h] = st_ref[h].T


def hgrn_prompt(proj, lb_logits, onorm, layer, states, tc=512, nh=8):
    B, T, _ = proj.shape
    H = HG_HEADS
    hp = H // nh
    n_mats = 1 + HG_MATRIX_LEVELS
    alias_spec, alias_arg, aliases = [pl.BlockSpec(memory_space=pl.ANY)], [states], {6: 1}
    return pl.pallas_call(
        functools.partial(_hgrn_kernel, layer=layer, tc=tc, nh=nh),
        grid=(B, hp, T // tc),
        in_specs=[pl.BlockSpec((1, tc, nh * HG_K), lambda b, h, t: (b, t, h)),
                  pl.BlockSpec((1, tc, nh * HG_K), lambda b, h, t: (b, t, hp + h)),
                  pl.BlockSpec((1, tc, nh * HG_V), lambda b, h, t: (b, t, 2 * hp + h)),
                  pl.BlockSpec((1, tc, nh * HG_V), lambda b, h, t: (b, t, 3 * hp + h)),
                  pl.BlockSpec((N_A, nh * HG_K), lambda b, h, t: (0, h)),
                  pl.BlockSpec((1, nh * HG_V), lambda b, h, t: (0, h))] + alias_spec,
        out_specs=[pl.BlockSpec((1, tc, nh * HG_V), lambda b, h, t: (b, t, h)),
                   pl.BlockSpec((1, 1, nh, HG_K, HG_V), lambda b, h, t: (layer, b, h, 0, 0))],
        out_shape=[jax.ShapeDtypeStruct((B, T, H * HG_V), BF16),
                   jax.ShapeDtypeStruct((N_A, B, H, HG_K, HG_V), F32)],
        input_output_aliases=aliases,
        scratch_shapes=[pltpu.VMEM((nh, HG_V, HG_K), F32),
                        pltpu.VMEM((n_mats * HG_CHUNK, HG_CHUNK), BF16),
                        pltpu.VMEM((nh, HG_CHUNK, HG_K), F32)],
        compiler_params=_cparams("parallel", "parallel", "arbitrary"),
        name="hgrn_prompt",
    )(proj, proj, proj, proj, lb_logits, onorm.reshape(1, H * HG_V), *alias_arg)


QK_LANES = 2 * NSA_HEAD_DIM
VT_ROWS = NSA_HEAD_DIM + 16


def _kv_proj_kernel(x_ref, g_ref, w_ref, nsa_ref, win_ref, k_ref, vt_ref, wb_ref, *, tpb):
    @pl.when(pl.program_id(0) == 0)
    def _():
        wb_ref[...] = w_ref[...].astype(BF16)

    G, hd = NSA_KV_GROUPS, NSA_HEAD_DIM
    tm = x_ref.shape[0]
    y = _rms(x_ref[...], g_ref[...]).astype(BF16)
    kv = _dot(y, wb_ref[...])
    n_nsa = nsa_ref.shape[1]
    cols_t = [kv[:, c:c + 2 * hd].T for c in range(0, kv.shape[1], 2 * hd)]
    for n, t in enumerate(cols_t):
        c = n * 2 * hd
        if c < n_nsa:
            nsa_ref[0, c:c + 2 * hd, :] = t
        else:
            win_ref[0, c - n_nsa:c - n_nsa + 2 * hd, :] = t
    lane = lax.broadcasted_iota(jnp.int32, (tm, QK_LANES), 1)
    blk = ((pl.program_id(0) % tpb) * tm + lax.broadcasted_iota(jnp.int32, (tm, QK_LANES), 0)) // SLC_BLOCK
    tails = (jnp.where(lane - hd == blk, 1.0, 0.0), jnp.zeros((tm, QK_LANES), F32))
    for n, kind in enumerate((2, 4)):
        for gp in range(G // 2):
            col = (kind * G + 2 * gp) * hd
            pair = kv[:, col:col + 2 * hd]
            for gl, src in enumerate((pair, pltpu.roll(pair, hd, axis=1))):
                k_ref[0, n * G + 2 * gp + gl] = jnp.where(lane < hd, src, tails[n]).astype(BF16)
    ones_row = jnp.where(lax.broadcasted_iota(jnp.int32, (VT_ROWS - hd, tm), 0) == 0, 1.0, 0.0).astype(BF16)
    for n, kind in enumerate((3, 5)):
        for gp in range(G // 2):
            t = cols_t[(kind * G + 2 * gp) * hd // (2 * hd)]
            for gl in range(2):
                vt_ref[0, n * G + 2 * gp + gl, 0:hd, :] = t[gl * hd:(gl + 1) * hd].astype(BF16)
                vt_ref[0, n * G + 2 * gp + gl, hd:, :] = ones_row


def kv_proj_prompt(x, g, w_kv, B, T, tm=512):
    M, D = x.shape
    N = w_kv.shape[1]
    G, hd = NSA_KV_GROUPS, NSA_HEAD_DIM
    n_nsa = 4 * G * hd
    tpb = T // tm
    assert T // SLC_BLOCK <= QK_LANES - hd
    return pl.pallas_call(
        functools.partial(_kv_proj_kernel, tpb=tpb),
        grid=(M // tm,),
        in_specs=[pl.BlockSpec((tm, D), lambda i: (i, 0)),
                  pl.BlockSpec((1, D), lambda i: (0, 0)),
                  pl.BlockSpec((D, N), lambda i: (0, 0))],
        out_specs=[pl.BlockSpec((1, n_nsa, tm), lambda i: (i // tpb, 0, i % tpb)),
                   pl.BlockSpec((1, N - n_nsa, tm), lambda i: (i // tpb, 0, i % tpb)),
                   pl.BlockSpec((1, 2 * G, tm, QK_LANES), lambda i: (i // tpb, 0, i % tpb, 0)),
                   pl.BlockSpec((1, 2 * G, VT_ROWS, tm), lambda i: (i // tpb, 0, 0, i % tpb))],
        out_shape=[jax.ShapeDtypeStruct((B, n_nsa, T), F32),
                   jax.ShapeDtypeStruct((B, N - n_nsa, T), F32),
                   jax.ShapeDtypeStruct((B, 2 * G, T, QK_LANES), BF16),
                   jax.ShapeDtypeStruct((B, 2 * G, VT_ROWS, T), BF16)],
        scratch_shapes=[pltpu.VMEM((D, N), BF16)],
        compiler_params=_cparams("arbitrary"),
        name="kv_proj",
    )(x, g.reshape(1, D), w_kv)


GATE_ROWS = 16
GATE_LANES = 128


def _q_proj_kernel(x_ref, g_ref, w_ref, q_ref, gt_ref, wb_ref):
    @pl.when(pl.program_id(0) == 0)
    def _():
        wb_ref[...] = w_ref[...].astype(BF16)

    y = _rms(x_ref[...], g_ref[...]).astype(BF16)
    pr = _dot(y, wb_ref[...])
    hd = NSA_HEAD_DIM
    nq = NSA_HEADS * hd
    low = lax.broadcasted_iota(jnp.int32, (pr.shape[0], QK_LANES), 1) < hd
    for hp in range(NSA_HEADS // 2):
        pair = pr[:, hp * 2 * hd:(hp + 1) * 2 * hd] * NSA_SCALE
        for hl, src in enumerate((pair, pltpu.roll(pair, hd, axis=1))):
            q_ref[0, 2 * hp + hl] = jnp.where(low, src, 0.0).astype(BF16)
    gates_t = _sigmoid(pr[:, nq:]).T
    for gi in range(NSA_KV_GROUPS):
        gt_ref[0, gi] = gates_t[gi * GATE_ROWS:(gi + 1) * GATE_ROWS]


def _permute_gate_cols(w_q):
    nq = NSA_HEADS * NSA_HEAD_DIM
    d = w_q.shape[0]
    wg = w_q[:, nq:].reshape(d, 3, NSA_KV_GROUPS, NSA_HPG).transpose(0, 2, 1, 3).reshape(d, NSA_KV_GROUPS, 3 * NSA_HPG)
    wg = jnp.pad(wg, ((0, 0), (0, 0), (0, GATE_ROWS - 3 * NSA_HPG))).reshape(d, NSA_KV_GROUPS * GATE_ROWS)
    wg = jnp.pad(wg, ((0, 0), (0, GATE_LANES - NSA_KV_GROUPS * GATE_ROWS)))
    return jnp.concatenate([w_q[:, :nq], wg], axis=1)


def q_proj_prompt(x, g, w_qp, B, T, tm=512):
    M, D = x.shape
    N = w_qp.shape[1]
    tpb = T // tm
    return pl.pallas_call(
        _q_proj_kernel,
        grid=(M // tm,),
        in_specs=[pl.BlockSpec((tm, D), lambda i: (i, 0)),
                  pl.BlockSpec((1, D), lambda i: (0, 0)),
                  pl.BlockSpec((D, N), lambda i: (0, 0))],
        out_specs=[pl.BlockSpec((1, NSA_HEADS, tm, QK_LANES), lambda i: (i // tpb, 0, i % tpb, 0)),
                   pl.BlockSpec((1, NSA_KV_GROUPS, GATE_ROWS, tm), lambda i: (i // tpb, 0, 0, i % tpb))],
        out_shape=[jax.ShapeDtypeStruct((B, NSA_HEADS, T, QK_LANES), BF16),
                   jax.ShapeDtypeStruct((B, NSA_KV_GROUPS, GATE_ROWS, T), F32)],
        scratch_shapes=[pltpu.VMEM((D, N), BF16)],
        compiler_params=_cparams("arbitrary"),
        name="q_proj",
    )(x, g.reshape(1, D), w_qp)


def _cmp_weights(pe_k, w1_k, pe_v, w1_v, w2_k, w2_v):
    half = CMP_STRIDE * NSA_HEAD_DIM

    def ab(w1):
        return jnp.concatenate([w1[:half], w1[half:]], axis=1)

    def big(w1):
        w = w1.reshape(2, CMP_STRIDE, NSA_HEAD_DIM, -1)
        b = jnp.einsum("alds,gh->lgdhas", w, jnp.eye(2, dtype=w1.dtype))
        return b.reshape(CMP_STRIDE * 2 * NSA_HEAD_DIM, 2 * 2 * w.shape[-1])

    wab = jnp.stack([ab(w1_k), ab(w1_v)])
    wbig = jnp.stack([big(w1_k), big(w1_v)])
    pe = jnp.stack([pe_k.reshape(2, half), pe_v.reshape(2, half)])
    w2 = jnp.stack([w2_k, w2_v])
    w2t = jnp.stack([w2_k.T, w2_v.T])
    return wab, wbig, pe, w2, w2t


def _cmp_taps(x_ref, ns):
    return jnp.concatenate([x_ref[pl.ds(l, ns, stride=CMP_STRIDE), :].astype(BF16) for l in range(CMP_STRIDE)], axis=1)


def _cmp_hidden(ab, pe, wab):
    hd = NSA_HEAD_DIM
    n = ab.shape[0]
    nxt = pltpu.roll(ab, n - 1, axis=0)
    pt = _dot(pe.astype(BF16), wab)
    hid = ab[:, :hd] + nxt[:, hd:] + pt[0:1, :hd] + pt[1:2, hd:]
    return hid * _sigmoid(hid)


def _cmp_prompt_kernel(xk_ref, xv_ref, wbig_ref, wab_ref, pe_ref, w2_ref, w2t_ref, kc_ref, vct_ref, buf_ref):
    hd = NSA_HEAD_DIM
    ns = xk_ref.shape[2] // CMP_STRIDE
    for c, x_ref in enumerate((xk_ref, xv_ref)):
        wab = wab_ref[c].astype(BF16)
        buf_ref[...] = x_ref[0].T
        ab2 = _dot(_cmp_taps(buf_ref, ns), wbig_ref[c].astype(BF16))
        for gl in range(2):
            act = _cmp_hidden(ab2[:, gl * 2 * hd:(gl + 1) * 2 * hd], pe_ref[c], wab).astype(BF16)
            if c == 0:
                kc_ref[0, gl] = _dot(act, w2_ref[c].astype(BF16)).astype(kc_ref.dtype)
            else:
                vct_ref[0, gl] = _dot_nt(w2t_ref[c].astype(BF16), act).astype(vct_ref.dtype)


def compress_prompt(nsa_t, wbig, wab, pe, w2, w2t):
    B, _, T = nsa_t.shape
    G, hd = NSA_KV_GROUPS, NSA_HEAD_DIM
    ns = T // CMP_STRIDE
    w2 = jnp.pad(w2, ((0, 0), (0, 0), (0, QK_LANES - hd)))
    const = lambda a: pl.BlockSpec(a.shape, lambda b, gp: (0,) * a.ndim)
    return pl.pallas_call(
        _cmp_prompt_kernel,
        grid=(B, G // 2),
        in_specs=[pl.BlockSpec((1, 2 * hd, T), lambda b, gp: (b, gp, 0)),
                  pl.BlockSpec((1, 2 * hd, T), lambda b, gp: (b, G // 2 + gp, 0)),
                  const(wbig), const(wab), const(pe), const(w2), const(w2t)],
        out_specs=[pl.BlockSpec((1, 2, ns, QK_LANES), lambda b, gp: (b, gp, 0, 0)),
                   pl.BlockSpec((1, 2, hd, ns), lambda b, gp: (b, gp, 0, 0))],
        out_shape=[jax.ShapeDtypeStruct((B, G, ns, QK_LANES), BF16),
                   jax.ShapeDtypeStruct((B, G, hd, ns), BF16)],
        scratch_shapes=[pltpu.VMEM((T, 2 * hd), F32)],
        compiler_params=_cparams("parallel", "parallel"),
        name="compress_prompt",
    )(nsa_t, nsa_t, wbig, wab, pe, w2, w2t)


def _nsa_prompt_kernel(q_ref, gt_ref, kc_ref, vct_ref, ks_ref, vst_ref, kw_ref, vwt_ref, o_ref, *, tq):
    i = pl.program_id(2)
    s0 = i * tq
    hpg, hd = NSA_HPG, NSA_HEAD_DIM
    T = ks_ref.shape[2]
    n_cmp = kc_ref.shape[2]
    n_slc = T // SLC_BLOCK
    R = hpg * tq
    Q = q_ref[0].reshape(R, QK_LANES)
    tpos = s0 + lax.broadcasted_iota(jnp.int32, (1, R), 1) % tq

    sc = _dot_nt(kc_ref[0, 0], Q)
    ok_c = lax.broadcasted_iota(jnp.int32, (n_cmp, R), 0) * CMP_STRIDE + (CMP_LEN - 1) <= tpos
    sc = jnp.where(ok_c, sc, NEG_INF)
    e_c = jnp.where(ok_c, jnp.exp(sc - jnp.max(sc, axis=0, keepdims=True)), 0.0)
    p_c = e_c / jnp.maximum(jnp.sum(e_c, axis=0, keepdims=True), TINY)
    o_c = _dot(vct_ref[0, 0], p_c.astype(BF16))

    psum = p_c[:, 0:tq]
    for h in range(1, hpg):
        psum = psum + p_c[:, h * tq:(h + 1) * tq]
    sj = lax.broadcasted_iota(jnp.int32, (n_slc, n_cmp), 0) * SLC_BLOCK
    ci = lax.broadcasted_iota(jnp.int32, (n_slc, n_cmp), 1) * CMP_STRIDE
    ov = jnp.where((ci < sj + SLC_BLOCK) & (ci + CMP_LEN > sj), 1.0, 0.0).astype(BF16)
    hi, mid, lo = _split3(psum)
    imp = _dot(ov, hi) + _dot(ov, mid) + _dot(ov, lo)
    blk = lax.broadcasted_iota(jnp.int32, (n_slc, tq), 0)
    qpos = s0 + lax.broadcasted_iota(jnp.int32, (n_slc, tq), 1)
    cur = qpos // SLC_BLOCK
    forced = (blk == 0) | (blk == cur) | (blk == cur - 1)
    score = jnp.where(forced, FORCED_SCORE, jnp.where(blk <= cur, imp, -1.0))
    rank = jnp.zeros((n_slc, tq), F32)
    for j in range(n_slc):
        cj = score[j:j + 1, :]
        rank = rank + jnp.where((cj > score) | ((cj == score) & (blk > j)), 1.0, 0.0)
    bias = jnp.where(rank < SLC_TOPK, 0.0, NEG_INF)
    bias = jnp.concatenate([jnp.zeros((hd, tq), F32), bias, jnp.zeros((QK_LANES - hd - n_slc, tq), F32)], axis=0)
    bias_t = bias.T.astype(BF16)
    q_sel = Q + jnp.concatenate([bias_t] * hpg, axis=0)

    heads = range(hpg)
    q_heads = [q_sel[h * tq:(h + 1) * tq] for h in heads]
    tpos_h = tpos[:, :tq]

    hq = tq // 2

    def update(m, acc, s, vt):
        m_new = jnp.maximum(m, jnp.max(s, axis=0, keepdims=True))
        return m_new, jnp.exp(m - m_new) * acc + _dot(vt, jnp.exp(s - m_new).astype(BF16))

    def causal(s, k0, tp):
        return jnp.where(k0 + lax.broadcasted_iota(jnp.int32, s.shape, 0) <= tp, s, NEG_INF)

    def chunk(c, carry):
        k0 = pl.multiple_of(c * tq, tq)
        ks = ks_ref[0, 0, pl.ds(k0, tq), :]
        vst = vst_ref[0, 0, :, pl.ds(k0, tq)]
        s_heads = [_dot_nt(ks, q_heads[h]) for h in heads]
        return tuple(update(*carry[h], s_heads[h], vst) for h in heads)

    def diagonal_chunk(carry):
        k0 = pl.multiple_of(i * tq, tq)
        k1 = pl.multiple_of(i * tq + hq, hq)
        s_lo = [_dot_nt(ks_ref[0, 0, pl.ds(k0, hq), :], q_heads[h]) for h in heads]
        s_hi = [_dot_nt(ks_ref[0, 0, pl.ds(k1, hq), :], q_heads[h][hq:]) for h in heads]
        new = []
        for h in heads:
            m, acc = update(*carry[h], causal(s_lo[h], k0, tpos_h), vst_ref[0, 0, :, pl.ds(k0, hq)])
            m_b, acc_b = update(m[:, hq:], acc[:, hq:], causal(s_hi[h], k1, tpos_h[:, hq:]),
                                vst_ref[0, 0, :, pl.ds(k1, hq)])
            new.append((jnp.concatenate([m[:, :hq], m_b], axis=1), jnp.concatenate([acc[:, :hq], acc_b], axis=1)))
        return tuple(new)

    carry = tuple((jnp.full((1, tq), NEG_INF, F32), jnp.zeros((VT_ROWS, tq), F32)) for _ in heads)
    carry = diagonal_chunk(lax.fori_loop(0, i, chunk, carry))
    o_s = jnp.concatenate([acc[:hd] / jnp.maximum(acc[hd:hd + 1], TINY) for _, acc in carry], axis=1)

    ws = pl.multiple_of(jnp.maximum(s0 - WINDOW, 0), tq)
    n_w = WINDOW // tq + 1
    pieces = []
    for j in range(n_w - 1):
        pieces.append((pl.multiple_of(ws + j * tq, tq), tq, 0))
    pieces.append((pl.multiple_of(ws + (n_w - 1) * tq, tq), hq, 0))
    pieces.append((pl.multiple_of(ws + (n_w - 1) * tq + hq, hq), hq, hq))
    o_w = []
    for h in heads:
        q_h = Q[h * tq:(h + 1) * tq]
        sw = []
        for n, (k0, nk, t0) in enumerate(pieces):
            s = _dot_nt(kw_ref[0, 0, pl.ds(k0, nk), :], q_h[t0:])
            wpos = k0 + lax.broadcasted_iota(jnp.int32, s.shape, 0)
            tp = tpos_h[:, t0:]
            ok = (wpos <= tp) & (wpos > tp - WINDOW) if n == 0 else wpos <= tp
            sw.append(jnp.where(ok, s, NEG_INF))
        m_w = functools.reduce(jnp.maximum, [jnp.max(s, axis=0, keepdims=True) for s in sw[:-1]])
        m_w = jnp.concatenate([m_w[:, :hq], jnp.maximum(m_w[:, hq:], jnp.max(sw[-1], axis=0, keepdims=True))], axis=1)
        acc_w = sum(_dot(vwt_ref[0, 0, :, pl.ds(k0, nk)], jnp.exp(s - m_w).astype(BF16))
                    for (k0, nk, _), s in zip(pieces[:-1], sw[:-1]))
        k0, nk, _ = pieces[-1]
        tail = _dot(vwt_ref[0, 0, :, pl.ds(k0, nk)], jnp.exp(sw[-1] - m_w[:, hq:]).astype(BF16))
        acc_w = acc_w + jnp.concatenate([jnp.zeros((VT_ROWS, hq), F32), tail], axis=1)
        o_w.append(acc_w[:hd] / jnp.maximum(acc_w[hd:hd + 1], TINY))
    o_w = jnp.concatenate(o_w, axis=1)

    gt = gt_ref[0, 0]
    outs = []
    for h in range(hpg):
        cols = slice(h * tq, (h + 1) * tq)
        outs.append(gt[h:h + 1] * o_c[:, cols] + gt[hpg + h:hpg + h + 1] * o_s[:, cols]
                    + gt[2 * hpg + h:2 * hpg + h + 1] * o_w[:, cols])
    for pair in range(hpg // 2):
        both = jnp.concatenate(outs[2 * pair:2 * pair + 2], axis=0)
        o_ref[0, :, pair * 2 * hd:(pair + 1) * 2 * hd] = both.T.astype(o_ref.dtype)


def nsa_prompt(q_hm, gates_t, kc, vct, k_hm, vt_hm, tq=512):
    B, _, T, _ = q_hm.shape
    G, hpg, hd = NSA_KV_GROUPS, NSA_HPG, NSA_HEAD_DIM
    n_cmp = kc.shape[2]
    assert WINDOW % tq == 0 and T >= WINDOW + tq
    keys = lambda off: pl.BlockSpec((1, 1, T, QK_LANES), lambda b, g, i: (b, off + g, 0, 0))
    vals = lambda off: pl.BlockSpec((1, 1, VT_ROWS, T), lambda b, g, i: (b, off + g, 0, 0))
    return pl.pallas_call(
        functools.partial(_nsa_prompt_kernel, tq=tq),
        grid=(B, G, T // tq),
        in_specs=[pl.BlockSpec((1, hpg, tq, QK_LANES), lambda b, g, i: (b, g, i, 0)),
                  pl.BlockSpec((1, 1, GATE_ROWS, tq), lambda b, g, i: (b, g, 0, i)),
                  pl.BlockSpec((1, 1, n_cmp, QK_LANES), lambda b, g, i: (b, g, 0, 0)),
                  pl.BlockSpec((1, 1, hd, n_cmp), lambda b, g, i: (b, g, 0, 0)),
                  keys(0), vals(0), keys(G), vals(G)],
        out_specs=pl.BlockSpec((1, tq, hpg * hd), lambda b, g, i: (b, i, g)),
        out_shape=jax.ShapeDtypeStruct((B, T, NSA_HEADS * hd), BF16),
        compiler_params=_cparams("parallel", "parallel", "arbitrary"),
        name="nsa_prompt",
    )(q_hm, gates_t, kc, vct, k_hm, vt_hm, k_hm, vt_hm)


def _row_to_col(row):
    n = row.shape[1]
    eye = lax.broadcasted_iota(jnp.int32, (n, n), 0) == lax.broadcasted_iota(jnp.int32, (n, n), 1)
    return jnp.sum(jnp.where(eye, jnp.broadcast_to(row, (n, n)), 0.0), axis=-1, keepdims=True)


def _hgrn_step_kernel(z_ref, s0_ref, lg_ref, on_ref, _states_in, o_ref, s_ref, *, layer):
    s0_ref, s_ref = s0_ref.at[0], s_ref.at[0]
    hk = HG_HEADS * HG_K
    hv = HG_HEADS * HG_V
    for h in range(HG_HEADS):
        kl = slice(h * HG_K, (h + 1) * HG_K)
        vl = slice(h * HG_V, (h + 1) * HG_V)
        q = z_ref[0, :, kl]
        logf = _hgrn_logf(z_ref[0, :, hk + h * HG_K:hk + (h + 1) * HG_K], lg_ref[:, kl], layer)
        f = jnp.exp(logf)
        v = z_ref[0, :, 2 * hk + h * HG_V:2 * hk + (h + 1) * HG_V]
        zg = z_ref[0, :, 2 * hk + hv + h * HG_V:2 * hk + hv + (h + 1) * HG_V]
        s = _row_to_col(f) * s0_ref[0, h] + _row_to_col(1.0 - f) * v
        s_ref[0, h] = s
        o = jnp.sum(_row_to_col(q) * s, axis=0, keepdims=True)
        o = o * lax.rsqrt(jnp.mean(o * o, axis=-1, keepdims=True) + RMS_EPS) * on_ref[:, vl]
        o_ref[0, :, vl] = o * (zg * _sigmoid(zg))


def hgrn_step(proj, s0_all, lb_logits, onorm, layer, states):
    B = proj.shape[0]
    H = HG_HEADS
    alias_spec, alias_arg, aliases = [pl.BlockSpec(memory_space=pl.ANY)], [states], {4: 1}
    state_block = pl.BlockSpec((1, 1, H, HG_K, HG_V), lambda b: (layer, b, 0, 0, 0))
    return pl.pallas_call(
        functools.partial(_hgrn_step_kernel, layer=layer),
        grid=(B,),
        in_specs=[pl.BlockSpec((1, 1, proj.shape[2]), lambda b: (b, 0, 0)),
                  state_block,
                  pl.BlockSpec((N_A, H * HG_K), lambda b: (0, 0)),
                  pl.BlockSpec((1, H * HG_V), lambda b: (0, 0))] + alias_spec,
        out_specs=[pl.BlockSpec((1, 1, H * HG_V), lambda b: (b, 0, 0)), state_block],
        out_shape=[jax.ShapeDtypeStruct((B, 1, H * HG_V), F32),
                   jax.ShapeDtypeStruct(s0_all.shape, F32)],
        input_output_aliases=aliases,
        compiler_params=_cparams("parallel"),
        name="hgrn_step",
    )(proj, s0_all, lb_logits, onorm.reshape(1, H * HG_V), *alias_arg)


def _cmp_sample_kernel(pt_ref, cache_ref, wbig_ref, wab_ref, pe_ref, w2_ref, o_ref, raw_ref, buf_ref, wb_ref,
                       sem, *, n_pages):
    b = pl.program_id(0)
    nb = pl.num_programs(0)
    G, hd = NSA_KV_GROUPS, NSA_HEAD_DIM
    hp = n_pages // 2
    spp = cache_ref.shape[3] // CMP_STRIDE
    ns = n_pages * spp

    def page_copy(bb, half, p):
        src = cache_ref.at[pt_ref[bb, half * hp + p], pl.ds(0, 2)]
        return pltpu.make_async_copy(src, raw_ref.at[half, p], sem.at[half])

    def start_half(bb, half):
        def body(p, c):
            page_copy(bb, half, p).start()
            return c
        lax.fori_loop(0, hp, body, 0)

    def wait_half(bb, half):
        def body(p, c):
            page_copy(bb, half, p).wait()
            return c
        lax.fori_loop(0, hp, body, 0)

    page = cache_ref.shape[3]
    r_i = lax.broadcasted_iota(jnp.int32, (page, page), 0)
    c_i = lax.broadcasted_iota(jnp.int32, (page, page), 1)
    perm = jnp.where(c_i == (r_i % spp) * CMP_STRIDE + r_i // spp, 1.0, 0.0).astype(BF16)

    def to_token_rows(half):
        def body(p, c):
            for kind in range(2):
                moved = _dot_nt(raw_ref[half, p, kind].astype(BF16), perm)
                for gp in range(G // 2):
                    buf_ref[kind * (G // 2) + gp, half * hp + p] = moved[gp * 2 * hd:(gp + 1) * 2 * hd].T
            return c
        lax.fori_loop(0, hp, body, 0, unroll=4)

    @pl.when(b == 0)
    def _():
        start_half(0, 0)
        start_half(0, 1)
        wb_ref[...] = wbig_ref[...].astype(BF16)

    for half in range(2):
        wait_half(b, half)
        to_token_rows(half)

        @pl.when(b + 1 < nb)
        def _():
            start_half(b + 1, half)

    for c in range(2):
        wab = wab_ref[c].astype(BF16)
        w2 = w2_ref[c].astype(BF16)
        for gp in range(G // 2):
            cb = c * (G // 2) + gp
            x = jnp.concatenate(
                [buf_ref[cb, :, l * spp:(l + 1) * spp, :].reshape(ns, 2 * hd).astype(BF16)
                 for l in range(CMP_STRIDE)], axis=1)
            ab2 = _dot(x, wb_ref[c])
            for gl in range(2):
                act = _cmp_hidden(ab2[:, gl * 2 * hd:(gl + 1) * 2 * hd], pe_ref[c], wab)
                col = (c * G + 2 * gp + gl) * hd
                o_ref[0, :, col:col + hd] = _dot(act.astype(BF16), w2).astype(o_ref.dtype)


def compress_sample(cache_t, page_table, wbig, wab, pe, w2):
    B, n_pages = page_table.shape
    page = cache_t.shape[3]
    G, hd = NSA_KV_GROUPS, NSA_HEAD_DIM
    ns = n_pages * page // CMP_STRIDE
    assert n_pages % 2 == 0 and page == 2 * hd
    const = lambda a: pl.BlockSpec(a.shape, lambda b, pt: (0,) * a.ndim)
    grid_spec = pltpu.PrefetchScalarGridSpec(
        num_scalar_prefetch=1,
        grid=(B,),
        in_specs=[pl.BlockSpec(memory_space=pl.ANY), const(wbig), const(wab), const(pe), const(w2)],
        out_specs=pl.BlockSpec((1, ns, 2 * G * hd), lambda b, pt: (b, 0, 0)),
        scratch_shapes=[pltpu.VMEM((2, n_pages // 2, 2, G * hd, page), F32),
                        pltpu.VMEM((G, n_pages, page, 2 * hd), F32),
                        pltpu.VMEM(wbig.shape, BF16),
                        pltpu.SemaphoreType.DMA((2,))],
    )
    return pl.pallas_call(
        functools.partial(_cmp_sample_kernel, n_pages=n_pages),
        grid_spec=grid_spec,
        out_shape=jax.ShapeDtypeStruct((B, ns, 2 * G * hd), BF16),
        compiler_params=_cparams("arbitrary"),
        name="compress_sample",
    )(page_table, cache_t, wbig, wab, pe, w2)


def _group_queries(pr_ref, g):
    hd = NSA_HEAD_DIM
    rows = [pr_ref[0, :, (g * NSA_HPG + h) * hd:(g * NSA_HPG + h + 1) * hd] for h in range(NSA_HPG)]
    return jnp.concatenate(rows, axis=0) * NSA_SCALE


def _nsa_sample_select_kernel(pr_ref, cmp_ref, oc_ref, idx_ref, *, t_pos, n_slc, n_pad):
    G, hpg, hd = NSA_KV_GROUPS, NSA_HPG, NSA_HEAD_DIM
    n_cmp = cmp_ref.shape[1]
    cmp = cmp_ref[0]
    ci = lax.broadcasted_iota(jnp.int32, (n_cmp, n_pad), 0) * CMP_STRIDE
    sj = lax.broadcasted_iota(jnp.int32, (n_cmp, n_pad), 1) * SLC_BLOCK
    ov = jnp.where((ci < sj + SLC_BLOCK) & (ci + CMP_LEN > sj), 1.0, 0.0).astype(BF16)
    blk = lax.broadcasted_iota(jnp.int32, (1, n_pad), 1)
    cur = t_pos // SLC_BLOCK
    forced = (blk == 0) | (blk == cur) | (blk == cur - 1)
    jr = lax.broadcasted_iota(jnp.int32, (n_pad, n_pad), 0)
    jc = lax.broadcasted_iota(jnp.int32, (n_pad, n_pad), 1)
    for g in range(G):
        qg = _group_queries(pr_ref, g).astype(BF16)
        sc = _dot_nt(qg, cmp[:, g * hd:(g + 1) * hd])
        e_pos = lax.broadcasted_iota(jnp.int32, (hpg, n_cmp), 1) * CMP_STRIDE + (CMP_LEN - 1)
        p_c = _masked_softmax(sc, e_pos <= t_pos)
        o_c = _dot(p_c.astype(BF16), cmp[:, (G + g) * hd:(G + g + 1) * hd])
        for h in range(hpg):
            col = (g * hpg + h) * hd
            oc_ref[0, :, col:col + hd] = o_c[h:h + 1]
        hi, mid, lo = _split3(jnp.sum(p_c, axis=0, keepdims=True))
        imp = _dot(hi, ov) + _dot(mid, ov) + _dot(lo, ov)
        score = jnp.where(forced, FORCED_SCORE, jnp.where(blk <= cur, imp, -1.0))
        score = jnp.where(blk < n_slc, score, -2.0)
        col_s = _row_to_col(score)
        beats = (col_s > score) | ((col_s == score) & (jr < jc))
        rank = jnp.sum(jnp.where(beats, 1.0, 0.0), axis=0, keepdims=True)
        rr = lax.broadcasted_iota(jnp.int32, (SLC_TOPK, n_pad), 0).astype(F32)
        bsel = jnp.where(jnp.broadcast_to(rank, (SLC_TOPK, n_pad)) == rr,
                         lax.broadcasted_iota(jnp.int32, (SLC_TOPK, n_pad), 1).astype(F32), 0.0)
        idx_ref[0, g * SLC_TOPK:(g + 1) * SLC_TOPK, :] = jnp.sum(bsel, axis=-1, keepdims=True).astype(jnp.int32)


def nsa_sample_select(proj, cmp_s, t_pos, n_slc):
    B = proj.shape[0]
    n_pad = -(-n_slc // 128) * 128
    G = NSA_KV_GROUPS
    nq = NSA_HEADS * NSA_HEAD_DIM
    return pl.pallas_call(
        functools.partial(_nsa_sample_select_kernel, t_pos=t_pos, n_slc=n_slc, n_pad=n_pad),
        grid=(B,),
        in_specs=[pl.BlockSpec((1, 1, proj.shape[2]), lambda b: (b, 0, 0)),
                  pl.BlockSpec((1,) + cmp_s.shape[1:], lambda b: (b, 0, 0))],
        out_specs=[pl.BlockSpec((1, 1, nq), lambda b: (b, 0, 0)),
                   pl.BlockSpec((1, G * SLC_TOPK, 1), lambda b: (b, 0, 0))],
        out_shape=[jax.ShapeDtypeStruct((B, 1, nq), F32),
                   jax.ShapeDtypeStruct((B, G * SLC_TOPK, 1), jnp.int32)],
        compiler_params=_cparams("parallel"),
        name="nsa_sample_select",
    )(proj, cmp_s)


def _nsa_sample_attend_kernel(pt_ref, idx_ref, pr_ref, oc_ref, kvn_ref, win_ref, cache_ref, o_ref,
                              kvbuf_ref, sem, *, t_pos, past_len):
    b = pl.program_id(0)
    nb = pl.num_programs(0)
    G, hpg, hd = NSA_KV_GROUPS, NSA_HPG, NSA_HEAD_DIM
    n_sel = G * SLC_TOPK
    page = cache_ref.shape[3]
    bpp = page // SLC_BLOCK
    new_blk = past_len // SLC_BLOCK

    def blk_copy(bb, n):
        slot = bb % 2
        j = jnp.minimum(idx_ref[bb, n], new_blk - 1)
        pg = pt_ref[bb, j // bpp]
        rows = pl.ds(pl.multiple_of((n // SLC_TOPK) * hd, hd), hd)
        return pltpu.make_async_copy(cache_ref.at[pg, pl.ds(2, 2), rows], kvbuf_ref.at[slot, n], sem.at[slot])

    def start_all(bb):
        def body(n, c):
            blk_copy(bb, n).start()
            return c
        lax.fori_loop(0, n_sel, body, 0)

    def wait_all(bb):
        def body(n, c):
            blk_copy(bb, n).wait()
            return c
        lax.fori_loop(0, n_sel, body, 0)

    @pl.when(b == 0)
    def _():
        start_all(0)

    @pl.when(b + 1 < nb)
    def _():
        start_all(b + 1)

    wait_all(b)
    slot = b % 2

    nk = SLC_TOPK * page
    w_buf = win_ref.shape[3]
    nq = NSA_HEADS * hd
    gates = _sigmoid(pr_ref[0, :, nq:nq + GATE_LANES])
    kvn = kvn_ref[0]

    def new_row(kind, g):
        return kvn[:, (kind * G + g) * hd:(kind * G + g + 1) * hd].astype(BF16).astype(F32)

    def attend_with_new(qg, s, ok, vt, k_new, v_new):
        s_new = jnp.sum(qg.astype(F32) * k_new, axis=-1, keepdims=True)
        s = jnp.where(ok, s, NEG_INF)
        m = jnp.maximum(jnp.max(s, axis=-1, keepdims=True), s_new)
        e = jnp.where(ok, jnp.exp(s - m), 0.0)
        e_new = jnp.exp(s_new - m)
        den = jnp.maximum(jnp.sum(e, axis=-1, keepdims=True) + e_new, TINY)
        return (_dot_nt(e.astype(BF16), vt) + e_new * v_new) / den

    lane = lax.broadcasted_iota(jnp.int32, (1, nk), 1)
    for g in range(G):
        qg = _group_queries(pr_ref, g).astype(BF16)
        kt = jnp.concatenate([kvbuf_ref[slot, g * SLC_TOPK + r, 0] for r in range(SLC_TOPK)], axis=1).astype(BF16)
        vt = jnp.concatenate([kvbuf_ref[slot, g * SLC_TOPK + r, 1] for r in range(SLC_TOPK)], axis=1).astype(BF16)
        vis = jnp.zeros((1, nk), jnp.int32)
        for r in range(SLC_TOPK):
            j = idx_ref[b, g * SLC_TOPK + r]
            half = jnp.where(j < new_blk, j % bpp, -1)
            vis = jnp.where(lane // page == r, jnp.where((lane % page) // SLC_BLOCK == half, 1, 0), vis)
        ok = jnp.broadcast_to(vis > 0, (hpg, nk))
        o_s = attend_with_new(qg, _dot(qg, kt), ok, vt, new_row(2, g), new_row(3, g))
        kwt = win_ref[0, 0, g * hd:(g + 1) * hd, :].astype(BF16)
        vwt = win_ref[0, 1, g * hd:(g + 1) * hd, :].astype(BF16)
        wpos = past_len - w_buf + lax.broadcasted_iota(jnp.int32, (hpg, w_buf), 1)
        okw = (wpos <= t_pos) & (wpos > t_pos - WINDOW) & (wpos >= 0)
        o_w = attend_with_new(qg, _dot(qg, kwt), okw, vwt, new_row(4, g), new_row(5, g))
        for h in range(hpg):
            col = (g * hpg + h) * hd
            gc = g * GATE_ROWS + h
            o_h = (gates[:, gc:gc + 1] * oc_ref[0, :, col:col + hd]
                   + gates[:, gc + hpg:gc + hpg + 1] * o_s[h:h + 1]
                   + gates[:, gc + 2 * hpg:gc + 2 * hpg + 1] * o_w[h:h + 1])
            o_ref[0, :, col:col + hd] = o_h


def nsa_sample_attend(proj, o_c, kv_new, win_t, cache_t, page_table, idx, t_pos, past_len):
    B = proj.shape[0]
    G, hd = NSA_KV_GROUPS, NSA_HEAD_DIM
    nq = NSA_HEADS * hd
    page = cache_t.shape[3]
    row = lambda a: pl.BlockSpec((1, 1, a.shape[2]), lambda b, pt, ix: (b, 0, 0))
    grid_spec = pltpu.PrefetchScalarGridSpec(
        num_scalar_prefetch=2,
        grid=(B,),
        in_specs=[row(proj), row(o_c), row(kv_new),
                  pl.BlockSpec((1,) + win_t.shape[1:], lambda b, pt, ix: (b, 0, 0, 0)),
                  pl.BlockSpec(memory_space=pl.ANY)],
        out_specs=pl.BlockSpec((1, 1, nq), lambda b, pt, ix: (b, 0, 0)),
        scratch_shapes=[pltpu.VMEM((2, G * SLC_TOPK, 2, hd, page), F32),
                        pltpu.SemaphoreType.DMA((2,))],
    )
    return pl.pallas_call(
        functools.partial(_nsa_sample_attend_kernel, t_pos=t_pos, past_len=past_len),
        grid_spec=grid_spec,
        out_shape=jax.ShapeDtypeStruct((B, 1, nq), F32),
        compiler_params=_cparams("arbitrary"),
        name="nsa_sample_attend",
    )(page_table, idx, proj, o_c, kv_new, win_t, cache_t)


def kernel(x_prompt, x_sample, cache_nsa_kv, cache_win_kv, state_hgrn, page_table, norm_mix, norm_mlp, w_mlp_up, w_mlp_down, w_hgrn_in, hgrn_lb_logits, hgrn_onorm, w_hgrn_out, norm_kv, w_kv, cmp_pe_k, cmp_w1_k, cmp_w2_k, cmp_pe_v, cmp_w1_v, cmp_w2_v, w_nsa_q, w_nsa_out, norm_final):
    B, T, D = x_prompt.shape
    Bs, Ts, _ = x_sample.shape
    G, hd = NSA_KV_GROUPS, NSA_HEAD_DIM
    n_pool, page = cache_nsa_kv.shape[:2]
    past_len = page_table.shape[1] * page
    w_buf = cache_win_kv.shape[1]
    assert Ts == 1 and T % 1024 == 0 and T >= WINDOW + 256 and past_len % SLC_BLOCK == 0 and w_buf <= past_len

    wab, wbig, pe, w2, w2t = _cmp_weights(cmp_pe_k, cmp_w1_k, cmp_pe_v, cmp_w1_v, cmp_w2_k, cmp_w2_v)
    wq = [_permute_gate_cols(w_nsa_q[l]) for l in range(DEPTH - N_A)]

    tm = 1024
    x = x_prompt.reshape(B * T, D)
    states_p = jnp.zeros((N_A, B, HG_HEADS, HG_K, HG_V), F32)
    for l in range(DEPTH):
        if l == N_A:
            nsa_p, win_p, k_hm, vt_hm = kv_proj_prompt(x, norm_kv, w_kv, B, T)
            kc_p, vct_p = compress_prompt(nsa_p, wbig, wab, pe, w2, w2t)
        if l < N_A:
            proj = rms_proj(x, norm_mix[l], w_hgrn_in, 2 * tm, 512, layer=l).reshape(B, T, -1)
            o, states_p = hgrn_prompt(proj, hgrn_lb_logits, hgrn_onorm[l], l, states_p)
            x = proj_res(o.reshape(B * T, -1), w_hgrn_out, x, tm, layer=l)
        else:
            q_hm, gates_t = q_proj_prompt(x, norm_mix[l], wq[l - N_A], B, T)
            o = nsa_prompt(q_hm, gates_t, kc_p, vct_p, k_hm, vt_hm)
            x = proj_res(o.reshape(B * T, -1), w_nsa_out, x, tm, layer=l - N_A)
        x = mlp_res(x, norm_mlp[l], w_mlp_up, w_mlp_down, norm_final, tm, 1024, l == DEPTH - 1, l)
    y_prompt = x.reshape(B, T, D)
    nsa_kv_prompt = nsa_p.reshape(B, 4, G, hd, T).transpose(0, 4, 1, 2, 3)
    win_kv_prompt = win_p.reshape(B, 2, G, hd, T)[..., -min(WINDOW, T):].transpose(0, 4, 1, 2, 3)

    t_pos = past_len
    n_slc = -(-(past_len + 1) // SLC_BLOCK)
    xs = x_sample.reshape(Bs, D)
    cache_t = cache_nsa_kv.transpose(0, 2, 3, 4, 1).reshape(n_pool, 4, G * hd, page)
    win_t = cache_win_kv.transpose(0, 2, 3, 4, 1).reshape(Bs, 2, G * hd, w_buf)
    states_s = jnp.zeros(state_hgrn.shape, F32)
    for l in range(DEPTH):
        if l == N_A:
            kv_s = rms_proj(xs, norm_kv, w_kv, Bs, 512)
            cmp_s = compress_sample(cache_t, page_table, wbig, wab, pe, w2)
        if l < N_A:
            proj = rms_proj(xs, norm_mix[l], w_hgrn_in, Bs, 512, layer=l).reshape(Bs, 1, -1)
            o, states_s = hgrn_step(proj, state_hgrn, hgrn_lb_logits, hgrn_onorm[l], l, states_s)
        else:
            proj = rms_proj(xs, norm_mix[l], wq[l - N_A], Bs, 384).reshape(Bs, 1, -1)
            o_c, idx = nsa_sample_select(proj, cmp_s, t_pos, n_slc)
            o = nsa_sample_attend(proj, o_c, kv_s.reshape(Bs, 1, -1), win_t, cache_t, page_table,
                                  idx.reshape(Bs, G * SLC_TOPK), t_pos, past_len)
        w_o, lo = (w_hgrn_out, l) if l < N_A else (w_nsa_out, l - N_A)
        xs = proj_res(o.reshape(Bs, -1), w_o, xs, Bs, layer=lo)
        xs = mlp_res(xs, norm_mlp[l], w_mlp_up, w_mlp_down, norm_final, Bs, 512, l == DEPTH - 1, l)
    y_sample = xs.reshape(Bs, 1, D)
    n_nsa = 4 * G * hd
    nsa_kv_sample = kv_s[:, :n_nsa].reshape(Bs, 1, 4, G, hd)
    win_new = kv_s[:, n_nsa:].reshape(Bs, 1, 2, G, hd).astype(cache_win_kv.dtype)
    win_kv_sample = jnp.concatenate([cache_win_kv, win_new], axis=1)[:, -w_buf:]

    return (y_prompt, y_sample, nsa_kv_prompt, nsa_kv_sample, win_kv_prompt, win_kv_sample,
            states_p, states_s)
```

```python
import functools

import jax
import jax.numpy as jnp
from jax import lax
from jax.experimental import pallas as pl
from jax.experimental.pallas import tpu as pltpu

F32 = jnp.float32
BF16 = jnp.bfloat16

D_MODEL = 1024
DEPTH = 4
N_A = DEPTH // 2
D_FF = 4 * D_MODEL
RMS_EPS = 1e-6
HG_HEADS = 8
HG_K = 128
HG_V = 128
NSA_HEADS = 16
NSA_HEAD_DIM = 64
NSA_KV_GROUPS = 4
NSA_HPG = NSA_HEADS // NSA_KV_GROUPS
NSA_SCALE = NSA_HEAD_DIM ** -0.5
CMP_LEN = 32
CMP_STRIDE = 16
SLC_BLOCK = 64
SLC_TOPK = 16
WINDOW = 512
FORCED_SCORE = 1e4
NEG_INF = -1e30
TINY = 1e-30

HG_CHUNK = 128
HG_MATRIX_LEVELS = 2
VMEM_LIMIT = 56 * 1024 * 1024


def _cparams(*sem):
    return pltpu.CompilerParams(dimension_semantics=sem, vmem_limit_bytes=VMEM_LIMIT)


def _rms(x, g):
    return x * lax.rsqrt(jnp.mean(x * x, axis=-1, keepdims=True) + RMS_EPS) * g


def _sigmoid(x):
    return 1.0 / (1.0 + jnp.exp(-x))


def _dot(a, b):
    return jnp.dot(a, b, preferred_element_type=F32)


def _dot_nt(a, b):
    return lax.dot_general(a, b, (((1,), (1,)), ((), ())), preferred_element_type=F32)


def _dot_tn(a, b):
    return lax.dot_general(a, b, (((0,), (0,)), ((), ())), preferred_element_type=F32)


def _split3(x):
    hi = x.astype(BF16)
    r1 = x - hi.astype(F32)
    mid = r1.astype(BF16)
    lo = (r1 - mid.astype(F32)).astype(BF16)
    return hi, mid, lo


def _masked_softmax(s, mask):
    s = jnp.where(mask, s, NEG_INF)
    e = jnp.where(mask, jnp.exp(s - jnp.max(s, axis=-1, keepdims=True)), 0.0)
    return e / jnp.maximum(jnp.sum(e, axis=-1, keepdims=True), TINY)


def _rms_proj_kernel(x_ref, g_ref, w_ref, o_ref, y_ref):
    @pl.when(pl.program_id(1) == 0)
    def _():
        y_ref[...] = _rms(x_ref[...], g_ref[...]).astype(BF16)

    o_ref[...] = _dot(y_ref[...], w_ref[...].astype(BF16)).astype(o_ref.dtype)


def _stacked(w, layer):
    return (w[None], 0) if w.ndim == 2 else (w, layer)


def rms_proj(x, g, w, tm, tn, out_dtype=F32, layer=0):
    M, D = x.shape
    w, layer = _stacked(w, layer)
    N = w.shape[2]
    return pl.pallas_call(
        _rms_proj_kernel,
        grid=(M // tm, N // tn),
        in_specs=[pl.BlockSpec((tm, D), lambda i, j: (i, 0)),
                  pl.BlockSpec((1, D), lambda i, j: (0, 0)),
                  pl.BlockSpec((None, D, tn), lambda i, j: (layer, 0, j))],
        out_specs=pl.BlockSpec((tm, tn), lambda i, j: (i, j)),
        out_shape=jax.ShapeDtypeStruct((M, N), out_dtype),
        scratch_shapes=[pltpu.VMEM((tm, D), BF16)],
        compiler_params=_cparams("parallel", "arbitrary"),
        name="rms_proj",
    )(x, g.reshape(1, D), w)


def _proj_res_kernel(a_ref, w_ref, r_ref, o_ref, wb_ref):
    @pl.when(pl.program_id(0) == 0)
    def _():
        wb_ref[...] = w_ref[...].astype(BF16)

    o_ref[...] = r_ref[...] + _dot(a_ref[...].astype(BF16), wb_ref[...])


def proj_res(a, w, res, tm, layer=0):
    M, K = a.shape
    w, layer = _stacked(w, layer)
    N = w.shape[2]
    return pl.pallas_call(
        _proj_res_kernel,
        grid=(M // tm,),
        in_specs=[pl.BlockSpec((tm, K), lambda i: (i, 0)),
                  pl.BlockSpec((None, K, N), lambda i: (layer, 0, 0)),
                  pl.BlockSpec((tm, N), lambda i: (i, 0))],
        out_specs=pl.BlockSpec((tm, N), lambda i: (i, 0)),
        out_shape=jax.ShapeDtypeStruct((M, N), F32),
        scratch_shapes=[pltpu.VMEM((K, N), BF16)],
        compiler_params=_cparams("arbitrary"),
        name="proj_res",
    )(a, w, res)


def _mlp_kernel(x_ref, g_ref, wu_ref, wd_ref, gf_ref, o_ref, y_ref, acc_ref, *, final_norm):
    f = pl.program_id(1)

    @pl.when(f == 0)
    def _():
        y_ref[...] = _rms(x_ref[...], g_ref[...]).astype(BF16)
        acc_ref[...] = jnp.zeros_like(acc_ref)

    h = jnp.maximum(_dot(y_ref[...], wu_ref[...].astype(BF16)), 0.0)
    acc_ref[...] += _dot((h * h).astype(BF16), wd_ref[...].astype(BF16))

    @pl.when(f == pl.num_programs(1) - 1)
    def _():
        out = x_ref[...] + acc_ref[...]
        if final_norm:
            out = _rms(out, gf_ref[...])
        o_ref[...] = out


def mlp_res(x, g, w_up, w_down, g_final, tm, tf, final_norm, layer):
    M, D = x.shape
    Fdim = w_up.shape[2]
    return pl.pallas_call(
        functools.partial(_mlp_kernel, final_norm=final_norm),
        grid=(M // tm, Fdim // tf),
        in_specs=[pl.BlockSpec((tm, D), lambda i, f: (i, 0)),
                  pl.BlockSpec((1, D), lambda i, f: (0, 0)),
                  pl.BlockSpec((None, D, tf), lambda i, f: (layer, 0, f)),
                  pl.BlockSpec((None, tf, D), lambda i, f: (layer, f, 0)),
                  pl.BlockSpec((1, D), lambda i, f: (0, 0))],
        out_specs=pl.BlockSpec((tm, D), lambda i, f: (i, 0)),
        out_shape=jax.ShapeDtypeStruct((M, D), F32),
        scratch_shapes=[pltpu.VMEM((tm, D), BF16), pltpu.VMEM((tm, D), F32)],
        compiler_params=_cparams("parallel", "arbitrary"),
        name="mlp_res",
    )(x, g.reshape(1, D), w_up, w_down, g_final.reshape(1, D))


def _hgrn_lower_bound(lg, layer):
    m = jnp.max(lg, axis=0, keepdims=True)
    e = jnp.exp(lg - m)
    p = e / jnp.sum(e, axis=0, keepdims=True)
    lb = jnp.sum(p[1:layer + 1], axis=0, keepdims=True)
    return jnp.log(lb), jnp.log(1.0 - lb)


def _hgrn_logf(z, lg, layer):
    ls = jnp.minimum(z, 0.0) - jnp.log(1.0 + jnp.exp(-jnp.abs(z)))
    if layer == 0:
        return ls
    log_lb, log1m = _hgrn_lower_bound(lg, layer)
    b2 = log1m + ls
    return jnp.maximum(log_lb, b2) + jnp.log(1.0 + jnp.exp(-jnp.abs(log_lb - b2)))


def _hgrn_sum_matrices(C, n_lev):
    r = lax.broadcasted_iota(jnp.int32, (C, C), 0)
    u = lax.broadcasted_iota(jnp.int32, (C, C), 1)
    mats = [r >= u]
    for lev in range(n_lev):
        h = 1 << lev
        off = r & (2 * h - 1)
        mid = r - off + h
        mats.append(((off >= h) & (u >= mid) & (u <= r)) | ((off < h) & (u > r) & (u < mid)))
    return jnp.concatenate([jnp.where(m, 1.0, 0.0).astype(BF16) for m in mats], axis=0)


def _hgrn_kernel(zq_ref, zf_ref, zi_ref, zg_ref, lg_ref, on_ref, _states_in, o_ref, s_ref, st_ref, w_ref, b_ref,
                 *, layer, tc, nh):
    t = pl.program_id(2)
    C = HG_CHUNK
    n_lev = C.bit_length() - 1
    heads = range(nh)

    @pl.when(t == 0)
    def _():
        st_ref[...] = jnp.zeros_like(st_ref)
        w_ref[...] = _hgrn_sum_matrices(C, HG_MATRIX_LEVELS)

    r_i = lax.broadcasted_iota(jnp.int32, (C, C), 0)
    c_i = lax.broadcasted_iota(jnp.int32, (C, C), 1)
    row = lax.broadcasted_iota(jnp.int32, (C, HG_K), 0)

    def chunk(ci, sts):
        rows = pl.ds(pl.multiple_of(ci * C, C), C)
        kl = [slice(h * HG_K, (h + 1) * HG_K) for h in heads]
        vl = [slice(h * HG_V, (h + 1) * HG_V) for h in heads]
        q = [zq_ref[0, rows, kl[h]] for h in heads]
        vb = [zi_ref[0, rows, vl[h]].astype(BF16) for h in heads]
        logf = [_hgrn_logf(zf_ref[0, rows, kl[h]], lg_ref[:, kl[h]], layer) for h in heads]
        k = [1.0 - jnp.exp(logf[h]) for h in heads]
        parts = [_split3(logf[h]) for h in heads]
        w_sum, w_lev = w_ref[0:C, :], w_ref[C:, :]
        b = [_dot(w_sum, parts[h][0]) + _dot(w_sum, parts[h][1]) + _dot(w_sum, parts[h][2]) for h in heads]
        for h in heads:
            b_ref[h] = b[h]
        e_low = [jnp.exp(_dot(w_lev, parts[h][0]) + _dot(w_lev, parts[h][1])) for h in heads]
        a = [jnp.where(r_i == c_i, _dot_nt(q[h].astype(BF16), k[h].astype(BF16)), 0.0) for h in heads]
        for lev in range(n_lev):
            half = 1 << lev
            upper = (row & (2 * half - 1)) >= half
            same = (r_i >> (lev + 1)) == (c_i >> (lev + 1))
            if lev < HG_MATRIX_LEVELS:
                e = [e_low[h][lev * C:(lev + 1) * C] for h in heads]
            else:
                nblk = C // (2 * half)
                e = []
                for h in heads:
                    bm = b_ref[h, pl.ds(half - 1, nblk, stride=2 * half), :] if nblk > 1 else b_ref[h, half - 1:half, :]
                    bm = jnp.broadcast_to(bm[:, None, :], (nblk, 2 * half, HG_K)).reshape(C, HG_K)
                    e.append(jnp.exp(jnp.where(upper, b[h] - bm, bm - b[h])))
            qt = [jnp.where(upper, q[h] * e[h], 0.0).astype(BF16) for h in heads]
            kt = [jnp.where(upper, 0.0, k[h] * e[h]).astype(BF16) for h in heads]
            al = [_dot_nt(qt[h], kt[h]) for h in heads]
            a = [a[h] + (jnp.where(same, al[h], 0.0) if 2 * half < C else al[h]) for h in heads]
        qd = [(q[h] * jnp.exp(b[h])).astype(BF16) for h in heads]
        o = [_dot(a[h].astype(BF16), vb[h]) + _dot_nt(qd[h], sts[h].astype(BF16)) for h in heads]
        b_end = [b[h][C - 1:C] for h in heads]
        kd = [(k[h] * jnp.exp(b_end[h] - b[h])).astype(BF16) for h in heads]
        new = tuple(sts[h] * jnp.exp(b_end[h]) + _dot_tn(vb[h], kd[h]) for h in heads)
        for h in heads:
            oh = o[h] * lax.rsqrt(jnp.mean(o[h] * o[h], axis=-1, keepdims=True) + RMS_EPS) * on_ref[:, vl[h]]
            zg = zg_ref[0, rows, vl[h]]
            o_ref[0, rows, vl[h]] = (oh * (zg * _sigmoid(zg))).astype(o_ref.dtype)
        return new

    sts = lax.fori_loop(0, tc // C, chunk, tuple(st_ref[h] for h in heads))
    for h in heads:
        st_ref[h] = sts[h]

    @pl.when(t == pl.num_programs(2) - 1)
    def _():
        for h in heads:
            s_ref[0, 0, h] = st_ref[h].T


def hgrn_prompt(proj, lb_logits, onorm, layer, states, tc=1024, nh=8):
    B, T, _ = proj.shape
    H = HG_HEADS
    hp = H // nh
    n_mats = 1 + HG_MATRIX_LEVELS
    alias_spec, alias_arg, aliases = [pl.BlockSpec(memory_space=pl.ANY)], [states], {6: 1}
    return pl.pallas_call(
        functools.partial(_hgrn_kernel, layer=layer, tc=tc, nh=nh),
        grid=(B, hp, T // tc),
        in_specs=[pl.BlockSpec((1, tc, nh * HG_K), lambda b, h, t: (b, t, h)),
                  pl.BlockSpec((1, tc, nh * HG_K), lambda b, h, t: (b, t, hp + h)),
                  pl.BlockSpec((1, tc, nh * HG_V), lambda b, h, t: (b, t, 2 * hp + h)),
                  pl.BlockSpec((1, tc, nh * HG_V), lambda b, h, t: (b, t, 3 * hp + h)),
                  pl.BlockSpec((N_A, nh * HG_K), lambda b, h, t: (0, h)),
                  pl.BlockSpec((1, nh * HG_V), lambda b, h, t: (0, h))] + alias_spec,
        out_specs=[pl.BlockSpec((1, tc, nh * HG_V), lambda b, h, t: (b, t, h)),
                   pl.BlockSpec((1, 1, nh, HG_K, HG_V), lambda b, h, t: (layer, b, h, 0, 0))],
        out_shape=[jax.ShapeDtypeStruct((B, T, H * HG_V), BF16),
                   jax.ShapeDtypeStruct((N_A, B, H, HG_K, HG_V), F32)],
        input_output_aliases=aliases,
        scratch_shapes=[pltpu.VMEM((nh, HG_V, HG_K), F32),
                        pltpu.VMEM((n_mats * HG_CHUNK, HG_CHUNK), BF16),
                        pltpu.VMEM((nh, HG_CHUNK, HG_K), F32)],
        compiler_params=_cparams("parallel", "parallel", "arbitrary"),
        name="hgrn_prompt",
    )(proj, proj, proj, proj, lb_logits, onorm.reshape(1, H * HG_V), *alias_arg)


QK_LANES = 2 * NSA_HEAD_DIM
VT_ROWS = NSA_HEAD_DIM + 16


def _kv_proj_kernel(x_ref, g_ref, w_ref, nsa_ref, win_ref, k_ref, vt_ref, wb_ref, *, tpb):
    @pl.when(pl.program_id(0) == 0)
    def _():
        wb_ref[...] = w_ref[...].astype(BF16)

    G, hd = NSA_KV_GROUPS, NSA_HEAD_DIM
    tm = x_ref.shape[0]
    y = _rms(x_ref[...], g_ref[...]).astype(BF16)
    kv = _dot(y, wb_ref[...])
    n_nsa = nsa_ref.shape[1]
    cols_t = [kv[:, c:c + 2 * hd].T for c in range(0, kv.shape[1], 2 * hd)]
    for n, t in enumerate(cols_t):
        c = n * 2 * hd
        if c < n_nsa:
            nsa_ref[0, c:c + 2 * hd, :] = t
        else:
            win_ref[0, c - n_nsa:c - n_nsa + 2 * hd, :] = t
    lane = lax.broadcasted_iota(jnp.int32, (tm, QK_LANES), 1)
    blk = ((pl.program_id(0) % tpb) * tm + lax.broadcasted_iota(jnp.int32, (tm, QK_LANES), 0)) // SLC_BLOCK
    tails = (jnp.where(lane - hd == blk, 1.0, 0.0), jnp.zeros((tm, QK_LANES), F32))
    for n, kind in enumerate((2, 4)):
        for gp in range(G // 2):
            col = (kind * G + 2 * gp) * hd
            pair = kv[:, col:col + 2 * hd]
            for gl, src in enumerate((pair, pltpu.roll(pair, hd, axis=1))):
                k_ref[0, n * G + 2 * gp + gl] = jnp.where(lane < hd, src, tails[n]).astype(BF16)
    ones_row = jnp.where(lax.broadcasted_iota(jnp.int32, (VT_ROWS - hd, tm), 0) == 0, 1.0, 0.0).astype(BF16)
    for n, kind in enumerate((3, 5)):
        for gp in range(G // 2):
            t = cols_t[(kind * G + 2 * gp) * hd // (2 * hd)]
            for gl in range(2):
                vt_ref[0, n * G + 2 * gp + gl, 0:hd, :] = t[gl * hd:(gl + 1) * hd].astype(BF16)
                vt_ref[0, n * G + 2 * gp + gl, hd:, :] = ones_row


def kv_proj_prompt(x, g, w_kv, B, T, tm=512):
    M, D = x.shape
    N = w_kv.shape[1]
    G, hd = NSA_KV_GROUPS, NSA_HEAD_DIM
    n_nsa = 4 * G * hd
    tpb = T // tm
    assert T // SLC_BLOCK <= QK_LANES - hd
    return pl.pallas_call(
        functools.partial(_kv_proj_kernel, tpb=tpb),
        grid=(M // tm,),
        in_specs=[pl.BlockSpec((tm, D), lambda i: (i, 0)),
                  pl.BlockSpec((1, D), lambda i: (0, 0)),
                  pl.BlockSpec((D, N), lambda i: (0, 0))],
        out_specs=[pl.BlockSpec((1, n_nsa, tm), lambda i: (i // tpb, 0, i % tpb)),
                   pl.BlockSpec((1, N - n_nsa, tm), lambda i: (i // tpb, 0, i % tpb)),
                   pl.BlockSpec((1, 2 * G, tm, QK_LANES), lambda i: (i // tpb, 0, i % tpb, 0)),
                   pl.BlockSpec((1, 2 * G, VT_ROWS, tm), lambda i: (i // tpb, 0, 0, i % tpb))],
        out_shape=[jax.ShapeDtypeStruct((B, n_nsa, T), F32),
                   jax.ShapeDtypeStruct((B, N - n_nsa, T), F32),
                   jax.ShapeDtypeStruct((B, 2 * G, T, QK_LANES), BF16),
                   jax.ShapeDtypeStruct((B, 2 * G, VT_ROWS, T), BF16)],
        scratch_shapes=[pltpu.VMEM((D, N), BF16)],
        compiler_params=_cparams("arbitrary"),
        name="kv_proj",
    )(x, g.reshape(1, D), w_kv)


GATE_ROWS = 16
GATE_LANES = 128


def _q_proj_kernel(x_ref, g_ref, w_ref, q_ref, gt_ref, wb_ref):
    @pl.when(pl.program_id(0) == 0)
    def _():
        wb_ref[...] = w_ref[...].astype(BF16)

    y = _rms(x_ref[...], g_ref[...]).astype(BF16)
    pr = _dot(y, wb_ref[...])
    hd = NSA_HEAD_DIM
    nq = NSA_HEADS * hd
    low = lax.broadcasted_iota(jnp.int32, (pr.shape[0], QK_LANES), 1) < hd
    for hp in range(NSA_HEADS // 2):
        pair = pr[:, hp * 2 * hd:(hp + 1) * 2 * hd] * NSA_SCALE
        for hl, src in enumerate((pair, pltpu.roll(pair, hd, axis=1))):
            q_ref[0, 2 * hp + hl] = jnp.where(low, src, 0.0).astype(BF16)
    gates_t = _sigmoid(pr[:, nq:]).T
    for gi in range(NSA_KV_GROUPS):
        gt_ref[0, gi] = gates_t[gi * GATE_ROWS:(gi + 1) * GATE_ROWS]


def _permute_gate_cols(w_q):
    nq = NSA_HEADS * NSA_HEAD_DIM
    d = w_q.shape[0]
    wg = w_q[:, nq:].reshape(d, 3, NSA_KV_GROUPS, NSA_HPG).transpose(0, 2, 1, 3).reshape(d, NSA_KV_GROUPS, 3 * NSA_HPG)
    wg = jnp.pad(wg, ((0, 0), (0, 0), (0, GATE_ROWS - 3 * NSA_HPG))).reshape(d, NSA_KV_GROUPS * GATE_ROWS)
    wg = jnp.pad(wg, ((0, 0), (0, GATE_LANES - NSA_KV_GROUPS * GATE_ROWS)))
    return jnp.concatenate([w_q[:, :nq], wg], axis=1)


def q_proj_prompt(x, g, w_qp, B, T, tm=1024):
    M, D = x.shape
    N = w_qp.shape[1]
    tpb = T // tm
    return pl.pallas_call(
        _q_proj_kernel,
        grid=(M // tm,),
        in_specs=[pl.BlockSpec((tm, D), lambda i: (i, 0)),
                  pl.BlockSpec((1, D), lambda i: (0, 0)),
                  pl.BlockSpec((D, N), lambda i: (0, 0))],
        out_specs=[pl.BlockSpec((1, NSA_HEADS, tm, QK_LANES), lambda i: (i // tpb, 0, i % tpb, 0)),
                   pl.BlockSpec((1, NSA_KV_GROUPS, GATE_ROWS, tm), lambda i: (i // tpb, 0, 0, i % tpb))],
        out_shape=[jax.ShapeDtypeStruct((B, NSA_HEADS, T, QK_LANES), BF16),
                   jax.ShapeDtypeStruct((B, NSA_KV_GROUPS, GATE_ROWS, T), F32)],
        scratch_shapes=[pltpu.VMEM((D, N), BF16)],
        compiler_params=_cparams("arbitrary"),
        name="q_proj",
    )(x, g.reshape(1, D), w_qp)


def _cmp_weights(pe_k, w1_k, pe_v, w1_v, w2_k, w2_v):
    half = CMP_STRIDE * NSA_HEAD_DIM

    def ab(w1):
        return jnp.concatenate([w1[:half], w1[half:]], axis=1)

    def big(w1):
        w = w1.reshape(2, CMP_STRIDE, NSA_HEAD_DIM, -1)
        b = jnp.einsum("alds,gh->lgdhas", w, jnp.eye(2, dtype=w1.dtype))
        return b.reshape(CMP_STRIDE * 2 * NSA_HEAD_DIM, 2 * 2 * w.shape[-1])

    wab = jnp.stack([ab(w1_k), ab(w1_v)])
    wbig = jnp.stack([big(w1_k), big(w1_v)])
    pe = jnp.stack([pe_k.reshape(2, half), pe_v.reshape(2, half)])
    w2 = jnp.stack([w2_k, w2_v])
    w2t = jnp.stack([w2_k.T, w2_v.T])
    return wab, wbig, pe, w2, w2t


def _cmp_taps(x_ref, ns):
    return jnp.concatenate([x_ref[pl.ds(l, ns, stride=CMP_STRIDE), :].astype(BF16) for l in range(CMP_STRIDE)], axis=1)


def _cmp_hidden(ab, pe, wab):
    hd = NSA_HEAD_DIM
    n = ab.shape[0]
    nxt = pltpu.roll(ab, n - 1, axis=0)
    pt = _dot(pe.astype(BF16), wab)
    hid = ab[:, :hd] + nxt[:, hd:] + pt[0:1, :hd] + pt[1:2, hd:]
    return hid * _sigmoid(hid)


def _cmp_prompt_kernel(xk_ref, xv_ref, wbig_ref, wab_ref, pe_ref, w2_ref, w2t_ref, kc_ref, vct_ref, buf_ref):
    hd = NSA_HEAD_DIM
    ns = xk_ref.shape[2] // CMP_STRIDE
    for c, x_ref in enumerate((xk_ref, xv_ref)):
        wab = wab_ref[c].astype(BF16)
        buf_ref[...] = x_ref[0].T
        ab2 = _dot(_cmp_taps(buf_ref, ns), wbig_ref[c].astype(BF16))
        for gl in range(2):
            act = _cmp_hidden(ab2[:, gl * 2 * hd:(gl + 1) * 2 * hd], pe_ref[c], wab).astype(BF16)
            if c == 0:
                kc_ref[0, gl] = _dot(act, w2_ref[c].astype(BF16)).astype(kc_ref.dtype)
            else:
                vct_ref[0, gl] = _dot_nt(w2t_ref[c].astype(BF16), act).astype(vct_ref.dtype)


def compress_prompt(nsa_t, wbig, wab, pe, w2, w2t):
    B, _, T = nsa_t.shape
    G, hd = NSA_KV_GROUPS, NSA_HEAD_DIM
    ns = T // CMP_STRIDE
    w2 = jnp.pad(w2, ((0, 0), (0, 0), (0, QK_LANES - hd)))
    const = lambda a: pl.BlockSpec(a.shape, lambda b, gp: (0,) * a.ndim)
    return pl.pallas_call(
        _cmp_prompt_kernel,
        grid=(B, G // 2),
        in_specs=[pl.BlockSpec((1, 2 * hd, T), lambda b, gp: (b, gp, 0)),
                  pl.BlockSpec((1, 2 * hd, T), lambda b, gp: (b, G // 2 + gp, 0)),
                  const(wbig), const(wab), const(pe), const(w2), const(w2t)],
        out_specs=[pl.BlockSpec((1, 2, ns, QK_LANES), lambda b, gp: (b, gp, 0, 0)),
                   pl.BlockSpec((1, 2, hd, ns), lambda b, gp: (b, gp, 0, 0))],
        out_shape=[jax.ShapeDtypeStruct((B, G, ns, QK_LANES), BF16),
                   jax.ShapeDtypeStruct((B, G, hd, ns), BF16)],
        scratch_shapes=[pltpu.VMEM((T, 2 * hd), F32)],
        compiler_params=_cparams("parallel", "parallel"),
        name="compress_prompt",
    )(nsa_t, nsa_t, wbig, wab, pe, w2, w2t)


def _nsa_prompt_kernel(q_ref, gt_ref, kc_ref, vct_ref, ks_ref, vst_ref, kw_ref, vwt_ref, o_ref, *, tq):
    i = pl.program_id(2)
    s0 = i * tq
    hpg, hd = NSA_HPG, NSA_HEAD_DIM
    T = ks_ref.shape[2]
    n_cmp = kc_ref.shape[2]
    n_slc = T // SLC_BLOCK
    R = hpg * tq
    Q = q_ref[0].reshape(R, QK_LANES)
    tpos = s0 + lax.broadcasted_iota(jnp.int32, (1, R), 1) % tq

    sc = _dot_nt(kc_ref[0, 0], Q)
    ok_c = lax.broadcasted_iota(jnp.int32, (n_cmp, R), 0) * CMP_STRIDE + (CMP_LEN - 1) <= tpos
    sc = jnp.where(ok_c, sc, NEG_INF)
    e_c = jnp.where(ok_c, jnp.exp(sc - jnp.max(sc, axis=0, keepdims=True)), 0.0)
    p_c = e_c / jnp.maximum(jnp.sum(e_c, axis=0, keepdims=True), TINY)
    o_c = _dot(vct_ref[0, 0], p_c.astype(BF16))

    psum = p_c[:, 0:tq]
    for h in range(1, hpg):
        psum = psum + p_c[:, h * tq:(h + 1) * tq]
    sj = lax.broadcasted_iota(jnp.int32, (n_slc, n_cmp), 0) * SLC_BLOCK
    ci = lax.broadcasted_iota(jnp.int32, (n_slc, n_cmp), 1) * CMP_STRIDE
    ov = jnp.where((ci < sj + SLC_BLOCK) & (ci + CMP_LEN > sj), 1.0, 0.0).astype(BF16)
    hi, mid, lo = _split3(psum)
    imp = _dot(ov, hi) + _dot(ov, mid) + _dot(ov, lo)
    blk = lax.broadcasted_iota(jnp.int32, (n_slc, tq), 0)
    qpos = s0 + lax.broadcasted_iota(jnp.int32, (n_slc, tq), 1)
    cur = qpos // SLC_BLOCK
    forced = (blk == 0) | (blk == cur) | (blk == cur - 1)
    score = jnp.where(forced, FORCED_SCORE, jnp.where(blk <= cur, imp, -1.0))
    rank = jnp.zeros((n_slc, tq), F32)
    for j in range(n_slc):
        cj = score[j:j + 1, :]
        rank = rank + jnp.where((cj > score) | ((cj == score) & (blk > j)), 1.0, 0.0)
    bias = jnp.where(rank < SLC_TOPK, 0.0, NEG_INF)
    bias = jnp.concatenate([jnp.zeros((hd, tq), F32), bias, jnp.zeros((QK_LANES - hd - n_slc, tq), F32)], axis=0)
    bias_t = bias.T.astype(BF16)
    q_sel = Q + jnp.concatenate([bias_t] * hpg, axis=0)

    heads = range(hpg)
    q_heads = [q_sel[h * tq:(h + 1) * tq] for h in heads]
    tpos_h = tpos[:, :tq]

    hq = tq // 2

    def update(m, acc, s, vt):
        m_new = jnp.maximum(m, jnp.max(s, axis=0, keepdims=True))
        return m_new, jnp.exp(m - m_new) * acc + _dot(vt, jnp.exp(s - m_new).astype(BF16))

    def causal(s, k0, tp):
        return jnp.where(k0 + lax.broadcasted_iota(jnp.int32, s.shape, 0) <= tp, s, NEG_INF)

    def chunk(c, carry):
        k0 = pl.multiple_of(c * tq, tq)
        ks = ks_ref[0, 0, pl.ds(k0, tq), :]
        vst = vst_ref[0, 0, :, pl.ds(k0, tq)]
        s_heads = [_dot_nt(ks, q_heads[h]) for h in heads]
        return tuple(update(*carry[h], s_heads[h], vst) for h in heads)

    def diagonal_chunk(carry):
        k0 = pl.multiple_of(i * tq, tq)
        k1 = pl.multiple_of(i * tq + hq, hq)
        s_lo = [_dot_nt(ks_ref[0, 0, pl.ds(k0, hq), :], q_heads[h]) for h in heads]
        s_hi = [_dot_nt(ks_ref[0, 0, pl.ds(k1, hq), :], q_heads[h][hq:]) for h in heads]
        new = []
        for h in heads:
            m, acc = update(*carry[h], causal(s_lo[h], k0, tpos_h), vst_ref[0, 0, :, pl.ds(k0, hq)])
            m_b, acc_b = update(m[:, hq:], acc[:, hq:], causal(s_hi[h], k1, tpos_h[:, hq:]),
                                vst_ref[0, 0, :, pl.ds(k1, hq)])
            new.append((jnp.concatenate([m[:, :hq], m_b], axis=1), jnp.concatenate([acc[:, :hq], acc_b], axis=1)))
        return tuple(new)

    carry = tuple((jnp.full((1, tq), NEG_INF, F32), jnp.zeros((VT_ROWS, tq), F32)) for _ in heads)
    carry = diagonal_chunk(lax.fori_loop(0, i, chunk, carry))
    o_s = jnp.concatenate([acc[:hd] / jnp.maximum(acc[hd:hd + 1], TINY) for _, acc in carry], axis=1)

    ws = pl.multiple_of(jnp.maximum(s0 - WINDOW, 0), tq)
    n_w = WINDOW // tq + 1
    pieces = []
    for j in range(n_w - 1):
        pieces.append((pl.multiple_of(ws + j * tq, tq), tq, 0))
    pieces.append((pl.multiple_of(ws + (n_w - 1) * tq, tq), hq, 0))
    pieces.append((pl.multiple_of(ws + (n_w - 1) * tq + hq, hq), hq, hq))
    o_w = []
    for h in heads:
        q_h = Q[h * tq:(h + 1) * tq]
        sw = []
        for n, (k0, nk, t0) in enumerate(pieces):
            s = _dot_nt(kw_ref[0, 0, pl.ds(k0, nk), :], q_h[t0:])
            wpos = k0 + lax.broadcasted_iota(jnp.int32, s.shape, 0)
            tp = tpos_h[:, t0:]
            ok = (wpos <= tp) & (wpos > tp - WINDOW) if n == 0 else wpos <= tp
            sw.append(jnp.where(ok, s, NEG_INF))
        m_w = functools.reduce(jnp.maximum, [jnp.max(s, axis=0, keepdims=True) for s in sw[:-1]])
        m_w = jnp.concatenate([m_w[:, :hq], jnp.maximum(m_w[:, hq:], jnp.max(sw[-1], axis=0, keepdims=True))], axis=1)
        acc_w = sum(_dot(vwt_ref[0, 0, :, pl.ds(k0, nk)], jnp.exp(s - m_w).astype(BF16))
                    for (k0, nk, _), s in zip(pieces[:-1], sw[:-1]))
        k0, nk, _ = pieces[-1]
        tail = _dot(vwt_ref[0, 0, :, pl.ds(k0, nk)], jnp.exp(sw[-1] - m_w[:, hq:]).astype(BF16))
        acc_w = acc_w + jnp.concatenate([jnp.zeros((VT_ROWS, hq), F32), tail], axis=1)
        o_w.append(acc_w[:hd] / jnp.maximum(acc_w[hd:hd + 1], TINY))
    o_w = jnp.concatenate(o_w, axis=1)

    gt = gt_ref[0, 0]
    outs = []
    for h in range(hpg):
        cols = slice(h * tq, (h + 1) * tq)
        outs.append(gt[h:h + 1] * o_c[:, cols] + gt[hpg + h:hpg + h + 1] * o_s[:, cols]
                    + gt[2 * hpg + h:2 * hpg + h + 1] * o_w[:, cols])
    for pair in range(hpg // 2):
        both = jnp.concatenate(outs[2 * pair:2 * pair + 2], axis=0)
        o_ref[0, :, pair * 2 * hd:(pair + 1) * 2 * hd] = both.T.astype(o_ref.dtype)


def nsa_prompt(q_hm, gates_t, kc, vct, k_hm, vt_hm, tq=512):
    B, _, T, _ = q_hm.shape
    G, hpg, hd = NSA_KV_GROUPS, NSA_HPG, NSA_HEAD_DIM
    n_cmp = kc.shape[2]
    assert WINDOW % tq == 0 and T >= WINDOW + tq
    keys = lambda off: pl.BlockSpec((1, 1, T, QK_LANES), lambda b, g, i: (b, off + g, 0, 0))
    vals = lambda off: pl.BlockSpec((1, 1, VT_ROWS, T), lambda b, g, i: (b, off + g, 0, 0))
    return pl.pallas_call(
        functools.partial(_nsa_prompt_kernel, tq=tq),
        grid=(B, G, T // tq),
        in_specs=[pl.BlockSpec((1, hpg, tq, QK_LANES), lambda b, g, i: (b, g, i, 0)),
                  pl.BlockSpec((1, 1, GATE_ROWS, tq), lambda b, g, i: (b, g, 0, i)),
                  pl.BlockSpec((1, 1, n_cmp, QK_LANES), lambda b, g, i: (b, g, 0, 0)),
                  pl.BlockSpec((1, 1, hd, n_cmp), lambda b, g, i: (b, g, 0, 0)),
                  keys(0), vals(0), keys(G), vals(G)],
        out_specs=pl.BlockSpec((1, tq, hpg * hd), lambda b, g, i: (b, i, g)),
        out_shape=jax.ShapeDtypeStruct((B, T, NSA_HEADS * hd), BF16),
        compiler_params=_cparams("parallel", "parallel", "arbitrary"),
        name="nsa_prompt",
    )(q_hm, gates_t, kc, vct, k_hm, vt_hm, k_hm, vt_hm)


def _row_to_col(row):
    n = row.shape[1]
    eye = lax.broadcasted_iota(jnp.int32, (n, n), 0) == lax.broadcasted_iota(jnp.int32, (n, n), 1)
    return jnp.sum(jnp.where(eye, jnp.broadcast_to(row, (n, n)), 0.0), axis=-1, keepdims=True)


def _hgrn_step_kernel(z_ref, s0_ref, lg_ref, on_ref, _states_in, o_ref, s_ref, *, layer):
    s0_ref, s_ref = s0_ref.at[0], s_ref.at[0]
    hk = HG_HEADS * HG_K
    hv = HG_HEADS * HG_V
    for h in range(HG_HEADS):
        kl = slice(h * HG_K, (h + 1) * HG_K)
        vl = slice(h * HG_V, (h + 1) * HG_V)
        q = z_ref[0, :, kl]
        logf = _hgrn_logf(z_ref[0, :, hk + h * HG_K:hk + (h + 1) * HG_K], lg_ref[:, kl], layer)
        f = jnp.exp(logf)
        v = z_ref[0, :, 2 * hk + h * HG_V:2 * hk + (h + 1) * HG_V]
        zg = z_ref[0, :, 2 * hk + hv + h * HG_V:2 * hk + hv + (h + 1) * HG_V]
        s = _row_to_col(f) * s0_ref[0, h] + _row_to_col(1.0 - f) * v
        s_ref[0, h] = s
        o = jnp.sum(_row_to_col(q) * s, axis=0, keepdims=True)
        o = o * lax.rsqrt(jnp.mean(o * o, axis=-1, keepdims=True) + RMS_EPS) * on_ref[:, vl]
        o_ref[0, :, vl] = o * (zg * _sigmoid(zg))


def hgrn_step(proj, s0_all, lb_logits, onorm, layer, states):
    B = proj.shape[0]
    H = HG_HEADS
    alias_spec, alias_arg, aliases = [pl.BlockSpec(memory_space=pl.ANY)], [states], {4: 1}
    state_block = pl.BlockSpec((1, 1, H, HG_K, HG_V), lambda b: (layer, b, 0, 0, 0))
    return pl.pallas_call(
        functools.partial(_hgrn_step_kernel, layer=layer),
        grid=(B,),
        in_specs=[pl.BlockSpec((1, 1, proj.shape[2]), lambda b: (b, 0, 0)),
                  state_block,
                  pl.BlockSpec((N_A, H * HG_K), lambda b: (0, 0)),
                  pl.BlockSpec((1, H * HG_V), lambda b: (0, 0))] + alias_spec,
        out_specs=[pl.BlockSpec((1, 1, H * HG_V), lambda b: (b, 0, 0)), state_block],
        out_shape=[jax.ShapeDtypeStruct((B, 1, H * HG_V), F32),
                   jax.ShapeDtypeStruct(s0_all.shape, F32)],
        input_output_aliases=aliases,
        compiler_params=_cparams("parallel"),
        name="hgrn_step",
    )(proj, s0_all, lb_logits, onorm.reshape(1, H * HG_V), *alias_arg)


def _cmp_sample_kernel(pt_ref, cache_ref, wbig_ref, wab_ref, pe_ref, w2_ref, o_ref, raw_ref, buf_ref, wb_ref,
                       sem, *, n_pages):
    b = pl.program_id(0)
    nb = pl.num_programs(0)
    G, hd = NSA_KV_GROUPS, NSA_HEAD_DIM
    hp = n_pages // 2
    spp = cache_ref.shape[3] // CMP_STRIDE
    ns = n_pages * spp

    def page_copy(bb, half, p):
        src = cache_ref.at[pt_ref[bb, half * hp + p], pl.ds(0, 2)]
        return pltpu.make_async_copy(src, raw_ref.at[half, p], sem.at[half])

    def start_half(bb, half):
        def body(p, c):
            page_copy(bb, half, p).start()
            return c
        lax.fori_loop(0, hp, body, 0)

    def wait_half(bb, half):
        def body(p, c):
            page_copy(bb, half, p).wait()
            return c
        lax.fori_loop(0, hp, body, 0)

    page = cache_ref.shape[3]
    r_i = lax.broadcasted_iota(jnp.int32, (page, page), 0)
    c_i = lax.broadcasted_iota(jnp.int32, (page, page), 1)
    perm = jnp.where(c_i == (r_i % spp) * CMP_STRIDE + r_i // spp, 1.0, 0.0).astype(BF16)

    def to_token_rows(half):
        def body(p, c):
            for kind in range(2):
                moved = _dot_nt(raw_ref[half, p, kind].astype(BF16), perm)
                for gp in range(G // 2):
                    buf_ref[kind * (G // 2) + gp, half * hp + p] = moved[gp * 2 * hd:(gp + 1) * 2 * hd].T
            return c
        lax.fori_loop(0, hp, body, 0, unroll=4)

    @pl.when(b == 0)
    def _():
        start_half(0, 0)
        start_half(0, 1)
        wb_ref[...] = wbig_ref[...].astype(BF16)

    for half in range(2):
        wait_half(b, half)
        to_token_rows(half)

        @pl.when(b + 1 < nb)
        def _():
            start_half(b + 1, half)

    for c in range(2):
        wab = wab_ref[c].astype(BF16)
        w2 = w2_ref[c].astype(BF16)
        for gp in range(G // 2):
            cb = c * (G // 2) + gp
            x = jnp.concatenate(
                [buf_ref[cb, :, l * spp:(l + 1) * spp, :].reshape(ns, 2 * hd).astype(BF16)
                 for l in range(CMP_STRIDE)], axis=1)
            ab2 = _dot(x, wb_ref[c])
            for gl in range(2):
                act = _cmp_hidden(ab2[:, gl * 2 * hd:(gl + 1) * 2 * hd], pe_ref[c], wab)
                col = (c * G + 2 * gp + gl) * hd
                o_ref[0, :, col:col + hd] = _dot(act.astype(BF16), w2).astype(o_ref.dtype)


def compress_sample(cache_t, page_table, wbig, wab, pe, w2):
    B, n_pages = page_table.shape
    page = cache_t.shape[3]
    G, hd = NSA_KV_GROUPS, NSA_HEAD_DIM
    ns = n_pages * page // CMP_STRIDE
    assert n_pages % 2 == 0 and page == 2 * hd
    const = lambda a: pl.BlockSpec(a.shape, lambda b, pt: (0,) * a.ndim)
    grid_spec = pltpu.PrefetchScalarGridSpec(
        num_scalar_prefetch=1,
        grid=(B,),
        in_specs=[pl.BlockSpec(memory_space=pl.ANY), const(wbig), const(wab), const(pe), const(w2)],
        out_specs=pl.BlockSpec((1, ns, 2 * G * hd), lambda b, pt: (b, 0, 0)),
        scratch_shapes=[pltpu.VMEM((2, n_pages // 2, 2, G * hd, page), F32),
                        pltpu.VMEM((G, n_pages, page, 2 * hd), F32),
                        pltpu.VMEM(wbig.shape, BF16),
                        pltpu.SemaphoreType.DMA((2,))],
    )
    return pl.pallas_call(
        functools.partial(_cmp_sample_kernel, n_pages=n_pages),
        grid_spec=grid_spec,
        out_shape=jax.ShapeDtypeStruct((B, ns, 2 * G * hd), BF16),
        compiler_params=_cparams("arbitrary"),
        name="compress_sample",
    )(page_table, cache_t, wbig, wab, pe, w2)


def _group_queries(pr_ref, g):
    hd = NSA_HEAD_DIM
    rows = [pr_ref[0, :, (g * NSA_HPG + h) * hd:(g * NSA_HPG + h + 1) * hd] for h in range(NSA_HPG)]
    return jnp.concatenate(rows, axis=0) * NSA_SCALE


def _nsa_sample_select_kernel(pr_ref, cmp_ref, oc_ref, idx_ref, *, t_pos, n_slc, n_pad):
    G, hpg, hd = NSA_KV_GROUPS, NSA_HPG, NSA_HEAD_DIM
    n_cmp = cmp_ref.shape[1]
    cmp = cmp_ref[0]
    ci = lax.broadcasted_iota(jnp.int32, (n_cmp, n_pad), 0) * CMP_STRIDE
    sj = lax.broadcasted_iota(jnp.int32, (n_cmp, n_pad), 1) * SLC_BLOCK
    ov = jnp.where((ci < sj + SLC_BLOCK) & (ci + CMP_LEN > sj), 1.0, 0.0).astype(BF16)
    blk = lax.broadcasted_iota(jnp.int32, (1, n_pad), 1)
    cur = t_pos // SLC_BLOCK
    forced = (blk == 0) | (blk == cur) | (blk == cur - 1)
    jr = lax.broadcasted_iota(jnp.int32, (n_pad, n_pad), 0)
    jc = lax.broadcasted_iota(jnp.int32, (n_pad, n_pad), 1)
    for g in range(G):
        qg = _group_queries(pr_ref, g).astype(BF16)
        sc = _dot_nt(qg, cmp[:, g * hd:(g + 1) * hd])
        e_pos = lax.broadcasted_iota(jnp.int32, (hpg, n_cmp), 1) * CMP_STRIDE + (CMP_LEN - 1)
        p_c = _masked_softmax(sc, e_pos <= t_pos)
        o_c = _dot(p_c.astype(BF16), cmp[:, (G + g) * hd:(G + g + 1) * hd])
        for h in range(hpg):
            col = (g * hpg + h) * hd
            oc_ref[0, :, col:col + hd] = o_c[h:h + 1]
        hi, mid, lo = _split3(jnp.sum(p_c, axis=0, keepdims=True))
        imp = _dot(hi, ov) + _dot(mid, ov) + _dot(lo, ov)
        score = jnp.where(forced, FORCED_SCORE, jnp.where(blk <= cur, imp, -1.0))
        score = jnp.where(blk < n_slc, score, -2.0)
        col_s = _row_to_col(score)
        beats = (col_s > score) | ((col_s == score) & (jr < jc))
        rank = jnp.sum(jnp.where(beats, 1.0, 0.0), axis=0, keepdims=True)
        rr = lax.broadcasted_iota(jnp.int32, (SLC_TOPK, n_pad), 0).astype(F32)
        bsel = jnp.where(jnp.broadcast_to(rank, (SLC_TOPK, n_pad)) == rr,
                         lax.broadcasted_iota(jnp.int32, (SLC_TOPK, n_pad), 1).astype(F32), 0.0)
        idx_ref[0, g * SLC_TOPK:(g + 1) * SLC_TOPK, :] = jnp.sum(bsel, axis=-1, keepdims=True).astype(jnp.int32)


def nsa_sample_select(proj, cmp_s, t_pos, n_slc):
    B = proj.shape[0]
    n_pad = -(-n_slc // 128) * 128
    G = NSA_KV_GROUPS
    nq = NSA_HEADS * NSA_HEAD_DIM
    return pl.pallas_call(
        functools.partial(_nsa_sample_select_kernel, t_pos=t_pos, n_slc=n_slc, n_pad=n_pad),
        grid=(B,),
        in_specs=[pl.BlockSpec((1, 1, proj.shape[2]), lambda b: (b, 0, 0)),
                  pl.BlockSpec((1,) + cmp_s.shape[1:], lambda b: (b, 0, 0))],
        out_specs=[pl.BlockSpec((1, 1, nq), lambda b: (b, 0, 0)),
                   pl.BlockSpec((1, G * SLC_TOPK, 1), lambda b: (b, 0, 0))],
        out_shape=[jax.ShapeDtypeStruct((B, 1, nq), F32),
                   jax.ShapeDtypeStruct((B, G * SLC_TOPK, 1), jnp.int32)],
        compiler_params=_cparams("parallel"),
        name="nsa_sample_select",
    )(proj, cmp_s)


def _nsa_sample_attend_kernel(pt_ref, idx_ref, pr_ref, oc_ref, kvn_ref, win_ref, cache_ref, o_ref,
                              kvbuf_ref, sem, *, t_pos, past_len):
    b = pl.program_id(0)
    nb = pl.num_programs(0)
    G, hpg, hd = NSA_KV_GROUPS, NSA_HPG, NSA_HEAD_DIM
    n_sel = G * SLC_TOPK
    page = cache_ref.shape[3]
    bpp = page // SLC_BLOCK
    new_blk = past_len // SLC_BLOCK

    def blk_copy(bb, n):
        slot = bb % 2
        j = jnp.minimum(idx_ref[bb, n], new_blk - 1)
        pg = pt_ref[bb, j // bpp]
        rows = pl.ds(pl.multiple_of((n // SLC_TOPK) * hd, hd), hd)
        return pltpu.make_async_copy(cache_ref.at[pg, pl.ds(2, 2), rows], kvbuf_ref.at[slot, n], sem.at[slot])

    def start_all(bb):
        def body(n, c):
            blk_copy(bb, n).start()
            return c
        lax.fori_loop(0, n_sel, body, 0)

    def wait_all(bb):
        def body(n, c):
            blk_copy(bb, n).wait()
            return c
        lax.fori_loop(0, n_sel, body, 0)

    @pl.when(b == 0)
    def _():
        start_all(0)

    @pl.when(b + 1 < nb)
    def _():
        start_all(b + 1)

    wait_all(b)
    slot = b % 2

    nk = SLC_TOPK * page
    w_buf = win_ref.shape[3]
    nq = NSA_HEADS * hd
    gates = _sigmoid(pr_ref[0, :, nq:nq + GATE_LANES])
    kvn = kvn_ref[0]

    def new_row(kind, g):
        return kvn[:, (kind * G + g) * hd:(kind * G + g + 1) * hd].astype(BF16).astype(F32)

    def attend_with_new(qg, s, ok, vt, k_new, v_new):
        s_new = jnp.sum(qg.astype(F32) * k_new, axis=-1, keepdims=True)
        s = jnp.where(ok, s, NEG_INF)
        m = jnp.maximum(jnp.max(s, axis=-1, keepdims=True), s_new)
        e = jnp.where(ok, jnp.exp(s - m), 0.0)
        e_new = jnp.exp(s_new - m)
        den = jnp.maximum(jnp.sum(e, axis=-1, keepdims=True) + e_new, TINY)
        return (_dot_nt(e.astype(BF16), vt) + e_new * v_new) / den

    lane = lax.broadcasted_iota(jnp.int32, (1, nk), 1)
    for g in range(G):
        qg = _group_queries(pr_ref, g).astype(BF16)
        kt = jnp.concatenate([kvbuf_ref[slot, g * SLC_TOPK + r, 0] for r in range(SLC_TOPK)], axis=1).astype(BF16)
        vt = jnp.concatenate([kvbuf_ref[slot, g * SLC_TOPK + r, 1] for r in range(SLC_TOPK)], axis=1).astype(BF16)
        vis = jnp.zeros((1, nk), jnp.int32)
        for r in range(SLC_TOPK):
            j = idx_ref[b, g * SLC_TOPK + r]
            half = jnp.where(j < new_blk, j % bpp, -1)
            vis = jnp.where(lane // page == r, jnp.where((lane % page) // SLC_BLOCK == half, 1, 0), vis)
        ok = jnp.broadcast_to(vis > 0, (hpg, nk))
        o_s = attend_with_new(qg, _dot(qg, kt), ok, vt, new_row(2, g), new_row(3, g))
        kwt = win_ref[0, 0, g * hd:(g + 1) * hd, :].astype(BF16)
        vwt = win_ref[0, 1, g * hd:(g + 1) * hd, :].astype(BF16)
        wpos = past_len - w_buf + lax.broadcasted_iota(jnp.int32, (hpg, w_buf), 1)
        okw = (wpos <= t_pos) & (wpos > t_pos - WINDOW) & (wpos >= 0)
        o_w = attend_with_new(qg, _dot(qg, kwt), okw, vwt, new_row(4, g), new_row(5, g))
        for h in range(hpg):
            col = (g * hpg + h) * hd
            gc = g * GATE_ROWS + h
            o_h = (gates[:, gc:gc + 1] * oc_ref[0, :, col:col + hd]
                   + gates[:, gc + hpg:gc + hpg + 1] * o_s[h:h + 1]
                   + gates[:, gc + 2 * hpg:gc + 2 * hpg + 1] * o_w[h:h + 1])
            o_ref[0, :, col:col + hd] = o_h


def nsa_sample_attend(proj, o_c, kv_new, win_t, cache_t, page_table, idx, t_pos, past_len):
    B = proj.shape[0]
    G, hd = NSA_KV_GROUPS, NSA_HEAD_DIM
    nq = NSA_HEADS * hd
    page = cache_t.shape[3]
    row = lambda a: pl.BlockSpec((1, 1, a.shape[2]), lambda b, pt, ix: (b, 0, 0))
    grid_spec = pltpu.PrefetchScalarGridSpec(
        num_scalar_prefetch=2,
        grid=(B,),
        in_specs=[row(proj), row(o_c), row(kv_new),
                  pl.BlockSpec((1,) + win_t.shape[1:], lambda b, pt, ix: (b, 0, 0, 0)),
                  pl.BlockSpec(memory_space=pl.ANY)],
        out_specs=pl.BlockSpec((1, 1, nq), lambda b, pt, ix: (b, 0, 0)),
        scratch_shapes=[pltpu.VMEM((2, G * SLC_TOPK, 2, hd, page), F32),
                        pltpu.SemaphoreType.DMA((2,))],
    )
    return pl.pallas_call(
        functools.partial(_nsa_sample_attend_kernel, t_pos=t_pos, past_len=past_len),
        grid_spec=grid_spec,
        out_shape=jax.ShapeDtypeStruct((B, 1, nq), F32),
        compiler_params=_cparams("arbitrary"),
        name="nsa_sample_attend",
    )(page_table, idx, proj, o_c, kv_new, win_t, cache_t)


def kernel(x_prompt, x_sample, cache_nsa_kv, cache_win_kv, state_hgrn, page_table, norm_mix, norm_mlp, w_mlp_up, w_mlp_down, w_hgrn_in, hgrn_lb_logits, hgrn_onorm, w_hgrn_out, norm_kv, w_kv, cmp_pe_k, cmp_w1_k, cmp_w2_k, cmp_pe_v, cmp_w1_v, cmp_w2_v, w_nsa_q, w_nsa_out, norm_final):
    B, T, D = x_prompt.shape
    Bs, Ts, _ = x_sample.shape
    G, hd = NSA_KV_GROUPS, NSA_HEAD_DIM
    n_pool, page = cache_nsa_kv.shape[:2]
    past_len = page_table.shape[1] * page
    w_buf = cache_win_kv.shape[1]
    assert Ts == 1 and T % 1024 == 0 and T >= WINDOW + 256 and past_len % SLC_BLOCK == 0 and w_buf <= past_len

    wab, wbig, pe, w2, w2t = _cmp_weights(cmp_pe_k, cmp_w1_k, cmp_pe_v, cmp_w1_v, cmp_w2_k, cmp_w2_v)
    wq = [_permute_gate_cols(w_nsa_q[l]) for l in range(DEPTH - N_A)]

    tm = 1024
    x = x_prompt.reshape(B * T, D)
    states_p = jnp.zeros((N_A, B, HG_HEADS, HG_K, HG_V), F32)
    for l in range(DEPTH):
        if l == N_A:
            nsa_p, win_p, k_hm, vt_hm = kv_proj_prompt(x, norm_kv, w_kv, B, T)
            kc_p, vct_p = compress_prompt(nsa_p, wbig, wab, pe, w2, w2t)
        if l < N_A:
            proj = rms_proj(x, norm_mix[l], w_hgrn_in, 2 * tm, 512, layer=l).reshape(B, T, -1)
            o, states_p = hgrn_prompt(proj, hgrn_lb_logits, hgrn_onorm[l], l, states_p)
            x = proj_res(o.reshape(B * T, -1), w_hgrn_out, x, tm, layer=l)
        else:
            q_hm, gates_t = q_proj_prompt(x, norm_mix[l], wq[l - N_A], B, T)
            o = nsa_prompt(q_hm, gates_t, kc_p, vct_p, k_hm, vt_hm)
            x = proj_res(o.reshape(B * T, -1), w_nsa_out, x, tm, layer=l - N_A)
        x = mlp_res(x, norm_mlp[l], w_mlp_up, w_mlp_down, norm_final, tm, 1024, l == DEPTH - 1, l)
    y_prompt = x.reshape(B, T, D)
    nsa_kv_prompt = nsa_p.reshape(B, 4, G, hd, T).transpose(0, 4, 1, 2, 3)
    win_kv_prompt = win_p.reshape(B, 2, G, hd, T)[..., -min(WINDOW, T):].transpose(0, 4, 1, 2, 3)

    t_pos = past_len
    n_slc = -(-(past_len + 1) // SLC_BLOCK)
    xs = x_sample.reshape(Bs, D)
    cache_t = cache_nsa_kv.transpose(0, 2, 3, 4, 1).reshape(n_pool, 4, G * hd, page)
    win_t = cache_win_kv.transpose(0, 2, 3, 4, 1).reshape(Bs, 2, G * hd, w_buf)
    states_s = jnp.zeros(state_hgrn.shape, F32)
    for l in range(DEPTH):
        if l == N_A:
            kv_s = rms_proj(xs, norm_kv, w_kv, Bs, 512)
            cmp_s = compress_sample(cache_t, page_table, wbig, wab, pe, w2)
        if l < N_A:
            proj = rms_proj(xs, norm_mix[l], w_hgrn_in, Bs, 512, layer=l).reshape(Bs, 1, -1)
            o, states_s = hgrn_step(proj, state_hgrn, hgrn_lb_logits, hgrn_onorm[l], l, states_s)
        else:
            proj = rms_proj(xs, norm_mix[l], wq[l - N_A], Bs, 384).reshape(Bs, 1, -1)
            o_c, idx = nsa_sample_select(proj, cmp_s, t_pos, n_slc)
            o = nsa_sample_attend(proj, o_c, kv_s.reshape(Bs, 1, -1), win_t, cache_t, page_table,
                                  idx.reshape(Bs, G * SLC_TOPK), t_pos, past_len)
        w_o, lo = (w_hgrn_out, l) if l < N_A else (w_nsa_out, l - N_A)
        xs = proj_res(o.reshape(Bs, -1), w_o, xs, Bs, layer=lo)
        xs = mlp_res(xs, norm_mlp[l], w_mlp_up, w_mlp_down, norm_final, Bs, 512, l == DEPTH - 1, l)
    y_sample = xs.reshape(Bs, 1, D)
    n_nsa = 4 * G * hd
    nsa_kv_sample = kv_s[:, :n_nsa].reshape(Bs, 1, 4, G, hd)
    win_new = kv_s[:, n_nsa:].reshape(Bs, 1, 2, G, hd).astype(cache_win_kv.dtype)
    win_kv_sample = jnp.concatenate([cache_win_kv, win_new], axis=1)[:, -w_buf:]

    return (y_prompt, y_sample, nsa_kv_prompt, nsa_kv_sample, win_kv_prompt, win_kv_sample,
            states_p, states_s)
```

```python
import functools

import jax
import jax.numpy as jnp
from jax import lax
from jax.experimental import pallas as pl
from jax.experimental.pallas import tpu as pltpu

F32 = jnp.float32
BF16 = jnp.bfloat16

D_MODEL = 1024
DEPTH = 4
N_A = DEPTH // 2
D_FF = 4 * D_MODEL
RMS_EPS = 1e-6
HG_HEADS = 8
HG_K = 128
HG_V = 128
NSA_HEADS = 16
NSA_HEAD_DIM = 64
NSA_KV_GROUPS = 4
NSA_HPG = NSA_HEADS // NSA_KV_GROUPS
NSA_SCALE = NSA_HEAD_DIM ** -0.5
CMP_LEN = 32
CMP_STRIDE = 16
SLC_BLOCK = 64
SLC_TOPK = 16
WINDOW = 512
FORCED_SCORE = 1e4
NEG_INF = -1e30
TINY = 1e-30

HG_CHUNK = 128
HG_MATRIX_LEVELS = 2
VMEM_LIMIT = 56 * 1024 * 1024


def _cparams(*sem):
    return pltpu.CompilerParams(dimension_semantics=sem, vmem_limit_bytes=VMEM_LIMIT)


def _rms(x, g):
    return x * lax.rsqrt(jnp.mean(x * x, axis=-1, keepdims=True) + RMS_EPS) * g


def _sigmoid(x):
    return 1.0 / (1.0 + jnp.exp(-x))


def _dot(a, b):
    return jnp.dot(a, b, preferred_element_type=F32)


def _dot_nt(a, b):
    return lax.dot_general(a, b, (((1,), (1,)), ((), ())), preferred_element_type=F32)


def _dot_tn(a, b):
    return lax.dot_general(a, b, (((0,), (0,)), ((), ())), preferred_element_type=F32)


def _split3(x):
    hi = x.astype(BF16)
    r1 = x - hi.astype(F32)
    mid = r1.astype(BF16)
    lo = (r1 - mid.astype(F32)).astype(BF16)
    return hi, mid, lo


def _masked_softmax(s, mask):
    s = jnp.where(mask, s, NEG_INF)
    e = jnp.where(mask, jnp.exp(s - jnp.max(s, axis=-1, keepdims=True)), 0.0)
    return e / jnp.maximum(jnp.sum(e, axis=-1, keepdims=True), TINY)


def _rms_proj_kernel(x_ref, g_ref, w_ref, o_ref, y_ref):
    @pl.when(pl.program_id(1) == 0)
    def _():
        y_ref[...] = _rms(x_ref[...], g_ref[...]).astype(BF16)

    o_ref[...] = _dot(y_ref[...], w_ref[...].astype(BF16)).astype(o_ref.dtype)


def _stacked(w, layer):
    return (w[None], 0) if w.ndim == 2 else (w, layer)


def rms_proj(x, g, w, tm, tn, out_dtype=F32, layer=0):
    M, D = x.shape
    w, layer = _stacked(w, layer)
    N = w.shape[2]
    return pl.pallas_call(
        _rms_proj_kernel,
        grid=(M // tm, N // tn),
        in_specs=[pl.BlockSpec((tm, D), lambda i, j: (i, 0)),
                  pl.BlockSpec((1, D), lambda i, j: (0, 0)),
                  pl.BlockSpec((None, D, tn), lambda i, j: (layer, 0, j))],
        out_specs=pl.BlockSpec((tm, tn), lambda i, j: (i, j)),
        out_shape=jax.ShapeDtypeStruct((M, N), out_dtype),
        scratch_shapes=[pltpu.VMEM((tm, D), BF16)],
        compiler_params=_cparams("parallel", "arbitrary"),
        name="rms_proj",
    )(x, g.reshape(1, D), w)


def _proj_res_kernel(a_ref, w_ref, r_ref, o_ref, wb_ref):
    @pl.when(pl.program_id(0) == 0)
    def _():
        wb_ref[...] = w_ref[...].astype(BF16)

    o_ref[...] = r_ref[...] + _dot(a_ref[...].astype(BF16), wb_ref[...])


def proj_res(a, w, res, tm, layer=0):
    M, K = a.shape
    w, layer = _stacked(w, layer)
    N = w.shape[2]
    return pl.pallas_call(
        _proj_res_kernel,
        grid=(M // tm,),
        in_specs=[pl.BlockSpec((tm, K), lambda i: (i, 0)),
                  pl.BlockSpec((None, K, N), lambda i: (layer, 0, 0)),
                  pl.BlockSpec((tm, N), lambda i: (i, 0))],
        out_specs=pl.BlockSpec((tm, N), lambda i: (i, 0)),
        out_shape=jax.ShapeDtypeStruct((M, N), F32),
        scratch_shapes=[pltpu.VMEM((K, N), BF16)],
        compiler_params=_cparams("arbitrary"),
        name="proj_res",
    )(a, w, res)


def _mlp_kernel(x_ref, g_ref, wu_ref, wd_ref, gf_ref, o_ref, y_ref, acc_ref, *, final_norm):
    f = pl.program_id(1)

    @pl.when(f == 0)
    def _():
        y_ref[...] = _rms(x_ref[...], g_ref[...]).astype(BF16)
        acc_ref[...] = jnp.zeros_like(acc_ref)

    h = jnp.maximum(_dot(y_ref[...], wu_ref[...].astype(BF16)), 0.0)
    acc_ref[...] += _dot((h * h).astype(BF16), wd_ref[...].astype(BF16))

    @pl.when(f == pl.num_programs(1) - 1)
    def _():
        out = x_ref[...] + acc_ref[...]
        if final_norm:
            out = _rms(out, gf_ref[...])
        o_ref[...] = out


def mlp_res(x, g, w_up, w_down, g_final, tm, tf, final_norm, layer):
    M, D = x.shape
    Fdim = w_up.shape[2]
    return pl.pallas_call(
        functools.partial(_mlp_kernel, final_norm=final_norm),
        grid=(M // tm, Fdim // tf),
        in_specs=[pl.BlockSpec((tm, D), lambda i, f: (i, 0)),
                  pl.BlockSpec((1, D), lambda i, f: (0, 0)),
                  pl.BlockSpec((None, D, tf), lambda i, f: (layer, 0, f)),
                  pl.BlockSpec((None, tf, D), lambda i, f: (layer, f, 0)),
                  pl.BlockSpec((1, D), lambda i, f: (0, 0))],
        out_specs=pl.BlockSpec((tm, D), lambda i, f: (i, 0)),
        out_shape=jax.ShapeDtypeStruct((M, D), F32),
        scratch_shapes=[pltpu.VMEM((tm, D), BF16), pltpu.VMEM((tm, D), F32)],
        compiler_params=_cparams("parallel", "arbitrary"),
        name="mlp_res",
    )(x, g.reshape(1, D), w_up, w_down, g_final.reshape(1, D))


def _hgrn_lower_bound(lg, layer):
    m = jnp.max(lg, axis=0, keepdims=True)
    e = jnp.exp(lg - m)
    p = e / jnp.sum(e, axis=0, keepdims=True)
    lb = jnp.sum(p[1:layer + 1], axis=0, keepdims=True)
    return jnp.log(lb), jnp.log(1.0 - lb)


def _hgrn_logf(z, lg, layer):
    ls = jnp.minimum(z, 0.0) - jnp.log(1.0 + jnp.exp(-jnp.abs(z)))
    if layer == 0:
        return ls
    log_lb, log1m = _hgrn_lower_bound(lg, layer)
    b2 = log1m + ls
    return jnp.maximum(log_lb, b2) + jnp.log(1.0 + jnp.exp(-jnp.abs(log_lb - b2)))


def _hgrn_sum_matrices(C, n_lev):
    r = lax.broadcasted_iota(jnp.int32, (C, C), 0)
    u = lax.broadcasted_iota(jnp.int32, (C, C), 1)
    mats = [r >= u]
    for lev in range(n_lev):
        h = 1 << lev
        off = r & (2 * h - 1)
        mid = r - off + h
        mats.append(((off >= h) & (u >= mid) & (u <= r)) | ((off < h) & (u > r) & (u < mid)))
    return jnp.concatenate([jnp.where(m, 1.0, 0.0).astype(BF16) for m in mats], axis=0)


def _hgrn_kernel(zq_ref, zf_ref, zi_ref, zg_ref, lg_ref, on_ref, _states_in, o_ref, s_ref, st_ref, w_ref, b_ref,
                 *, layer, tc, nh):
    t = pl.program_id(2)
    C = HG_CHUNK
    n_lev = C.bit_length() - 1
    heads = range(nh)

    @pl.when(t == 0)
    def _():
        st_ref[...] = jnp.zeros_like(st_ref)
        w_ref[...] = _hgrn_sum_matrices(C, HG_MATRIX_LEVELS)

    r_i = lax.broadcasted_iota(jnp.int32, (C, C), 0)
    c_i = lax.broadcasted_iota(jnp.int32, (C, C), 1)
    row = lax.broadcasted_iota(jnp.int32, (C, HG_K), 0)

    def chunk(ci, sts):
        rows = pl.ds(pl.multiple_of(ci * C, C), C)
        kl = [slice(h * HG_K, (h + 1) * HG_K) for h in heads]
        vl = [slice(h * HG_V, (h + 1) * HG_V) for h in heads]
        q = [zq_ref[0, rows, kl[h]] for h in heads]
        vb = [zi_ref[0, rows, vl[h]].astype(BF16) for h in heads]
        logf = [_hgrn_logf(zf_ref[0, rows, kl[h]], lg_ref[:, kl[h]], layer) for h in heads]
        k = [1.0 - jnp.exp(logf[h]) for h in heads]
        parts = [_split3(logf[h]) for h in heads]
        w_sum, w_lev = w_ref[0:C, :], w_ref[C:, :]
        b = [_dot(w_sum, parts[h][0]) + _dot(w_sum, parts[h][1]) + _dot(w_sum, parts[h][2]) for h in heads]
        for h in heads:
            b_ref[h] = b[h]
        e_low = [jnp.exp(_dot(w_lev, parts[h][0]) + _dot(w_lev, parts[h][1])) for h in heads]
        a = [jnp.where(r_i == c_i, _dot_nt(q[h].astype(BF16), k[h].astype(BF16)), 0.0) for h in heads]
        for lev in range(n_lev):
            half = 1 << lev
            upper = (row & (2 * half - 1)) >= half
            same = (r_i >> (lev + 1)) == (c_i >> (lev + 1))
            if lev < HG_MATRIX_LEVELS:
                e = [e_low[h][lev * C:(lev + 1) * C] for h in heads]
            else:
                nblk = C // (2 * half)
                e = []
                for h in heads:
                    bm = b_ref[h, pl.ds(half - 1, nblk, stride=2 * half), :] if nblk > 1 else b_ref[h, half - 1:half, :]
                    bm = jnp.broadcast_to(bm[:, None, :], (nblk, 2 * half, HG_K)).reshape(C, HG_K)
                    e.append(jnp.exp(jnp.where(upper, b[h] - bm, bm - b[h])))
            qt = [jnp.where(upper, q[h] * e[h], 0.0).astype(BF16) for h in heads]
            kt = [jnp.where(upper, 0.0, k[h] * e[h]).astype(BF16) for h in heads]
            al = [_dot_nt(qt[h], kt[h]) for h in heads]
            a = [a[h] + (jnp.where(same, al[h], 0.0) if 2 * half < C else al[h]) for h in heads]
        qd = [(q[h] * jnp.exp(b[h])).astype(BF16) for h in heads]
        o = [_dot(a[h].astype(BF16), vb[h]) + _dot_nt(qd[h], sts[h].astype(BF16)) for h in heads]
        b_end = [b[h][C - 1:C] for h in heads]
        kd = [(k[h] * jnp.exp(b_end[h] - b[h])).astype(BF16) for h in heads]
        new = tuple(sts[h] * jnp.exp(b_end[h]) + _dot_tn(vb[h], kd[h]) for h in heads)
        for h in heads:
            oh = o[h] * lax.rsqrt(jnp.mean(o[h] * o[h], axis=-1, keepdims=True) + RMS_EPS) * on_ref[:, vl[h]]
            zg = zg_ref[0, rows, vl[h]]
            o_ref[0, rows, vl[h]] = (oh * (zg * _sigmoid(zg))).astype(o_ref.dtype)
        return new

    sts = lax.fori_loop(0, tc // C, chunk, tuple(st_ref[h] for h in heads))
    for h in heads:
        st_ref[h] = sts[h]

    @pl.when(t == pl.num_programs(2) - 1)
    def _():
        for h in heads:
            s_ref[0, 0, h] = st_ref[h].T


def hgrn_prompt(proj, lb_logits, onorm, layer, states, tc=1024, nh=8):
    B, T, _ = proj.shape
    H = HG_HEADS
    hp = H // nh
    n_mats = 1 + HG_MATRIX_LEVELS
    alias_spec, alias_arg, aliases = [pl.BlockSpec(memory_space=pl.ANY)], [states], {6: 1}
    return pl.pallas_call(
        functools.partial(_hgrn_kernel, layer=layer, tc=tc, nh=nh),
        grid=(B, hp, T // tc),
        in_specs=[pl.BlockSpec((1, tc, nh * HG_K), lambda b, h, t: (b, t, h)),
                  pl.BlockSpec((1, tc, nh * HG_K), lambda b, h, t: (b, t, hp + h)),
                  pl.BlockSpec((1, tc, nh * HG_V), lambda b, h, t: (b, t, 2 * hp + h)),
                  pl.BlockSpec((1, tc, nh * HG_V), lambda b, h, t: (b, t, 3 * hp + h)),
                  pl.BlockSpec((N_A, nh * HG_K), lambda b, h, t: (0, h)),
                  pl.BlockSpec((1, nh * HG_V), lambda b, h, t: (0, h))] + alias_spec,
        out_specs=[pl.BlockSpec((1, tc, nh * HG_V), lambda b, h, t: (b, t, h)),
                   pl.BlockSpec((1, 1, nh, HG_K, HG_V), lambda b, h, t: (layer, b, h, 0, 0))],
        out_shape=[jax.ShapeDtypeStruct((B, T, H * HG_V), BF16),
                   jax.ShapeDtypeStruct((N_A, B, H, HG_K, HG_V), F32)],
        input_output_aliases=aliases,
        scratch_shapes=[pltpu.VMEM((nh, HG_V, HG_K), F32),
                        pltpu.VMEM((n_mats * HG_CHUNK, HG_CHUNK), BF16),
                        pltpu.VMEM((nh, HG_CHUNK, HG_K), F32)],
        compiler_params=_cparams("parallel", "parallel", "arbitrary"),
        name="hgrn_prompt",
    )(proj, proj, proj, proj, lb_logits, onorm.reshape(1, H * HG_V), *alias_arg)


QK_LANES = 2 * NSA_HEAD_DIM
VT_ROWS = NSA_HEAD_DIM + 16


def _kv_proj_kernel(x_ref, g_ref, w_ref, nsa_ref, win_ref, k_ref, vt_ref, wb_ref, *, tpb):
    @pl.when(pl.program_id(0) == 0)
    def _():
        wb_ref[...] = w_ref[...].astype(BF16)

    G, hd = NSA_KV_GROUPS, NSA_HEAD_DIM
    tm = x_ref.shape[0]
    y = _rms(x_ref[...], g_ref[...]).astype(BF16)
    kv = _dot(y, wb_ref[...])
    n_nsa = nsa_ref.shape[1]
    cols_t = [kv[:, c:c + 2 * hd].T for c in range(0, kv.shape[1], 2 * hd)]
    for n, t in enumerate(cols_t):
        c = n * 2 * hd
        if c < n_nsa:
            nsa_ref[0, c:c + 2 * hd, :] = t
        else:
            win_ref[0, c - n_nsa:c - n_nsa + 2 * hd, :] = t
    lane = lax.broadcasted_iota(jnp.int32, (tm, QK_LANES), 1)
    blk = ((pl.program_id(0) % tpb) * tm + lax.broadcasted_iota(jnp.int32, (tm, QK_LANES), 0)) // SLC_BLOCK
    tails = (jnp.where(lane - hd == blk, 1.0, 0.0), jnp.zeros((tm, QK_LANES), F32))
    for n, kind in enumerate((2, 4)):
        for gp in range(G // 2):
            col = (kind * G + 2 * gp) * hd
            pair = kv[:, col:col + 2 * hd]
            for gl, src in enumerate((pair, pltpu.roll(pair, hd, axis=1))):
                k_ref[0, n * G + 2 * gp + gl] = jnp.where(lane < hd, src, tails[n]).astype(BF16)
    ones_row = jnp.where(lax.broadcasted_iota(jnp.int32, (VT_ROWS - hd, tm), 0) == 0, 1.0, 0.0).astype(BF16)
    for n, kind in enumerate((3, 5)):
        for gp in range(G // 2):
            t = cols_t[(kind * G + 2 * gp) * hd // (2 * hd)]
            for gl in range(2):
                vt_ref[0, n * G + 2 * gp + gl, 0:hd, :] = t[gl * hd:(gl + 1) * hd].astype(BF16)
                vt_ref[0, n * G + 2 * gp + gl, hd:, :] = ones_row


def kv_proj_prompt(x, g, w_kv, B, T, tm=512):
    M, D = x.shape
    N = w_kv.shape[1]
    G, hd = NSA_KV_GROUPS, NSA_HEAD_DIM
    n_nsa = 4 * G * hd
    tpb = T // tm
    assert T // SLC_BLOCK <= QK_LANES - hd
    return pl.pallas_call(
        functools.partial(_kv_proj_kernel, tpb=tpb),
        grid=(M // tm,),
        in_specs=[pl.BlockSpec((tm, D), lambda i: (i, 0)),
                  pl.BlockSpec((1, D), lambda i: (0, 0)),
                  pl.BlockSpec((D, N), lambda i: (0, 0))],
        out_specs=[pl.BlockSpec((1, n_nsa, tm), lambda i: (i // tpb, 0, i % tpb)),
                   pl.BlockSpec((1, N - n_nsa, tm), lambda i: (i // tpb, 0, i % tpb)),
                   pl.BlockSpec((1, 2 * G, tm, QK_LANES), lambda i: (i // tpb, 0, i % tpb, 0)),
                   pl.BlockSpec((1, 2 * G, VT_ROWS, tm), lambda i: (i // tpb, 0, 0, i % tpb))],
        out_shape=[jax.ShapeDtypeStruct((B, n_nsa, T), F32),
                   jax.ShapeDtypeStruct((B, N - n_nsa, T), F32),
                   jax.ShapeDtypeStruct((B, 2 * G, T, QK_LANES), BF16),
                   jax.ShapeDtypeStruct((B, 2 * G, VT_ROWS, T), BF16)],
        scratch_shapes=[pltpu.VMEM((D, N), BF16)],
        compiler_params=_cparams("arbitrary"),
        name="kv_proj",
    )(x, g.reshape(1, D), w_kv)


GATE_ROWS = 16
GATE_LANES = 128


def _q_proj_kernel(x_ref, g_ref, w_ref, q_ref, gt_ref, wb_ref):
    @pl.when(pl.program_id(0) == 0)
    def _():
        wb_ref[...] = w_ref[...].astype(BF16)

    y = _rms(x_ref[...], g_ref[...]).astype(BF16)
    pr = _dot(y, wb_ref[...])
    hd = NSA_HEAD_DIM
    nq = NSA_HEADS * hd
    low = lax.broadcasted_iota(jnp.int32, (pr.shape[0], QK_LANES), 1) < hd
    for hp in range(NSA_HEADS // 2):
        pair = pr[:, hp * 2 * hd:(hp + 1) * 2 * hd] * NSA_SCALE
        for hl, src in enumerate((pair, pltpu.roll(pair, hd, axis=1))):
            q_ref[0, 2 * hp + hl] = jnp.where(low, src, 0.0).astype(BF16)
    gates_t = _sigmoid(pr[:, nq:]).T
    for gi in range(NSA_KV_GROUPS):
        gt_ref[0, gi] = gates_t[gi * GATE_ROWS:(gi + 1) * GATE_ROWS]


def _permute_gate_cols(w_q):
    nq = NSA_HEADS * NSA_HEAD_DIM
    d = w_q.shape[0]
    wg = w_q[:, nq:].reshape(d, 3, NSA_KV_GROUPS, NSA_HPG).transpose(0, 2, 1, 3).reshape(d, NSA_KV_GROUPS, 3 * NSA_HPG)
    wg = jnp.pad(wg, ((0, 0), (0, 0), (0, GATE_ROWS - 3 * NSA_HPG))).reshape(d, NSA_KV_GROUPS * GATE_ROWS)
    wg = jnp.pad(wg, ((0, 0), (0, GATE_LANES - NSA_KV_GROUPS * GATE_ROWS)))
    return jnp.concatenate([w_q[:, :nq], wg], axis=1)


def q_proj_prompt(x, g, w_qp, B, T, tm=1024):
    M, D = x.shape
    N = w_qp.shape[1]
    tpb = T // tm
    return pl.pallas_call(
        _q_proj_kernel,
        grid=(M // tm,),
        in_specs=[pl.BlockSpec((tm, D), lambda i: (i, 0)),
                  pl.BlockSpec((1, D), lambda i: (0, 0)),
                  pl.BlockSpec((D, N), lambda i: (0, 0))],
        out_specs=[pl.BlockSpec((1, NSA_HEADS, tm, QK_LANES), lambda i: (i // tpb, 0, i % tpb, 0)),
                   pl.BlockSpec((1, NSA_KV_GROUPS, GATE_ROWS, tm), lambda i: (i // tpb, 0, 0, i % tpb))],
        out_shape=[jax.ShapeDtypeStruct((B, NSA_HEADS, T, QK_LANES), BF16),
                   jax.ShapeDtypeStruct((B, NSA_KV_GROUPS, GATE_ROWS, T), F32)],
        scratch_shapes=[pltpu.VMEM((D, N), BF16)],
        compiler_params=_cparams("arbitrary"),
        name="q_proj",
    )(x, g.reshape(1, D), w_qp)


def _cmp_weights(pe_k, w1_k, pe_v, w1_v, w2_k, w2_v):
    half = CMP_STRIDE * NSA_HEAD_DIM

    def ab(w1):
        return jnp.concatenate([w1[:half], w1[half:]], axis=1)

    def big(w1):
        w = w1.reshape(2, CMP_STRIDE, NSA_HEAD_DIM, -1)
        b = jnp.einsum("alds,gh->lgdhas", w, jnp.eye(2, dtype=w1.dtype))
        return b.reshape(CMP_STRIDE * 2 * NSA_HEAD_DIM, 2 * 2 * w.shape[-1])

    wab = jnp.stack([ab(w1_k), ab(w1_v)])
    wbig = jnp.stack([big(w1_k), big(w1_v)])
    pe = jnp.stack([pe_k.reshape(2, half), pe_v.reshape(2, half)])
    w2 = jnp.stack([w2_k, w2_v])
    w2t = jnp.stack([w2_k.T, w2_v.T])
    return wab, wbig, pe, w2, w2t


def _cmp_taps(x_ref, ns):
    return jnp.concatenate([x_ref[pl.ds(l, ns, stride=CMP_STRIDE), :].astype(BF16) for l in range(CMP_STRIDE)], axis=1)


def _cmp_hidden(ab, pe, wab):
    hd = NSA_HEAD_DIM
    n = ab.shape[0]
    nxt = pltpu.roll(ab, n - 1, axis=0)
    pt = _dot(pe.astype(BF16), wab)
    hid = ab[:, :hd] + nxt[:, hd:] + pt[0:1, :hd] + pt[1:2, hd:]
    return hid * _sigmoid(hid)


def _cmp_prompt_kernel(xk_ref, xv_ref, wbig_ref, wab_ref, pe_ref, w2_ref, w2t_ref, kc_ref, vct_ref, buf_ref):
    hd = NSA_HEAD_DIM
    ns = xk_ref.shape[2] // CMP_STRIDE
    for c, x_ref in enumerate((xk_ref, xv_ref)):
        wab = wab_ref[c].astype(BF16)
        buf_ref[...] = x_ref[0].T
        ab2 = _dot(_cmp_taps(buf_ref, ns), wbig_ref[c].astype(BF16))
        for gl in range(2):
            act = _cmp_hidden(ab2[:, gl * 2 * hd:(gl + 1) * 2 * hd], pe_ref[c], wab).astype(BF16)
            if c == 0:
                kc_ref[0, gl] = _dot(act, w2_ref[c].astype(BF16)).astype(kc_ref.dtype)
            else:
                vct_ref[0, gl] = _dot_nt(w2t_ref[c].astype(BF16), act).astype(vct_ref.dtype)


def compress_prompt(nsa_t, wbig, wab, pe, w2, w2t):
    B, _, T = nsa_t.shape
    G, hd = NSA_KV_GROUPS, NSA_HEAD_DIM
    ns = T // CMP_STRIDE
    w2 = jnp.pad(w2, ((0, 0), (0, 0), (0, QK_LANES - hd)))
    const = lambda a: pl.BlockSpec(a.shape, lambda b, gp: (0,) * a.ndim)
    return pl.pallas_call(
        _cmp_prompt_kernel,
        grid=(B, G // 2),
        in_specs=[pl.BlockSpec((1, 2 * hd, T), lambda b, gp: (b, gp, 0)),
                  pl.BlockSpec((1, 2 * hd, T), lambda b, gp: (b, G // 2 + gp, 0)),
                  const(wbig), const(wab), const(pe), const(w2), const(w2t)],
        out_specs=[pl.BlockSpec((1, 2, ns, QK_LANES), lambda b, gp: (b, gp, 0, 0)),
                   pl.BlockSpec((1, 2, hd, ns), lambda b, gp: (b, gp, 0, 0))],
        out_shape=[jax.ShapeDtypeStruct((B, G, ns, QK_LANES), BF16),
                   jax.ShapeDtypeStruct((B, G, hd, ns), BF16)],
        scratch_shapes=[pltpu.VMEM((T, 2 * hd), F32)],
        compiler_params=_cparams("parallel", "parallel"),
        name="compress_prompt",
    )(nsa_t, nsa_t, wbig, wab, pe, w2, w2t)


def _nsa_prompt_kernel(q_ref, gt_ref, kc_ref, vct_ref, ks_ref, vst_ref, kw_ref, vwt_ref, o_ref, *, tq):
    i = pl.program_id(2)
    s0 = i * tq
    hpg, hd = NSA_HPG, NSA_HEAD_DIM
    T = ks_ref.shape[2]
    n_cmp = kc_ref.shape[2]
    n_slc = T // SLC_BLOCK
    R = hpg * tq
    Q = q_ref[0].reshape(R, QK_LANES)
    tpos = s0 + lax.broadcasted_iota(jnp.int32, (1, R), 1) % tq

    sc = _dot_nt(kc_ref[0, 0], Q)
    ok_c = lax.broadcasted_iota(jnp.int32, (n_cmp, R), 0) * CMP_STRIDE + (CMP_LEN - 1) <= tpos
    sc = jnp.where(ok_c, sc, NEG_INF)
    e_c = jnp.where(ok_c, jnp.exp(sc - jnp.max(sc, axis=0, keepdims=True)), 0.0)
    p_c = e_c / jnp.maximum(jnp.sum(e_c, axis=0, keepdims=True), TINY)
    o_c = _dot(vct_ref[0, 0], p_c.astype(BF16))

    psum = p_c[:, 0:tq]
    for h in range(1, hpg):
        psum = psum + p_c[:, h * tq:(h + 1) * tq]
    sj = lax.broadcasted_iota(jnp.int32, (n_slc, n_cmp), 0) * SLC_BLOCK
    ci = lax.broadcasted_iota(jnp.int32, (n_slc, n_cmp), 1) * CMP_STRIDE
    ov = jnp.where((ci < sj + SLC_BLOCK) & (ci + CMP_LEN > sj), 1.0, 0.0).astype(BF16)
    hi, mid, lo = _split3(psum)
    imp = _dot(ov, hi) + _dot(ov, mid) + _dot(ov, lo)
    blk = lax.broadcasted_iota(jnp.int32, (n_slc, tq), 0)
    qpos = s0 + lax.broadcasted_iota(jnp.int32, (n_slc, tq), 1)
    cur = qpos // SLC_BLOCK
    forced = (blk == 0) | (blk == cur) | (blk == cur - 1)
    score = jnp.where(forced, FORCED_SCORE, jnp.where(blk <= cur, imp, -1.0))
    rank = jnp.zeros((n_slc, tq), F32)
    for j in range(n_slc):
        cj = score[j:j + 1, :]
        rank = rank + jnp.where((cj > score) | ((cj == score) & (blk > j)), 1.0, 0.0)
    bias = jnp.where(rank < SLC_TOPK, 0.0, NEG_INF)
    bias = jnp.concatenate([jnp.zeros((hd, tq), F32), bias, jnp.zeros((QK_LANES - hd - n_slc, tq), F32)], axis=0)
    bias_t = bias.T.astype(BF16)
    q_sel = Q + jnp.concatenate([bias_t] * hpg, axis=0)

    heads = range(hpg)
    q_heads = [q_sel[h * tq:(h + 1) * tq] for h in heads]
    tpos_h = tpos[:, :tq]

    hq = tq // 2

    def update(m, acc, s, vt):
        m_new = jnp.maximum(m, jnp.max(s, axis=0, keepdims=True))
        return m_new, jnp.exp(m - m_new) * acc + _dot(vt, jnp.exp(s - m_new).astype(BF16))

    def causal(s, k0, tp):
        return jnp.where(k0 + lax.broadcasted_iota(jnp.int32, s.shape, 0) <= tp, s, NEG_INF)

    def chunk(c, carry):
        k0 = pl.multiple_of(c * tq, tq)
        ks = ks_ref[0, 0, pl.ds(k0, tq), :]
        vst = vst_ref[0, 0, :, pl.ds(k0, tq)]
        s_heads = [_dot_nt(ks, q_heads[h]) for h in heads]
        return tuple(update(*carry[h], s_heads[h], vst) for h in heads)

    def diagonal_chunk(carry):
        k0 = pl.multiple_of(i * tq, tq)
        k1 = pl.multiple_of(i * tq + hq, hq)
        s_lo = [_dot_nt(ks_ref[0, 0, pl.ds(k0, hq), :], q_heads[h]) for h in heads]
        s_hi = [_dot_nt(ks_ref[0, 0, pl.ds(k1, hq), :], q_heads[h][hq:]) for h in heads]
        new = []
        for h in heads:
            m, acc = update(*carry[h], causal(s_lo[h], k0, tpos_h), vst_ref[0, 0, :, pl.ds(k0, hq)])
            m_b, acc_b = update(m[:, hq:], acc[:, hq:], causal(s_hi[h], k1, tpos_h[:, hq:]),
                                vst_ref[0, 0, :, pl.ds(k1, hq)])
            new.append((jnp.concatenate([m[:, :hq], m_b], axis=1), jnp.concatenate([acc[:, :hq], acc_b], axis=1)))
        return tuple(new)

    carry = tuple((jnp.full((1, tq), NEG_INF, F32), jnp.zeros((VT_ROWS, tq), F32)) for _ in heads)
    carry = diagonal_chunk(lax.fori_loop(0, i, chunk, carry))
    o_s = jnp.concatenate([acc[:hd] / jnp.maximum(acc[hd:hd + 1], TINY) for _, acc in carry], axis=1)

    ws = pl.multiple_of(jnp.maximum(s0 - WINDOW, 0), tq)
    n_w = WINDOW // tq + 1
    pieces = []
    for j in range(n_w - 1):
        pieces.append((pl.multiple_of(ws + j * tq, tq), tq, 0))
    pieces.append((pl.multiple_of(ws + (n_w - 1) * tq, tq), hq, 0))
    pieces.append((pl.multiple_of(ws + (n_w - 1) * tq + hq, hq), hq, hq))
    o_w = []
    for h in heads:
        q_h = Q[h * tq:(h + 1) * tq]
        sw = []
        for n, (k0, nk, t0) in enumerate(pieces):
            s = _dot_nt(kw_ref[0, 0, pl.ds(k0, nk), :], q_h[t0:])
            wpos = k0 + lax.broadcasted_iota(jnp.int32, s.shape, 0)
            tp = tpos_h[:, t0:]
            ok = (wpos <= tp) & (wpos > tp - WINDOW) if n == 0 else wpos <= tp
            sw.append(jnp.where(ok, s, NEG_INF))
        m_w = functools.reduce(jnp.maximum, [jnp.max(s, axis=0, keepdims=True) for s in sw[:-1]])
        m_w = jnp.concatenate([m_w[:, :hq], jnp.maximum(m_w[:, hq:], jnp.max(sw[-1], axis=0, keepdims=True))], axis=1)
        acc_w = sum(_dot(vwt_ref[0, 0, :, pl.ds(k0, nk)], jnp.exp(s - m_w).astype(BF16))
                    for (k0, nk, _), s in zip(pieces[:-1], sw[:-1]))
        k0, nk, _ = pieces[-1]
        tail = _dot(vwt_ref[0, 0, :, pl.ds(k0, nk)], jnp.exp(sw[-1] - m_w[:, hq:]).astype(BF16))
        acc_w = acc_w + jnp.concatenate([jnp.zeros((VT_ROWS, hq), F32), tail], axis=1)
        o_w.append(acc_w[:hd] / jnp.maximum(acc_w[hd:hd + 1], TINY))
    o_w = jnp.concatenate(o_w, axis=1)

    gt = gt_ref[0, 0]
    outs = []
    for h in range(hpg):
        cols = slice(h * tq, (h + 1) * tq)
        outs.append(gt[h:h + 1] * o_c[:, cols] + gt[hpg + h:hpg + h + 1] * o_s[:, cols]
                    + gt[2 * hpg + h:2 * hpg + h + 1] * o_w[:, cols])
    for pair in range(hpg // 2):
        both = jnp.concatenate(outs[2 * pair:2 * pair + 2], axis=0)
        o_ref[0, :, pair * 2 * hd:(pair + 1) * 2 * hd] = both.T.astype(o_ref.dtype)


def nsa_prompt(q_hm, gates_t, kc, vct, k_hm, vt_hm, tq=512):
    B, _, T, _ = q_hm.shape
    G, hpg, hd = NSA_KV_GROUPS, NSA_HPG, NSA_HEAD_DIM
    n_cmp = kc.shape[2]
    assert WINDOW % tq == 0 and T >= WINDOW + tq
    keys = lambda off: pl.BlockSpec((1, 1, T, QK_LANES), lambda b, g, i: (b, off + g, 0, 0))
    vals = lambda off: pl.BlockSpec((1, 1, VT_ROWS, T), lambda b, g, i: (b, off + g, 0, 0))
    return pl.pallas_call(
        functools.partial(_nsa_prompt_kernel, tq=tq),
        grid=(B, G, T // tq),
        in_specs=[pl.BlockSpec((1, hpg, tq, QK_LANES), lambda b, g, i: (b, g, i, 0)),
                  pl.BlockSpec((1, 1, GATE_ROWS, tq), lambda b, g, i: (b, g, 0, i)),
                  pl.BlockSpec((1, 1, n_cmp, QK_LANES), lambda b, g, i: (b, g, 0, 0)),
                  pl.BlockSpec((1, 1, hd, n_cmp), lambda b, g, i: (b, g, 0, 0)),
                  keys(0), vals(0), keys(G), vals(G)],
        out_specs=pl.BlockSpec((1, tq, hpg * hd), lambda b, g, i: (b, i, g)),
        out_shape=jax.ShapeDtypeStruct((B, T, NSA_HEADS * hd), BF16),
        compiler_params=_cparams("parallel", "parallel", "arbitrary"),
        name="nsa_prompt",
    )(q_hm, gates_t, kc, vct, k_hm, vt_hm, k_hm, vt_hm)


def _row_to_col(row):
    n = row.shape[1]
    eye = lax.broadcasted_iota(jnp.int32, (n, n), 0) == lax.broadcasted_iota(jnp.int32, (n, n), 1)
    return jnp.sum(jnp.where(eye, jnp.broadcast_to(row, (n, n)), 0.0), axis=-1, keepdims=True)


def _hgrn_step_kernel(z_ref, s0_ref, lg_ref, on_ref, _states_in, o_ref, s_ref, *, layer):
    s0_ref, s_ref = s0_ref.at[0], s_ref.at[0]
    hk = HG_HEADS * HG_K
    hv = HG_HEADS * HG_V
    for h in range(HG_HEADS):
        kl = slice(h * HG_K, (h + 1) * HG_K)
        vl = slice(h * HG_V, (h + 1) * HG_V)
        q = z_ref[0, :, kl]
        logf = _hgrn_logf(z_ref[0, :, hk + h * HG_K:hk + (h + 1) * HG_K], lg_ref[:, kl], layer)
        f = jnp.exp(logf)
        v = z_ref[0, :, 2 * hk + h * HG_V:2 * hk + (h + 1) * HG_V]
        zg = z_ref[0, :, 2 * hk + hv + h * HG_V:2 * hk + hv + (h + 1) * HG_V]
        s = _row_to_col(f) * s0_ref[0, h] + _row_to_col(1.0 - f) * v
        s_ref[0, h] = s
        o = jnp.sum(_row_to_col(q) * s, axis=0, keepdims=True)
        o = o * lax.rsqrt(jnp.mean(o * o, axis=-1, keepdims=True) + RMS_EPS) * on_ref[:, vl]
        o_ref[0, :, vl] = o * (zg * _sigmoid(zg))


def hgrn_step(proj, s0_all, lb_logits, onorm, layer, states):
    B = proj.shape[0]
    H = HG_HEADS
    alias_spec, alias_arg, aliases = [pl.BlockSpec(memory_space=pl.ANY)], [states], {4: 1}
    state_block = pl.BlockSpec((1, 1, H, HG_K, HG_V), lambda b: (layer, b, 0, 0, 0))
    return pl.pallas_call(
        functools.partial(_hgrn_step_kernel, layer=layer),
        grid=(B,),
        in_specs=[pl.BlockSpec((1, 1, proj.shape[2]), lambda b: (b, 0, 0)),
                  state_block,
                  pl.BlockSpec((N_A, H * HG_K), lambda b: (0, 0)),
                  pl.BlockSpec((1, H * HG_V), lambda b: (0, 0))] + alias_spec,
        out_specs=[pl.BlockSpec((1, 1, H * HG_V), lambda b: (b, 0, 0)), state_block],
        out_shape=[jax.ShapeDtypeStruct((B, 1, H * HG_V), F32),
                   jax.ShapeDtypeStruct(s0_all.shape, F32)],
        input_output_aliases=aliases,
        compiler_params=_cparams("parallel"),
        name="hgrn_step",
    )(proj, s0_all, lb_logits, onorm.reshape(1, H * HG_V), *alias_arg)


def _cmp_sample_kernel(pt_ref, cache_ref, wbig_ref, wab_ref, pe_ref, w2_ref, o_ref, raw_ref, buf_ref, wb_ref,
                       sem, *, n_pages):
    b = pl.program_id(0)
    nb = pl.num_programs(0)
    G, hd = NSA_KV_GROUPS, NSA_HEAD_DIM
    hp = n_pages // 2
    spp = cache_ref.shape[3] // CMP_STRIDE
    ns = n_pages * spp

    def page_copy(bb, half, p):
        src = cache_ref.at[pt_ref[bb, half * hp + p], pl.ds(0, 2)]
        return pltpu.make_async_copy(src, raw_ref.at[half, p], sem.at[half])

    def start_half(bb, half):
        def body(p2, c):
            page_copy(bb, half, 2 * p2).start(priority=0)
            page_copy(bb, half, 2 * p2 + 1).start(priority=1)
            return c
        lax.fori_loop(0, hp // 2, body, 0)

    def wait_half(bb, half):
        def body(p, c):
            page_copy(bb, half, p).wait()
            return c
        lax.fori_loop(0, hp, body, 0)

    page = cache_ref.shape[3]
    r_i = lax.broadcasted_iota(jnp.int32, (page, page), 0)
    c_i = lax.broadcasted_iota(jnp.int32, (page, page), 1)
    perm = jnp.where(c_i == (r_i % spp) * CMP_STRIDE + r_i // spp, 1.0, 0.0).astype(BF16)

    def to_token_rows(half):
        def body(p, c):
            for kind in range(2):
                moved = _dot_nt(raw_ref[half, p, kind].astype(BF16), perm)
                for gp in range(G // 2):
                    buf_ref[kind * (G // 2) + gp, half * hp + p] = moved[gp * 2 * hd:(gp + 1) * 2 * hd].T
            return c
        lax.fori_loop(0, hp, body, 0, unroll=4)

    @pl.when(b == 0)
    def _():
        start_half(0, 0)
        start_half(0, 1)
        wb_ref[...] = wbig_ref[...].astype(BF16)

    for half in range(2):
        wait_half(b, half)
        to_token_rows(half)

        @pl.when(b + 1 < nb)
        def _():
            start_half(b + 1, half)

    for c in range(2):
        wab = wab_ref[c].astype(BF16)
        w2 = w2_ref[c].astype(BF16)
        for gp in range(G // 2):
            cb = c * (G // 2) + gp
            x = jnp.concatenate(
                [buf_ref[cb, :, l * spp:(l + 1) * spp, :].reshape(ns, 2 * hd).astype(BF16)
                 for l in range(CMP_STRIDE)], axis=1)
            ab2 = _dot(x, wb_ref[c])
            for gl in range(2):
                act = _cmp_hidden(ab2[:, gl * 2 * hd:(gl + 1) * 2 * hd], pe_ref[c], wab)
                col = (c * G + 2 * gp + gl) * hd
                o_ref[0, :, col:col + hd] = _dot(act.astype(BF16), w2).astype(o_ref.dtype)


def compress_sample(cache_t, page_table, wbig, wab, pe, w2):
    B, n_pages = page_table.shape
    page = cache_t.shape[3]
    G, hd = NSA_KV_GROUPS, NSA_HEAD_DIM
    ns = n_pages * page // CMP_STRIDE
    assert n_pages % 2 == 0 and page == 2 * hd
    const = lambda a: pl.BlockSpec(a.shape, lambda b, pt: (0,) * a.ndim)
    grid_spec = pltpu.PrefetchScalarGridSpec(
        num_scalar_prefetch=1,
        grid=(B,),
        in_specs=[pl.BlockSpec(memory_space=pl.ANY), const(wbig), const(wab), const(pe), const(w2)],
        out_specs=pl.BlockSpec((1, ns, 2 * G * hd), lambda b, pt: (b, 0, 0)),
        scratch_shapes=[pltpu.VMEM((2, n_pages // 2, 2, G * hd, page), F32),
                        pltpu.VMEM((G, n_pages, page, 2 * hd), F32),
                        pltpu.VMEM(wbig.shape, BF16),
                        pltpu.SemaphoreType.DMA((2,))],
    )
    return pl.pallas_call(
        functools.partial(_cmp_sample_kernel, n_pages=n_pages),
        grid_spec=grid_spec,
        out_shape=jax.ShapeDtypeStruct((B, ns, 2 * G * hd), BF16),
        compiler_params=_cparams("arbitrary"),
        name="compress_sample",
    )(page_table, cache_t, wbig, wab, pe, w2)


def _group_queries(pr_ref, g):
    hd = NSA_HEAD_DIM
    rows = [pr_ref[0, :, (g * NSA_HPG + h) * hd:(g * NSA_HPG + h + 1) * hd] for h in range(NSA_HPG)]
    return jnp.concatenate(rows, axis=0) * NSA_SCALE


def _nsa_sample_select_kernel(pr_ref, cmp_ref, oc_ref, idx_ref, *, t_pos, n_slc, n_pad):
    G, hpg, hd = NSA_KV_GROUPS, NSA_HPG, NSA_HEAD_DIM
    n_cmp = cmp_ref.shape[1]
    cmp = cmp_ref[0]
    ci = lax.broadcasted_iota(jnp.int32, (n_cmp, n_pad), 0) * CMP_STRIDE
    sj = lax.broadcasted_iota(jnp.int32, (n_cmp, n_pad), 1) * SLC_BLOCK
    ov = jnp.where((ci < sj + SLC_BLOCK) & (ci + CMP_LEN > sj), 1.0, 0.0).astype(BF16)
    blk = lax.broadcasted_iota(jnp.int32, (1, n_pad), 1)
    cur = t_pos // SLC_BLOCK
    forced = (blk == 0) | (blk == cur) | (blk == cur - 1)
    jr = lax.broadcasted_iota(jnp.int32, (n_pad, n_pad), 0)
    jc = lax.broadcasted_iota(jnp.int32, (n_pad, n_pad), 1)
    for g in range(G):
        qg = _group_queries(pr_ref, g).astype(BF16)
        sc = _dot_nt(qg, cmp[:, g * hd:(g + 1) * hd])
        e_pos = lax.broadcasted_iota(jnp.int32, (hpg, n_cmp), 1) * CMP_STRIDE + (CMP_LEN - 1)
        p_c = _masked_softmax(sc, e_pos <= t_pos)
        o_c = _dot(p_c.astype(BF16), cmp[:, (G + g) * hd:(G + g + 1) * hd])
        for h in range(hpg):
            col = (g * hpg + h) * hd
            oc_ref[0, :, col:col + hd] = o_c[h:h + 1]
        hi, mid, lo = _split3(jnp.sum(p_c, axis=0, keepdims=True))
        imp = _dot(hi, ov) + _dot(mid, ov) + _dot(lo, ov)
        score = jnp.where(forced, FORCED_SCORE, jnp.where(blk <= cur, imp, -1.0))
        score = jnp.where(blk < n_slc, score, -2.0)
        col_s = _row_to_col(score)
        beats = (col_s > score) | ((col_s == score) & (jr < jc))
        rank = jnp.sum(jnp.where(beats, 1.0, 0.0), axis=0, keepdims=True)
        rr = lax.broadcasted_iota(jnp.int32, (SLC_TOPK, n_pad), 0).astype(F32)
        bsel = jnp.where(jnp.broadcast_to(rank, (SLC_TOPK, n_pad)) == rr,
                         lax.broadcasted_iota(jnp.int32, (SLC_TOPK, n_pad), 1).astype(F32), 0.0)
        idx_ref[0, g * SLC_TOPK:(g + 1) * SLC_TOPK, :] = jnp.sum(bsel, axis=-1, keepdims=True).astype(jnp.int32)


def nsa_sample_select(proj, cmp_s, t_pos, n_slc):
    B = proj.shape[0]
    n_pad = -(-n_slc // 128) * 128
    G = NSA_KV_GROUPS
    nq = NSA_HEADS * NSA_HEAD_DIM
    return pl.pallas_call(
        functools.partial(_nsa_sample_select_kernel, t_pos=t_pos, n_slc=n_slc, n_pad=n_pad),
        grid=(B,),
        in_specs=[pl.BlockSpec((1, 1, proj.shape[2]), lambda b: (b, 0, 0)),
                  pl.BlockSpec((1,) + cmp_s.shape[1:], lambda b: (b, 0, 0))],
        out_specs=[pl.BlockSpec((1, 1, nq), lambda b: (b, 0, 0)),
                   pl.BlockSpec((1, G * SLC_TOPK, 1), lambda b: (b, 0, 0))],
        out_shape=[jax.ShapeDtypeStruct((B, 1, nq), F32),
                   jax.ShapeDtypeStruct((B, G * SLC_TOPK, 1), jnp.int32)],
        compiler_params=_cparams("parallel"),
        name="nsa_sample_select",
    )(proj, cmp_s)


def _nsa_sample_attend_kernel(pt_ref, idx_ref, pr_ref, oc_ref, kvn_ref, win_ref, cache_ref, o_ref,
                              kvbuf_ref, sem, *, t_pos, past_len):
    b = pl.program_id(0)
    nb = pl.num_programs(0)
    G, hpg, hd = NSA_KV_GROUPS, NSA_HPG, NSA_HEAD_DIM
    n_sel = G * SLC_TOPK
    page = cache_ref.shape[3]
    bpp = page // SLC_BLOCK
    new_blk = past_len // SLC_BLOCK

    def blk_copy(bb, n):
        slot = bb % 2
        j = jnp.minimum(idx_ref[bb, n], new_blk - 1)
        pg = pt_ref[bb, j // bpp]
        rows = pl.ds(pl.multiple_of((n // SLC_TOPK) * hd, hd), hd)
        return pltpu.make_async_copy(cache_ref.at[pg, pl.ds(2, 2), rows], kvbuf_ref.at[slot, n], sem.at[slot])

    def start_all(bb):
        def body(n2, c):
            blk_copy(bb, 2 * n2).start(priority=0)
            blk_copy(bb, 2 * n2 + 1).start(priority=1)
            return c
        lax.fori_loop(0, n_sel // 2, body, 0)

    def wait_all(bb):
        def body(n, c):
            blk_copy(bb, n).wait()
            return c
        lax.fori_loop(0, n_sel, body, 0)

    @pl.when(b == 0)
    def _():
        start_all(0)

    @pl.when(b + 1 < nb)
    def _():
        start_all(b + 1)

    wait_all(b)
    slot = b % 2

    nk = SLC_TOPK * page
    w_buf = win_ref.shape[3]
    nq = NSA_HEADS * hd
    gates = _sigmoid(pr_ref[0, :, nq:nq + GATE_LANES])
    kvn = kvn_ref[0]

    def new_row(kind, g):
        return kvn[:, (kind * G + g) * hd:(kind * G + g + 1) * hd].astype(BF16).astype(F32)

    def attend_with_new(qg, s, ok, vt, k_new, v_new):
        s_new = jnp.sum(qg.astype(F32) * k_new, axis=-1, keepdims=True)
        s = jnp.where(ok, s, NEG_INF)
        m = jnp.maximum(jnp.max(s, axis=-1, keepdims=True), s_new)
        e = jnp.where(ok, jnp.exp(s - m), 0.0)
        e_new = jnp.exp(s_new - m)
        den = jnp.maximum(jnp.sum(e, axis=-1, keepdims=True) + e_new, TINY)
        return (_dot_nt(e.astype(BF16), vt) + e_new * v_new) / den

    lane = lax.broadcasted_iota(jnp.int32, (1, nk), 1)
    for g in range(G):
        qg = _group_queries(pr_ref, g).astype(BF16)
        kt = jnp.concatenate([kvbuf_ref[slot, g * SLC_TOPK + r, 0] for r in range(SLC_TOPK)], axis=1).astype(BF16)
        vt = jnp.concatenate([kvbuf_ref[slot, g * SLC_TOPK + r, 1] for r in range(SLC_TOPK)], axis=1).astype(BF16)
        vis = jnp.zeros((1, nk), jnp.int32)
        for r in range(SLC_TOPK):
            j = idx_ref[b, g * SLC_TOPK + r]
            half = jnp.where(j < new_blk, j % bpp, -1)
            vis = jnp.where(lane // page == r, jnp.where((lane % page) // SLC_BLOCK == half, 1, 0), vis)
        ok = jnp.broadcast_to(vis > 0, (hpg, nk))
        o_s = attend_with_new(qg, _dot(qg, kt), ok, vt, new_row(2, g), new_row(3, g))
        kwt = win_ref[0, 0, g * hd:(g + 1) * hd, :].astype(BF16)
        vwt = win_ref[0, 1, g * hd:(g + 1) * hd, :].astype(BF16)
        wpos = past_len - w_buf + lax.broadcasted_iota(jnp.int32, (hpg, w_buf), 1)
        okw = (wpos <= t_pos) & (wpos > t_pos - WINDOW) & (wpos >= 0)
        o_w = attend_with_new(qg, _dot(qg, kwt), okw, vwt, new_row(4, g), new_row(5, g))
        for h in range(hpg):
            col = (g * hpg + h) * hd
            gc = g * GATE_ROWS + h
            o_h = (gates[:, gc:gc + 1] * oc_ref[0, :, col:col + hd]
                   + gates[:, gc + hpg:gc + hpg + 1] * o_s[h:h + 1]
                   + gates[:, gc + 2 * hpg:gc + 2 * hpg + 1] * o_w[h:h + 1])
            o_ref[0, :, col:col + hd] = o_h


def nsa_sample_attend(proj, o_c, kv_new, win_t, cache_t, page_table, idx, t_pos, past_len):
    B = proj.shape[0]
    G, hd = NSA_KV_GROUPS, NSA_HEAD_DIM
    nq = NSA_HEADS * hd
    page = cache_t.shape[3]
    row = lambda a: pl.BlockSpec((1, 1, a.shape[2]), lambda b, pt, ix: (b, 0, 0))
    grid_spec = pltpu.PrefetchScalarGridSpec(
        num_scalar_prefetch=2,
        grid=(B,),
        in_specs=[row(proj), row(o_c), row(kv_new),
                  pl.BlockSpec((1,) + win_t.shape[1:], lambda b, pt, ix: (b, 0, 0, 0)),
                  pl.BlockSpec(memory_space=pl.ANY)],
        out_specs=pl.BlockSpec((1, 1, nq), lambda b, pt, ix: (b, 0, 0)),
        scratch_shapes=[pltpu.VMEM((2, G * SLC_TOPK, 2, hd, page), F32),
                        pltpu.SemaphoreType.DMA((2,))],
    )
    return pl.pallas_call(
        functools.partial(_nsa_sample_attend_kernel, t_pos=t_pos, past_len=past_len),
        grid_spec=grid_spec,
        out_shape=jax.ShapeDtypeStruct((B, 1, nq), F32),
        compiler_params=_cparams("arbitrary"),
        name="nsa_sample_attend",
    )(page_table, idx, proj, o_c, kv_new, win_t, cache_t)


def kernel(x_prompt, x_sample, cache_nsa_kv, cache_win_kv, state_hgrn, page_table, norm_mix, norm_mlp, w_mlp_up, w_mlp_down, w_hgrn_in, hgrn_lb_logits, hgrn_onorm, w_hgrn_out, norm_kv, w_kv, cmp_pe_k, cmp_w1_k, cmp_w2_k, cmp_pe_v, cmp_w1_v, cmp_w2_v, w_nsa_q, w_nsa_out, norm_final):
    B, T, D = x_prompt.shape
    Bs, Ts, _ = x_sample.shape
    G, hd = NSA_KV_GROUPS, NSA_HEAD_DIM
    n_pool, page = cache_nsa_kv.shape[:2]
    past_len = page_table.shape[1] * page
    w_buf = cache_win_kv.shape[1]
    assert Ts == 1 and T % 1024 == 0 and T >= WINDOW + 256 and past_len % SLC_BLOCK == 0 and w_buf <= past_len

    wab, wbig, pe, w2, w2t = _cmp_weights(cmp_pe_k, cmp_w1_k, cmp_pe_v, cmp_w1_v, cmp_w2_k, cmp_w2_v)
    wq = [_permute_gate_cols(w_nsa_q[l]) for l in range(DEPTH - N_A)]

    tm = 1024
    x = x_prompt.reshape(B * T, D)
    states_p = jnp.zeros((N_A, B, HG_HEADS, HG_K, HG_V), F32)
    for l in range(DEPTH):
        if l == N_A:
            nsa_p, win_p, k_hm, vt_hm = kv_proj_prompt(x, norm_kv, w_kv, B, T)
            kc_p, vct_p = compress_prompt(nsa_p, wbig, wab, pe, w2, w2t)
        if l < N_A:
            proj = rms_proj(x, norm_mix[l], w_hgrn_in, 2 * tm, 512, layer=l).reshape(B, T, -1)
            o, states_p = hgrn_prompt(proj, hgrn_lb_logits, hgrn_onorm[l], l, states_p)
            x = proj_res(o.reshape(B * T, -1), w_hgrn_out, x, tm, layer=l)
        else:
            q_hm, gates_t = q_proj_prompt(x, norm_mix[l], wq[l - N_A], B, T)
            o = nsa_prompt(q_hm, gates_t, kc_p, vct_p, k_hm, vt_hm)
            x = proj_res(o.reshape(B * T, -1), w_nsa_out, x, tm, layer=l - N_A)
        x = mlp_res(x, norm_mlp[l], w_mlp_up, w_mlp_down, norm_final, tm, 1024, l == DEPTH - 1, l)
    y_prompt = x.reshape(B, T, D)
    nsa_kv_prompt = nsa_p.reshape(B, 4, G, hd, T).transpose(0, 4, 1, 2, 3)
    win_kv_prompt = win_p.reshape(B, 2, G, hd, T)[..., -min(WINDOW, T):].transpose(0, 4, 1, 2, 3)

    t_pos = past_len
    n_slc = -(-(past_len + 1) // SLC_BLOCK)
    xs = x_sample.reshape(Bs, D)
    cache_t = cache_nsa_kv.transpose(0, 2, 3, 4, 1).reshape(n_pool, 4, G * hd, page)
    win_t = cache_win_kv.transpose(0, 2, 3, 4, 1).reshape(Bs, 2, G * hd, w_buf)
    states_s = jnp.zeros(state_hgrn.shape, F32)
    for l in range(DEPTH):
        if l == N_A:
            kv_s = rms_proj(xs, norm_kv, w_kv, Bs, 512)
            cmp_s = compress_sample(cache_t, page_table, wbig, wab, pe, w2)
        if l < N_A:
            proj = rms_proj(xs, norm_mix[l], w_hgrn_in, Bs, 512, layer=l).reshape(Bs, 1, -1)
            o, states_s = hgrn_step(proj, state_hgrn, hgrn_lb_logits, hgrn_onorm[l], l, states_s)
        else:
            proj = rms_proj(xs, norm_mix[l], wq[l - N_A], Bs, 384).reshape(Bs, 1, -1)
            o_c, idx = nsa_sample_select(proj, cmp_s, t_pos, n_slc)
            o = nsa_sample_attend(proj, o_c, kv_s.reshape(Bs, 1, -1), win_t, cache_t, page_table,
                                  idx.reshape(Bs, G * SLC_TOPK), t_pos, past_len)
        w_o, lo = (w_hgrn_out, l) if l < N_A else (w_nsa_out, l - N_A)
        xs = proj_res(o.reshape(Bs, -1), w_o, xs, Bs, layer=lo)
        xs = mlp_res(xs, norm_mlp[l], w_mlp_up, w_mlp_down, norm_final, Bs, 512, l == DEPTH - 1, l)
    y_sample = xs.reshape(Bs, 1, D)
    n_nsa = 4 * G * hd
    nsa_kv_sample = kv_s[:, :n_nsa].reshape(Bs, 1, 4, G, hd)
    win_new = kv_s[:, n_nsa:].reshape(Bs, 1, 2, G, hd).astype(cache_win_kv.dtype)
    win_kv_sample = jnp.concatenate([cache_win_kv, win_new], axis=1)[:, -w_buf:]

    return (y_prompt, y_sample, nsa_kv_prompt, nsa_kv_sample, win_kv_prompt, win_kv_sample,
            states_p, states_s)
```
